```python
import jax, jax.numpy as jnp
from jax import lax
import numpy as np

D_MODEL = 2048
BATCH = 1
SEQ = 8192
DEPTH = 4

GRID_W = 64
CTX_LEN = 256

N_BRANCH = 4
BRANCH_W = 512
HEAD_DIM = 64
NA_HEADS = BRANCH_W // HEAD_DIM
NA_WIN_ROWS = 8
NA_WIN_COLS = 16
POOL_SIZES = (2, 4, 8, 16)
POOL_GROUP = BRANCH_W // len(POOL_SIZES)
GQA_Q_HEADS = BRANCH_W // HEAD_DIM
GQA_KV_HEADS = 2
Q_BLOCK = 128
ROPE_THETA = 10000.0
CONV_WIDTH = 31
EPS = 1e-6

PROJ_PARTS = (
    ("a_k", BRANCH_W), ("a_v", BRANCH_W),
    ("c_k", GQA_KV_HEADS * HEAD_DIM), ("c_v", GQA_KV_HEADS * HEAD_DIM),
    ("a_q", BRANCH_W), ("c_q", BRANCH_W),
    ("a_gate", BRANCH_W), ("b_in", BRANCH_W), ("b_gate", BRANCH_W),
    ("c_gate", BRANCH_W), ("d_glu", 2 * BRANCH_W), ("d_gate", BRANCH_W),
    ("merge", N_BRANCH * D_MODEL),
)
KV_COLS = 2 * BRANCH_W + 2 * GQA_KV_HEADS * HEAD_DIM
IN_COLS = sum(w for _, w in PROJ_PARTS)

kernel_name = "hybrid_parallel_gated_diffusion_block"


def split_proj(p):
    out = {}
    off = 0
    for name, w in PROJ_PARTS:
        if off + w > p.shape[-1]:
            break
        out[name] = p[..., off:off + w]
        off += w
    return out


def rms_norm(x, g):
    xf = x.astype(jnp.float32)
    y = xf * lax.rsqrt(jnp.mean(xf * xf, axis=-1, keepdims=True) + EPS)
    return (y * g).astype(x.dtype)


def layer_norm(x, g, b):
    xf = x.astype(jnp.float32)
    mu = jnp.mean(xf, axis=-1, keepdims=True)
    var = jnp.mean(jnp.square(xf - mu), axis=-1, keepdims=True)
    return ((xf - mu) * lax.rsqrt(var + EPS) * g + b).astype(x.dtype)


def modulate(x, g, shift, scale):
    return rms_norm(x, g) * (1 + scale) + shift


def _rope_axis(x, pos):
    half = x.shape[-1] // 2
    freqs = ROPE_THETA ** (-jnp.arange(half, dtype=jnp.float32) / half)
    ang = pos.astype(jnp.float32)[:, None] * freqs[None, :]
    cos = jnp.cos(ang)[:, None, :]
    sin = jnp.sin(ang)[:, None, :]
    xf = x.astype(jnp.float32)
    x1, x2 = xf[..., :half], xf[..., half:]
    return jnp.concatenate([x1 * cos - x2 * sin, x2 * cos + x1 * sin], axis=-1).astype(x.dtype)


def rope_2d(x, rows, cols):
    a = x.shape[-1] // 2
    return jnp.concatenate([_rope_axis(x[..., :a], rows), _rope_axis(x[..., a:], cols)], axis=-1)


def neighbourhood_attention(q, k, v, k_ctx, v_ctx, rpb):
    B, S, H, hd = q.shape
    R = S // GRID_W
    wr = min(NA_WIN_ROWS, R)
    r = jnp.arange(R)
    row_start = jnp.clip(r - wr // 2, 0, R - wr)
    band_rows = row_start[:, None] + jnp.arange(wr)[None, :]
    cq = jnp.arange(GRID_W)
    col_start = jnp.clip(cq - NA_WIN_COLS // 2, 0, GRID_W - NA_WIN_COLS)
    kc = jnp.arange(GRID_W)
    col_ok = (kc[None, :] >= col_start[:, None]) & (kc[None, :] < col_start[:, None] + NA_WIN_COLS)
    row_off = band_rows - r[:, None] + (NA_WIN_ROWS - 1)
    col_off = jnp.clip(kc[None, :] - cq[:, None], -(NA_WIN_COLS - 1), NA_WIN_COLS - 1) + (NA_WIN_COLS - 1)
    bias = rpb[:, row_off[:, None, :, None], col_off[None, :, None, :]].astype(jnp.float32)
    bias = jnp.where(col_ok[None, None, :, None, :], bias, -jnp.inf)
    scale = hd ** -0.5
    qg = q.reshape(B, R, GRID_W, H, hd)
    kg = k.reshape(B, R, GRID_W, H, hd)[:, band_rows]
    vg = v.reshape(B, R, GRID_W, H, hd)[:, band_rows]
    s_band = jnp.einsum('brqhd,brjkhd->bhrqjk', qg, kg).astype(jnp.float32) * scale + bias[None]
    s_ctx = jnp.einsum('brqhd,bchd->bhrqc', qg, k_ctx).astype(jnp.float32) * scale
    nb = wr * GRID_W
    s = jnp.concatenate([s_band.reshape(B, H, R, GRID_W, nb), s_ctx], axis=-1)
    p = jax.nn.softmax(s, axis=-1).astype(v.dtype)
    p_band = p[..., :nb].reshape(B, H, R, GRID_W, wr, GRID_W)
    p_ctx = p[..., nb:]
    o = jnp.einsum('bhrqjk,brjkhd->brqhd', p_band, vg) + jnp.einsum('bhrqc,bchd->brqhd', p_ctx, v_ctx)
    return o.reshape(B, S, H, hd)


def dense_attention(q, k, v):
    B, L, Hq, hd = q.shape
    Hk = k.shape[2]
    qg = q.reshape(B, L, Hk, Hq // Hk, hd)
    s = jnp.einsum('bqkgd,bnkd->bkgqn', qg, k).astype(jnp.float32) * (hd ** -0.5)
    p = jax.nn.softmax(s, axis=-1).astype(v.dtype)
    return jnp.einsum('bkgqn,bnkd->bqkgd', p, v).reshape(B, L, Hq, hd)


def gqa_latent_attention(q, k, v, k_ctx, v_ctx):
    B, S, Hq, hd = q.shape
    k_all = jnp.concatenate([k, k_ctx], axis=1)
    v_all = jnp.concatenate([v, v_ctx], axis=1)
    nblk = S // Q_BLOCK
    qb = jnp.moveaxis(q.reshape(B, nblk, Q_BLOCK, Hq, hd), 1, 0)
    o = lax.map(lambda qi: dense_attention(qi, k_all, v_all), qb)
    return jnp.moveaxis(o, 0, 1).reshape(B, S, Hq, hd)


def multiscale_pool(u, w_pool, pool_scale):
    B, L, W = u.shape
    uf = u.astype(jnp.float32)
    csum = jnp.concatenate([jnp.zeros((B, 1, W), jnp.float32), jnp.cumsum(uf, axis=1)], axis=1)
    t = jnp.arange(L)
    outs = []
    for gi, ksz in enumerate(POOL_SIZES):
        sl = slice(gi * POOL_GROUP, (gi + 1) * POOL_GROUP)
        lo = jnp.clip(t - ksz // 2, 0, L - 1)
        hi = jnp.clip(t + ksz - 1 - ksz // 2, 0, L - 1)
        cg = csum[..., sl]
        mean = (cg[:, hi + 1] - cg[:, lo]) / (hi - lo + 1).astype(jnp.float32)[None, :, None]
        d = (mean - uf[..., sl]).astype(u.dtype)
        outs.append(d @ w_pool[gi])
    return jnp.concatenate(outs, axis=-1) * pool_scale


def conformer_conv(glu_in, conv_w, conv_b, ln_g, ln_b, w_pw):
    a, g = jnp.split(glu_in, 2, axis=-1)
    u = a * jax.nn.sigmoid(g)
    y = lax.conv_general_dilated(
        u, conv_w[:, None, :].astype(u.dtype), window_strides=(1,),
        padding=((CONV_WIDTH // 2, CONV_WIDTH // 2),),
        dimension_numbers=('NWC', 'WIO', 'NWC'), feature_group_count=BRANCH_W) + conv_b
    y = jax.nn.silu(layer_norm(y, ln_g, ln_b))
    return y @ w_pw


def branch_merge(p, o_a, o_b, o_c, o_d, w_branch, w_out):
    outs = (o_a * jax.nn.silu(p['a_gate']), o_b * jax.nn.silu(p['b_gate']),
            o_c * jax.nn.silu(p['c_gate']), o_d * jax.nn.silu(p['d_gate']))
    gates = jax.nn.sigmoid(p['merge'])
    y = 0
    for bi in range(N_BRANCH):
        y = y + gates[..., bi * D_MODEL:(bi + 1) * D_MODEL] * (outs[bi] @ w_branch[bi])
    return y @ w_out


def setup_inputs(seed: int = 0) -> dict:
    key = jax.random.key(seed)
    ks = jax.random.split(key, 24)
    f = jnp.float32
    nrm = lambda k, shape, s: jax.random.normal(k, shape, f) * s
    return {
        "x": nrm(ks[0], (BATCH, SEQ, D_MODEL), 1.0),
        "c": nrm(ks[1], (BATCH, D_MODEL), 1.0),
        "ctx": nrm(ks[2], (BATCH, CTX_LEN, D_MODEL), 1.0),
        "c_ctx": nrm(ks[3], (D_MODEL,), 1.0),
        "w_ada": nrm(ks[4], (DEPTH, D_MODEL, 3 * D_MODEL), D_MODEL ** -0.5),
        "b_ada": nrm(ks[5], (DEPTH, 3 * D_MODEL), 0.01),
        "g_pre": 1.0 + nrm(ks[6], (DEPTH, D_MODEL), 0.05),
        "g_post": 1.0 + nrm(ks[7], (DEPTH, D_MODEL), 0.05),
        "w_in": nrm(ks[8], (DEPTH, D_MODEL, IN_COLS), D_MODEL ** -0.5),
        "na_rpb": nrm(ks[9], (DEPTH, NA_HEADS, 2 * NA_WIN_ROWS - 1, 2 * NA_WIN_COLS - 1), 0.1),
        "pool_w": nrm(ks[10], (DEPTH, len(POOL_SIZES), POOL_GROUP, POOL_GROUP), POOL_GROUP ** -0.5),
        "pool_scale": 1.0 + nrm(ks[11], (DEPTH, BRANCH_W), 0.1),
        "q_norm": 1.0 + nrm(ks[12], (DEPTH, HEAD_DIM), 0.05),
        "k_norm": 1.0 + nrm(ks[13], (DEPTH, HEAD_DIM), 0.05),
        "conv_w": nrm(ks[14], (DEPTH, CONV_WIDTH, BRANCH_W), CONV_WIDTH ** -0.5),
        "conv_b": nrm(ks[15], (DEPTH, BRANCH_W), 0.01),
        "conv_ln_g": 1.0 + nrm(ks[16], (DEPTH, BRANCH_W), 0.05),
        "conv_ln_b": nrm(ks[17], (DEPTH, BRANCH_W), 0.01),
        "conv_pw": nrm(ks[18], (DEPTH, BRANCH_W, BRANCH_W), BRANCH_W ** -0.5),
        "w_branch": nrm(ks[19], (DEPTH, N_BRANCH, BRANCH_W, D_MODEL), BRANCH_W ** -0.5),
        "w_out": nrm(ks[20], (DEPTH, D_MODEL, D_MODEL), D_MODEL ** -0.5),
    }


def reference(x, c, ctx, c_ctx, w_ada, b_ada, g_pre, g_post, w_in, na_rpb, pool_w, pool_scale,
              q_norm, k_norm, conv_w, conv_b, conv_ln_g, conv_ln_b, conv_pw, w_branch, w_out):
    B, S, _ = x.shape
    Cn = ctx.shape[1]
    t = jnp.arange(S)
    rows = t // GRID_W
    cols = t % GRID_W
    hd = HEAD_DIM
    for l in range(DEPTH):
        last = l == DEPTH - 1
        shift, scale, gate = jnp.split(jax.nn.silu(c) @ w_ada[l] + b_ada[l], 3, axis=-1)
        shift_c, scale_c, gate_c = jnp.split(jax.nn.silu(c_ctx) @ w_ada[l] + b_ada[l], 3, axis=-1)
        h = modulate(x, g_pre[l], shift[:, None], scale[:, None])
        hc = modulate(ctx, g_pre[l], shift_c, scale_c)
        p = split_proj(h @ w_in[l])
        pc = split_proj(hc @ (w_in[l, :, :KV_COLS] if last else w_in[l]))
        ka_c = pc['a_k'].reshape(B, Cn, NA_HEADS, hd)
        va_c = pc['a_v'].reshape(B, Cn, NA_HEADS, hd)
        kc_c = rms_norm(pc['c_k'].reshape(B, Cn, GQA_KV_HEADS, hd), k_norm[l])
        vc_c = pc['c_v'].reshape(B, Cn, GQA_KV_HEADS, hd)
        o_a = neighbourhood_attention(p['a_q'].reshape(B, S, NA_HEADS, hd), p['a_k'].reshape(B, S, NA_HEADS, hd),
                                      p['a_v'].reshape(B, S, NA_HEADS, hd), ka_c, va_c, na_rpb[l])
        o_b = multiscale_pool(p['b_in'], pool_w[l], pool_scale[l])
        qc = rope_2d(rms_norm(p['c_q'].reshape(B, S, GQA_Q_HEADS, hd), q_norm[l]), rows, cols)
        kc = rope_2d(rms_norm(p['c_k'].reshape(B, S, GQA_KV_HEADS, hd), k_norm[l]), rows, cols)
        vc = p['c_v'].reshape(B, S, GQA_KV_HEADS, hd)
        o_c = gqa_latent_attention(qc, kc, vc, kc_c, vc_c)
        o_d = conformer_conv(p['d_glu'], conv_w[l], conv_b[l], conv_ln_g[l], conv_ln_b[l], conv_pw[l])
        y = branch_merge(p, o_a.reshape(B, S, BRANCH_W), o_b, o_c.reshape(B, S, BRANCH_W), o_d,
                         w_branch[l], w_out[l])
        x_next = x + gate[:, None] * rms_norm(y, g_post[l])
        if not last:
            o_a_c = dense_attention(pc['a_q'].reshape(B, Cn, NA_HEADS, hd), ka_c, va_c)
            o_b_c = multiscale_pool(pc['b_in'], pool_w[l], pool_scale[l])
            qc_c = rms_norm(pc['c_q'].reshape(B, Cn, GQA_Q_HEADS, hd), q_norm[l])
            o_c_c = dense_attention(qc_c, kc_c, vc_c)
            o_d_c = conformer_conv(pc['d_glu'], conv_w[l], conv_b[l], conv_ln_g[l], conv_ln_b[l], conv_pw[l])
            y_c = branch_merge(pc, o_a_c.reshape(B, Cn, BRANCH_W), o_b_c, o_c_c.reshape(B, Cn, BRANCH_W),
                               o_d_c, w_branch[l], w_out[l])
            ctx = ctx + gate_c * rms_norm(y_c, g_post[l])
        x = x_next
    return x
```

```python
import functools

import numpy as np
import jax
import jax.numpy as jnp
from jax import lax
from jax.experimental import pallas as pl
from jax.experimental.pallas import tpu as pltpu

F32 = jnp.float32
BF16 = jnp.bfloat16

GRID_W = 64
HEAD_DIM = 64
BRANCH_W = 512
N_BRANCH = 4
N_HEADS = BRANCH_W // HEAD_DIM
GQA_KV_HEADS = 2
KV_W = GQA_KV_HEADS * HEAD_DIM
NA_WIN_ROWS = 8
NA_WIN_COLS = 16
POOL_SIZES = (2, 4, 8, 16)
POOL_GROUP = BRANCH_W // len(POOL_SIZES)
ROPE_THETA = 10000.0
CONV_WIDTH = 31
EPS = 1e-6

LANES = 128
HALO = 16
NA_ROW_BLOCK = 4
NA_BAND_ROWS = 12
NEG_BIG = -1e30
V7X_VMEM_LIMIT = 52 * 1024 * 1024

PARTS = (("merge", None), ("d_glu", 2 * BRANCH_W), ("a_q", BRANCH_W), ("c_q", BRANCH_W),
         ("a_gate", BRANCH_W), ("b_gate", BRANCH_W), ("c_gate", BRANCH_W), ("d_gate", BRANCH_W),
         ("b_in", BRANCH_W), ("a_k", BRANCH_W), ("a_v", BRANCH_W), ("c_k", KV_W), ("c_v", KV_W))
SRC_PARTS = ("a_k", "a_v", "c_k", "c_v", "a_q", "c_q", "a_gate", "b_in", "b_gate", "c_gate",
             "d_glu", "d_gate", "merge")


def _layout(d_model):
    width = {n: (N_BRANCH * d_model if w is None else w) for n, w in PARTS}
    off, o = {}, 0
    for n, _ in PARTS:
        off[n] = o
        o += width[n]
    src, o = {}, 0
    for n in SRC_PARTS:
        src[n] = o
        o += width[n]
    return width, off, src, o


def _params(**kw):
    return pltpu.CompilerParams(vmem_limit_bytes=V7X_VMEM_LIMIT, **kw)


def _tile(n, pref, mult):
    if n <= pref:
        return n
    t = (pref // mult) * mult
    while t >= mult:
        if n % t == 0:
            return t
        t -= mult
    raise ValueError(f"no tile for {n}")


def _resident(block_shape, index_map):
    return pl.BlockSpec(block_shape, index_map, pipeline_mode=pl.Buffered(1))


def _slab(tm, width, off):
    assert off % width == 0
    return pl.BlockSpec((tm, width), lambda i, _c=off // width: (i, _c))


def _silu(x):
    return x * jax.nn.sigmoid(x)


def _ada_body(cs_ref, w_ref, b_ref, o_ref):
    s = _silu(cs_ref[...]).astype(BF16)
    o_ref[0] = jnp.dot(s, w_ref[0].astype(BF16), preferred_element_type=F32) + b_ref[0]


def _ada(cs, w_ada, b_ada):
    depth, d, n = w_ada.shape
    tn = _tile(n, 1024, LANES)
    return pl.pallas_call(
        _ada_body,
        grid=(depth, n // tn),
        in_specs=[pl.BlockSpec((8, d), lambda l, j: (0, 0)),
                  pl.BlockSpec((1, d, tn), lambda l, j: (l, 0, j)),
                  pl.BlockSpec((1, 1, tn), lambda l, j: (l, 0, j))],
        out_specs=pl.BlockSpec((1, 8, tn), lambda l, j: (l, 0, j)),
        out_shape=jax.ShapeDtypeStruct((depth, 8, n), F32),
        compiler_params=_params(dimension_semantics=("arbitrary", "arbitrary")),
        name="ada",
    )(cs, w_ada, b_ada.reshape(depth, 1, n))


def _modulate_body(x_ref, g_ref, sh_ref, sc_ref, o_ref):
    x = x_ref[...]
    y = x * lax.rsqrt(jnp.mean(x * x, axis=-1, keepdims=True) + EPS) * g_ref[...]
    o_ref[...] = (y * (1.0 + sc_ref[...]) + sh_ref[...]).astype(BF16)


def _modulate(x, g, shift, scale):
    t, d = x.shape
    tm = _tile(t, 512, 16)
    vec = pl.BlockSpec((1, d), lambda i: (0, 0))
    return pl.pallas_call(
        _modulate_body,
        grid=(t // tm,),
        in_specs=[pl.BlockSpec((tm, d), lambda i: (i, 0)), vec, vec, vec],
        out_specs=pl.BlockSpec((tm, d), lambda i: (i, 0)),
        out_shape=jax.ShapeDtypeStruct((t, d), BF16),
        compiler_params=_params(dimension_semantics=("arbitrary",)),
        name="modulate",
    )(x, g, shift, scale)


def _matmul_body(h_ref, w_ref, o_ref):
    o_ref[...] = jnp.dot(h_ref[...], w_ref[...], preferred_element_type=F32).astype(o_ref.dtype)


def _col_tile(n):
    for mult in (2 * LANES, LANES):
        try:
            return _tile(n, 1536, mult)
        except ValueError:
            pass
    raise ValueError(n)


def _matmul(h, w):
    t, d = h.shape
    n = w.shape[1]
    tm = _tile(t, 1024, 16)
    tn = _col_tile(n)
    return pl.pallas_call(
        _matmul_body,
        grid=(t // tm, n // tn),
        in_specs=[pl.BlockSpec((tm, d), lambda i, j: (i, 0)),
                  pl.BlockSpec((d, tn), lambda i, j: (0, j))],
        out_specs=pl.BlockSpec((tm, tn), lambda i, j: (i, j)),
        out_shape=jax.ShapeDtypeStruct((t, n), BF16),
        compiler_params=_params(dimension_semantics=("arbitrary", "arbitrary")),
        name="in_proj",
    )(h, w)


def _head_meansq(x, ones_bd):
    ss = x * x
    hi = ss.astype(BF16)
    lo = (ss - hi.astype(F32)).astype(BF16)
    tot = (jnp.dot(hi, ones_bd, preferred_element_type=F32)
           + jnp.dot(lo, ones_bd, preferred_element_type=F32))
    return tot * (1.0 / HEAD_DIM)


def _rope(y, cos, sin_signed):
    w = y.shape[-1]
    lane = lax.broadcasted_iota(jnp.int32, y.shape, 1)
    nxt = pltpu.roll(y, w - 16, 1)
    prv = pltpu.roll(y, 16, 1)
    return y * cos + jnp.where((lane % 32) < 16, nxt, prv) * sin_signed


def _norm_rope(x, w, bd, rope):
    x = x * lax.rsqrt(_head_meansq(x, bd) + EPS) * w
    if rope is not None:
        reps = x.shape[-1] // LANES
        x = _rope(x, jnp.concatenate([rope[0]] * reps, axis=1), jnp.concatenate([rope[1]] * reps, axis=1))
    return x


def _qk_prep_body(*refs, use_rope, with_q):
    refs = list(refs)
    q_ref = refs.pop(0) if with_q else None
    k_ref = refs.pop(0)
    qn_ref = refs.pop(0) if with_q else None
    kn_ref, bd_ref = refs.pop(0), refs.pop(0)
    rope = (refs.pop(0)[...], refs.pop(0)[...]) if use_rope else None
    bd = bd_ref[...]
    if with_q:
        q = _norm_rope(q_ref[...].astype(F32), qn_ref[...], bd, rope)
        refs.pop(0)[...] = (q * HEAD_DIM ** -0.5).astype(BF16)
    k = _norm_rope(k_ref[...].astype(F32), kn_ref[...], bd[:KV_W, :KV_W], rope)
    refs.pop(0)[...] = k.astype(BF16)


def _qk_prep(p, q_off, k_off, qn, kn, ones_bd, rope_tabs):
    t = p.shape[0]
    tm = _tile(t, 512, 16)
    use_rope, with_q = rope_tabs is not None, q_off is not None
    const = lambda shape: pl.BlockSpec(shape, lambda i: (0, 0))
    in_specs, args, out_specs, out_shape = [], [], [], []
    if with_q:
        in_specs.append(_slab(tm, BRANCH_W, q_off))
        args.append(p)
    in_specs.append(_slab(tm, KV_W, k_off))
    args.append(p)
    if with_q:
        in_specs.append(const((1, BRANCH_W)))
        args.append(qn)
        out_specs.append(pl.BlockSpec((tm, BRANCH_W), lambda i: (i, 0)))
        out_shape.append(jax.ShapeDtypeStruct((t, BRANCH_W), BF16))
    in_specs += [const((1, KV_W)), const((BRANCH_W, BRANCH_W))]
    args += [kn, ones_bd]
    if use_rope:
        in_specs += [pl.BlockSpec((tm, LANES), lambda i: (i, 0))] * 2
        args += list(rope_tabs)
    out_specs.append(pl.BlockSpec((tm, KV_W), lambda i: (i, 0)))
    out_shape.append(jax.ShapeDtypeStruct((t, KV_W), BF16))
    res = pl.pallas_call(
        functools.partial(_qk_prep_body, use_rope=use_rope, with_q=with_q),
        grid=(t // tm,),
        in_specs=in_specs,
        out_specs=out_specs,
        out_shape=out_shape,
        compiler_params=_params(dimension_semantics=("arbitrary",)),
        name="qk_prep",
    )(*args)
    return res if with_q else (None, res[0])


def _rope_tables(seq):
    t = jnp.arange(seq)
    half = HEAD_DIM // 4
    freqs = ROPE_THETA ** (-jnp.arange(half, dtype=F32) / half)
    ang_r = (t // GRID_W).astype(F32)[:, None] * freqs[None, :]
    ang_c = (t % GRID_W).astype(F32)[:, None] * freqs[None, :]
    cos = jnp.concatenate([jnp.cos(ang_r)] * 2 + [jnp.cos(ang_c)] * 2, axis=1)
    sin = jnp.concatenate([-jnp.sin(ang_r), jnp.sin(ang_r), -jnp.sin(ang_c), jnp.sin(ang_c)], axis=1)
    reps = LANES // HEAD_DIM
    return jnp.concatenate([cos] * reps, axis=1), jnp.concatenate([sin] * reps, axis=1)


def _nt_dot(a, b):
    return lax.dot_general(a, b, (((1,), (1,)), ((), ())), preferred_element_type=F32)


def _pair_queries(qp, low, shared_kv_lanes):
    zero = jnp.zeros_like(qp)
    if shared_kv_lanes is None:
        return jnp.where(low, qp, zero), jnp.where(low, zero, qp)
    qr = pltpu.roll(qp, HEAD_DIM, 1)
    if shared_kv_lanes == 0:
        return jnp.where(low, qp, zero), jnp.where(low, qr, zero)
    return jnp.where(low, zero, qr), jnp.where(low, zero, qp)


def _dense_attn_body(*refs, n_src, chunks, kv_grouped, q_scale, tq):
    q_ref, gate_ref = refs[0], refs[1]
    srcs = [(refs[2 + 2 * i], refs[3 + 2 * i]) for i in range(n_src)]
    o_ref = refs[2 + 2 * n_src]
    m_s, l_s, acc_s = refs[3 + 2 * n_src:]
    low_q = lax.broadcasted_iota(jnp.int32, (tq, LANES), 1) < HEAD_DIM

    for p in range(N_HEADS // 2):
        cols = slice(LANES * p, LANES * (p + 1))
        qp = q_ref[:, cols].astype(F32) * q_scale
        if kv_grouped:
            kv_half = (2 * p) // (N_HEADS // GQA_KV_HEADS)
            kcols = slice(0, LANES)
        else:
            kv_half = None
            kcols = cols
        qa, qb = _pair_queries(qp, low_q, kv_half)
        lhs = jnp.concatenate([qa, qb], axis=0).astype(BF16)
        m_s[...] = jnp.full(m_s.shape, NEG_BIG, F32)
        l_s[...] = jnp.zeros(l_s.shape, F32)
        acc_s[...] = jnp.zeros(acc_s.shape, F32)

        def step(kc, vc, kv_half=kv_half, lhs=lhs):
            low_v = lax.broadcasted_iota(jnp.int32, vc.shape, 1) < HEAD_DIM
            zero = jnp.zeros_like(vc)
            if kv_half is None:
                va, vb = jnp.where(low_v, vc, zero), jnp.where(low_v, zero, vc)
            elif kv_half == 0:
                va = vb = jnp.where(low_v, vc, zero)
            else:
                va = vb = jnp.where(low_v, zero, vc)
            s = _nt_dot(lhs, kc)
            m_old = m_s[...]
            m_new = jnp.maximum(m_old, jnp.max(s, axis=-1, keepdims=True))
            alpha = jnp.exp(m_old - m_new)
            pr = jnp.exp(s - m_new)
            l_s[...] = alpha * l_s[...] + jnp.sum(pr, axis=-1, keepdims=True)
            pb = pr.astype(BF16)
            acc_s[:tq] = alpha[:tq] * acc_s[:tq] + jnp.dot(pb[:tq], va, preferred_element_type=F32)
            acc_s[tq:] = alpha[tq:] * acc_s[tq:] + jnp.dot(pb[tq:], vb, preferred_element_type=F32)
            m_s[...] = m_new

        for (k_ref, v_ref), ck in zip(srcs, chunks):
            n_chunk = k_ref.shape[0] // ck
            if n_chunk == 1:
                step(k_ref[:, kcols], v_ref[:, kcols])
            else:
                def loop(i, carry, k_ref=k_ref, v_ref=v_ref, ck=ck, kcols=kcols, step=step):
                    r = pl.ds(pl.multiple_of(i * ck, ck), ck)
                    step(k_ref[r, kcols], v_ref[r, kcols])
                    return carry
                lax.fori_loop(0, n_chunk, loop, 0)

        oa = acc_s[:tq] / l_s[:tq]
        ob = acc_s[tq:] / l_s[tq:]
        if kv_half is None:
            o = jnp.where(low_q, oa, ob)
        elif kv_half == 0:
            o = jnp.where(low_q, oa, pltpu.roll(ob, HEAD_DIM, 1))
        else:
            o = jnp.where(low_q, pltpu.roll(oa, HEAD_DIM, 1), ob)
        o_ref[:, cols] = (o * _silu(gate_ref[:, cols].astype(F32))).astype(BF16)


def _dense_attn(q, q_off, gate, gate_off, kv_srcs, kv_grouped, q_scale):
    t = q.shape[0]
    tq = _tile(t, 256, 16)
    kw = KV_W if kv_grouped else BRANCH_W
    in_specs = [_slab(tq, BRANCH_W, q_off), _slab(tq, BRANCH_W, gate_off)]
    args = [q, gate]
    chunks = []
    for k_arr, k_off, v_arr, v_off in kv_srcs:
        tk = k_arr.shape[0]
        assert k_off % kw == 0 and v_off % kw == 0
        in_specs += [_resident((tk, kw), lambda i, _c=k_off // kw: (0, _c)),
                     _resident((tk, kw), lambda i, _c=v_off // kw: (0, _c))]
        args += [k_arr, v_arr]
        chunks.append(_tile(tk, 1024, 16))
    return pl.pallas_call(
        functools.partial(_dense_attn_body, n_src=len(kv_srcs), chunks=tuple(chunks),
                          kv_grouped=kv_grouped, q_scale=q_scale, tq=tq),
        grid=(t // tq,),
        in_specs=in_specs,
        out_specs=pl.BlockSpec((tq, BRANCH_W), lambda i: (i, 0)),
        out_shape=jax.ShapeDtypeStruct((t, BRANCH_W), BF16),
        scratch_shapes=[pltpu.VMEM((2 * tq, 1), F32), pltpu.VMEM((2 * tq, 1), F32),
                        pltpu.VMEM((2 * tq, LANES), F32)],
        compiler_params=_params(dimension_semantics=("arbitrary",)),
        name="dense_attn",
    )(*args)


def _na_block_geometry(n_rows):
    rb, band = NA_ROW_BLOCK, NA_BAND_ROWS
    assert n_rows % rb == 0 and n_rows >= band and band >= rb + NA_WIN_ROWS - 1
    nb = n_rows // rb
    r0 = np.arange(nb) * rb
    u0 = np.clip(r0 - NA_WIN_ROWS // 2, 0, n_rows - band)
    r = r0[:, None] + np.arange(rb)[None, :]
    row_start = np.clip(r - NA_WIN_ROWS // 2, 0, n_rows - NA_WIN_ROWS)
    key_row = u0[:, None] + np.arange(band)[None, :]
    rel = key_row[:, None, :] - row_start[:, :, None]
    valid = (rel >= 0) & (rel < NA_WIN_ROWS)
    row_off = np.where(valid, key_row[:, None, :] - r[:, :, None] + NA_WIN_ROWS - 1, 0)
    return u0, row_off, valid


def _na_bias_tables(rpb, n_rows):
    u0, row_off, valid = _na_block_geometry(n_rows)
    nb = row_off.shape[0]
    keys = [row_off[b].tobytes() + valid[b].tobytes() for b in range(nb)]
    cases, case_of = [], []
    for b in range(nb):
        if keys[b] not in [keys[c] for c in cases]:
            cases.append(b)
        case_of.append([keys[c] for c in cases].index(keys[b]))
    assert case_of == [0] + [1] * (nb - 2) + [2], case_of
    cq = np.arange(GRID_W)
    col_start = np.clip(cq - NA_WIN_COLS // 2, 0, GRID_W - NA_WIN_COLS)
    col_ok = (cq[None, :] >= col_start[:, None]) & (cq[None, :] < col_start[:, None] + NA_WIN_COLS)
    col_off = np.clip(cq[None, :] - cq[:, None], -(NA_WIN_COLS - 1), NA_WIN_COLS - 1) + NA_WIN_COLS - 1
    ro = row_off[cases]
    ok = valid[cases][:, :, None, :, None] & col_ok[None, None, :, None, :]
    bias = rpb.astype(F32)[:, ro[:, :, None, :, None], col_off[None, None, :, None, :]]
    bias = jnp.where(ok[None], bias, NEG_BIG)
    bias = jnp.transpose(bias, (1, 0, 2, 3, 4, 5))
    return bias.reshape(3, N_HEADS * NA_ROW_BLOCK * GRID_W, NA_BAND_ROWS * GRID_W)


def _na_body(q_ref, gate_ref, k_ref, v_ref, kc_ref, vc_ref, bias_ref, o_ref, *, n_rows):
    rb, band = NA_ROW_BLOCK, NA_BAND_ROWS
    tq, nk = rb * GRID_W, band * GRID_W
    b = pl.program_id(0)
    u0 = jnp.clip(b * rb - NA_WIN_ROWS // 2, 0, n_rows - band)
    rows = pl.ds(pl.multiple_of(u0 * GRID_W, GRID_W), nk)
    low_q = lax.broadcasted_iota(jnp.int32, (tq, LANES), 1) < HEAD_DIM
    low_b = lax.broadcasted_iota(jnp.int32, (nk, LANES), 1) < HEAD_DIM
    low_c = lax.broadcasted_iota(jnp.int32, (kc_ref.shape[0], LANES), 1) < HEAD_DIM
    for p in range(N_HEADS // 2):
        cols = slice(LANES * p, LANES * (p + 1))
        qp = q_ref[:, cols].astype(F32) * HEAD_DIM ** -0.5
        qa, qb = _pair_queries(qp, low_q, None)
        lhs = jnp.concatenate([qa, qb], axis=0).astype(BF16)
        kb, vb = k_ref[rows, cols], v_ref[rows, cols]
        kc, vc = kc_ref[:, cols], vc_ref[:, cols]
        s_band = _nt_dot(lhs, kb) + bias_ref[0, 2 * tq * p:2 * tq * (p + 1), :]
        s_ctx = _nt_dot(lhs, kc)
        m = jnp.maximum(jnp.max(s_band, axis=-1, keepdims=True), jnp.max(s_ctx, axis=-1, keepdims=True))
        p_band = jnp.exp(s_band - m)
        p_ctx = jnp.exp(s_ctx - m)
        l = jnp.sum(p_band, axis=-1, keepdims=True) + jnp.sum(p_ctx, axis=-1, keepdims=True)
        pb, pc = p_band.astype(BF16), p_ctx.astype(BF16)
        zb, zc = jnp.zeros_like(vb), jnp.zeros_like(vc)
        acc_a = (jnp.dot(pb[:tq], jnp.where(low_b, vb, zb), preferred_element_type=F32)
                 + jnp.dot(pc[:tq], jnp.where(low_c, vc, zc), preferred_element_type=F32))
        acc_b = (jnp.dot(pb[tq:], jnp.where(low_b, zb, vb), preferred_element_type=F32)
                 + jnp.dot(pc[tq:], jnp.where(low_c, zc, vc), preferred_element_type=F32))
        o = jnp.where(low_q, acc_a / l[:tq], acc_b / l[tq:])
        o_ref[:, cols] = (o * _silu(gate_ref[:, cols].astype(F32))).astype(BF16)


def _na_attn(p, off, pc, a_k_ctx_off, a_v_ctx_off, bias):
    s = p.shape[0]
    cn = pc.shape[0]
    n_rows = s // GRID_W
    tq = NA_ROW_BLOCK * GRID_W
    nb = n_rows // NA_ROW_BLOCK
    w = BRANCH_W
    return pl.pallas_call(
        functools.partial(_na_body, n_rows=n_rows),
        grid=(nb,),
        in_specs=[_slab(tq, w, off["a_q"]), _slab(tq, w, off["a_gate"]),
                  _resident((s, w), lambda i, _c=off["a_k"] // w: (0, _c)),
                  _resident((s, w), lambda i, _c=off["a_v"] // w: (0, _c)),
                  _resident((cn, w), lambda i, _c=a_k_ctx_off // w: (0, _c)),
                  _resident((cn, w), lambda i, _c=a_v_ctx_off // w: (0, _c)),
                  pl.BlockSpec((1,) + bias.shape[1:],
                               lambda i: (jnp.where(i == 0, 0, jnp.where(i == nb - 1, 2, 1)), 0, 0))],
        out_specs=pl.BlockSpec((tq, w), lambda i: (i, 0)),
        out_shape=jax.ShapeDtypeStruct((s, w), BF16),
        compiler_params=_params(dimension_semantics=("arbitrary",)),
        name="na_attn",
    )(p, p, p, p, pc, pc, bias)


def _halo_specs(t, tm, width, off):
    assert off % width == 0 and tm % HALO == 0
    c = off // width
    per, last = tm // HALO, t // HALO - 1
    return [pl.BlockSpec((tm, width), lambda i: (i, c)),
            pl.BlockSpec((HALO, width), lambda i: (jnp.maximum(i * per - 1, 0), c)),
            pl.BlockSpec((HALO, width), lambda i: (jnp.minimum((i + 1) * per, last), c))]


def _fill_padded(pad_ref, cur, prev, nxt, tm):
    i, n = pl.program_id(0), pl.num_programs(0)
    pad_ref[0:HALO] = jnp.where(i > 0, prev, jnp.zeros_like(prev))
    pad_ref[HALO:HALO + tm] = cur
    pad_ref[HALO + tm:] = jnp.where(i < n - 1, nxt, jnp.zeros_like(nxt))


def _pool_body(u_ref, up_ref, un_ref, gate_ref, w_ref, sc_ref, o_ref, pad_ref, *, tm, seq):
    _fill_padded(pad_ref, u_ref[...].astype(F32), up_ref[...].astype(F32), un_ref[...].astype(F32), tm)
    t = pl.program_id(0) * tm + lax.broadcasted_iota(jnp.int32, (tm, POOL_GROUP), 0)
    for gi, ksz in enumerate(POOL_SIZES):
        cols = slice(POOL_GROUP * gi, POOL_GROUP * (gi + 1))
        back = ksz // 2
        tot = pad_ref[HALO - back:HALO - back + tm, cols]
        for d in range(1 - back, ksz - back):
            tot = tot + pad_ref[HALO + d:HALO + d + tm, cols]
        lo = jnp.maximum(t - back, 0)
        hi = jnp.minimum(t + (ksz - 1 - back), seq - 1)
        mean = tot / (hi - lo + 1).astype(F32)
        dlt = (mean - pad_ref[HALO:HALO + tm, cols]).astype(BF16)
        y = jnp.dot(dlt, w_ref[gi], preferred_element_type=F32) * sc_ref[:, cols]
        o_ref[:, cols] = (y * _silu(gate_ref[:, cols].astype(F32))).astype(BF16)


def _pool(p, off, w_pool, pool_scale):
    t = p.shape[0]
    tm = _tile(t, 512, HALO)
    w = BRANCH_W
    return pl.pallas_call(
        functools.partial(_pool_body, tm=tm, seq=t),
        grid=(t // tm,),
        in_specs=_halo_specs(t, tm, w, off["b_in"]) + [
            _slab(tm, w, off["b_gate"]),
            pl.BlockSpec(w_pool.shape, lambda i: (0, 0, 0)),
            pl.BlockSpec((1, w), lambda i: (0, 0))],
        out_specs=pl.BlockSpec((tm, w), lambda i: (i, 0)),
        out_shape=jax.ShapeDtypeStruct((t, w), BF16),
        scratch_shapes=[pltpu.VMEM((tm + 2 * HALO, w), F32)],
        compiler_params=_params(dimension_semantics=("arbitrary",)),
        name="pool",
    )(p, p, p, p, w_pool, pool_scale)


def _glu(x):
    x = x.astype(F32)
    return x[:, :BRANCH_W] * jax.nn.sigmoid(x[:, BRANCH_W:])


def _conv_body(x_ref, xp_ref, xn_ref, gate_ref, cw_ref, cb_ref, lg_ref, lb_ref, pw_ref, o_ref, pad_ref, *, tm):
    _fill_padded(pad_ref, _glu(x_ref[...]), _glu(xp_ref[...]), _glu(xn_ref[...]), tm)
    reach = CONV_WIDTH // 2
    y = jnp.zeros((tm, BRANCH_W), F32) + cb_ref[...]
    for j in range(CONV_WIDTH):
        s = HALO - reach + j
        y = y + pad_ref[s:s + tm, :] * cw_ref[j:j + 1, :]
    mu = jnp.mean(y, axis=-1, keepdims=True)
    yc = y - mu
    var = jnp.mean(yc * yc, axis=-1, keepdims=True)
    z = _silu(yc * lax.rsqrt(var + EPS) * lg_ref[...] + lb_ref[...]).astype(BF16)
    out = jnp.dot(z, pw_ref[...], preferred_element_type=F32)
    o_ref[...] = (out * _silu(gate_ref[...].astype(F32))).astype(BF16)


def _conv(p, off, conv_w, conv_b, ln_g, ln_b, w_pw):
    t = p.shape[0]
    tm = _tile(t, 512, HALO)
    w = BRANCH_W
    vec = pl.BlockSpec((1, w), lambda i: (0, 0))
    return pl.pallas_call(
        functools.partial(_conv_body, tm=tm),
        grid=(t // tm,),
        in_specs=_halo_specs(t, tm, 2 * w, off["d_glu"]) + [
            _slab(tm, w, off["d_gate"]),
            pl.BlockSpec(conv_w.shape, lambda i: (0, 0)), vec, vec, vec,
            pl.BlockSpec((w, w), lambda i: (0, 0))],
        out_specs=pl.BlockSpec((tm, w), lambda i: (i, 0)),
        out_shape=jax.ShapeDtypeStruct((t, w), BF16),
        scratch_shapes=[pltpu.VMEM((tm + 2 * HALO, w), F32)],
        compiler_params=_params(dimension_semantics=("arbitrary",)),
        name="conv",
    )(p, p, p, p, conv_w, conv_b, ln_g, ln_b, w_pw)


def _merge_body(m0, m1, m2, m3, o0, o1, o2, o3, wb_ref, y_ref):
    y = None
    for bi, (m_ref, o_ref) in enumerate(((m0, o0), (m1, o1), (m2, o2), (m3, o3))):
        proj = jnp.dot(o_ref[...], wb_ref[bi], preferred_element_type=F32)
        term = jax.nn.sigmoid(m_ref[...].astype(F32)) * proj
        y = term if y is None else y + term
    y_ref[...] = y.astype(BF16)


def _merge(p, merge_off, outs, w_branch):
    t = p.shape[0]
    d = w_branch.shape[-1]
    tm = _tile(t, 256, 16)
    return pl.pallas_call(
        _merge_body,
        grid=(t // tm,),
        in_specs=[_slab(tm, d, merge_off + bi * d) for bi in range(N_BRANCH)]
        + [pl.BlockSpec((tm, BRANCH_W), lambda i: (i, 0))] * N_BRANCH
        + [_resident(w_branch.shape, lambda i: (0, 0, 0))],
        out_specs=pl.BlockSpec((tm, d), lambda i: (i, 0)),
        out_shape=jax.ShapeDtypeStruct((t, d), BF16),
        compiler_params=_params(dimension_semantics=("arbitrary",)),
        name="merge",
    )(p, p, p, p, *outs, w_branch)


def _out_body(y_ref, w_ref, x_ref, g_ref, gate_ref, o_ref):
    y = jnp.dot(y_ref[...], w_ref[...], preferred_element_type=F32)
    yn = y * lax.rsqrt(jnp.mean(y * y, axis=-1, keepdims=True) + EPS) * g_ref[...]
    o_ref[...] = x_ref[...] + gate_ref[...] * yn


def _out_proj(y, w_out, x, g_post, gate):
    t, d = x.shape
    tm = _tile(t, 256, 16)
    vec = pl.BlockSpec((1, d), lambda i: (0, 0))
    row = pl.BlockSpec((tm, d), lambda i: (i, 0))
    return pl.pallas_call(
        _out_body,
        grid=(t // tm,),
        in_specs=[row, _resident((d, d), lambda i: (0, 0)), row, vec, vec],
        out_specs=row,
        out_shape=jax.ShapeDtypeStruct((t, d), F32),
        compiler_params=_params(dimension_semantics=("arbitrary",)),
        name="out_proj",
    )(y, w_out, x, g_post, gate)


def _permute_cast_w_in(w_in_l, d_model):
    width, _, src, _ = _layout(d_model)
    return jnp.concatenate([w_in_l[:, src[n]:src[n] + width[n]] for n, _ in PARTS], axis=1).astype(BF16)


def kernel(x, c, ctx, c_ctx, w_ada, b_ada, g_pre, g_post, w_in, na_rpb, pool_w, pool_scale,
           q_norm, k_norm, conv_w, conv_b, conv_ln_g, conv_ln_b, conv_pw, w_branch, w_out):
    batch, seq, d = x.shape
    assert batch == 1 and seq % GRID_W == 0
    cn = ctx.shape[1]
    depth = w_ada.shape[0]
    width, off, _, n_cols = _layout(d)
    kv_off = off["a_k"]
    kv_rel = {n: off[n] - kv_off for n in ("a_k", "a_v", "c_k", "c_v")}

    xs, cs = x[0], ctx[0]
    cvec = jnp.zeros((8, d), F32).at[0].set(c[0]).at[1].set(c_ctx)
    ada = _ada(cvec, w_ada, b_ada)

    ones_bd = jnp.asarray(np.kron(np.eye(N_HEADS), np.ones((HEAD_DIM, HEAD_DIM))), BF16)
    rope_tabs = _rope_tables(seq)
    row = lambda v: v.reshape(1, -1)
    qn_all = jnp.tile(q_norm, (1, N_HEADS))
    kn_all = jnp.tile(k_norm, (1, GQA_KV_HEADS))

    for l in range(depth):
        last = l == depth - 1
        w_l = _permute_cast_w_in(w_in[l], d)
        wb_l = w_branch[l].astype(BF16)
        wo_l = w_out[l].astype(BF16)
        pw_l = pool_w[l].astype(BF16)
        cpw_l = conv_pw[l].astype(BF16)
        mod = lambda r: (ada[l, r:r + 1, :d], ada[l, r:r + 1, d:2 * d], ada[l, r:r + 1, 2 * d:])
        (shift, scale, gate), (shift_c, scale_c, gate_c) = mod(0), mod(1)
        qn, kn = row(qn_all[l]), row(kn_all[l])

        hc = _modulate(cs, row(g_pre[l]), shift_c, scale_c)
        if last:
            pc = _matmul(hc, w_l[:, kv_off:])
            coff = kv_rel
        else:
            pc = _matmul(hc, w_l)
            coff = off
        qc_ctx, kq_ctx = _qk_prep(pc, None if last else coff["c_q"], coff["c_k"], qn, kn, ones_bd, None)

        h = _modulate(xs, row(g_pre[l]), shift, scale)
        p = _matmul(h, w_l)
        qc, kc = _qk_prep(p, off["c_q"], off["c_k"], qn, kn, ones_bd, rope_tabs)
        o_c = _dense_attn(qc, 0, p, off["c_gate"],
                          [(kc, 0, p, off["c_v"]), (kq_ctx, 0, pc, coff["c_v"])], True, 1.0)
        o_a = _na_attn(p, off, pc, coff["a_k"], coff["a_v"], _na_bias_tables(na_rpb[l], seq // GRID_W))
        o_b = _pool(p, off, pw_l, row(pool_scale[l]))
        o_d = _conv(p, off, conv_w[l], row(conv_b[l]), row(conv_ln_g[l]), row(conv_ln_b[l]), cpw_l)
        y = _merge(p, off["merge"], (o_a, o_b, o_c, o_d), wb_l)
        xs_next = _out_proj(y, wo_l, xs, row(g_post[l]), gate)

        if not last:
            o_a_c = _dense_attn(pc, off["a_q"], pc, off["a_gate"],
                                [(pc, off["a_k"], pc, off["a_v"])], False, HEAD_DIM ** -0.5)
            o_c_c = _dense_attn(qc_ctx, 0, pc, off["c_gate"], [(kq_ctx, 0, pc, off["c_v"])], True, 1.0)
            o_b_c = _pool(pc, off, pw_l, row(pool_scale[l]))
            o_d_c = _conv(pc, off, conv_w[l], row(conv_b[l]), row(conv_ln_g[l]), row(conv_ln_b[l]), cpw_l)
            y_c = _merge(pc, off["merge"], (o_a_c, o_b_c, o_c_c, o_d_c), wb_l)
            cs = _out_proj(y_c, wo_l, cs, row(g_post[l]), gate_c)
        xs = xs_next
    return xs[None]
```

```python
import functools

import numpy as np
import jax
import jax.numpy as jnp
from jax import lax
from jax.experimental import pallas as pl
from jax.experimental.pallas import tpu as pltpu

F32 = jnp.float32
BF16 = jnp.bfloat16

GRID_W = 64
HEAD_DIM = 64
BRANCH_W = 512
N_BRANCH = 4
N_HEADS = BRANCH_W // HEAD_DIM
GQA_KV_HEADS = 2
KV_W = GQA_KV_HEADS * HEAD_DIM
NA_WIN_ROWS = 8
NA_WIN_COLS = 16
POOL_SIZES = (2, 4, 8, 16)
POOL_GROUP = BRANCH_W // len(POOL_SIZES)
ROPE_THETA = 10000.0
CONV_WIDTH = 31
EPS = 1e-6

LANES = 128
HALO = 16
NA_ROW_BLOCK = 4
NA_BAND_ROWS = 12
NEG_BIG = -1e30
LOG2_E = 1.4426950408889634
SCORE_SCALE_LOG2 = HEAD_DIM ** -0.5 * LOG2_E
V7X_VMEM_LIMIT = 52 * 1024 * 1024

PARTS = (("merge", None), ("d_glu", 2 * BRANCH_W), ("a_q", BRANCH_W), ("c_q", BRANCH_W),
         ("a_gate", BRANCH_W), ("b_gate", BRANCH_W), ("c_gate", BRANCH_W), ("d_gate", BRANCH_W),
         ("b_in", BRANCH_W), ("a_k", BRANCH_W), ("a_v", BRANCH_W), ("c_k", KV_W), ("c_v", KV_W))
SRC_PARTS = ("a_k", "a_v", "c_k", "c_v", "a_q", "c_q", "a_gate", "b_in", "b_gate", "c_gate",
             "d_glu", "d_gate", "merge")


def _layout(d_model):
    width = {n: (N_BRANCH * d_model if w is None else w) for n, w in PARTS}
    off, o = {}, 0
    for n, _ in PARTS:
        off[n] = o
        o += width[n]
    src, o = {}, 0
    for n in SRC_PARTS:
        src[n] = o
        o += width[n]
    return width, off, src, o


def _params(**kw):
    return pltpu.CompilerParams(vmem_limit_bytes=V7X_VMEM_LIMIT, **kw)


def _tile(n, pref, mult):
    if n <= pref:
        return n
    t = (pref // mult) * mult
    while t >= mult:
        if n % t == 0:
            return t
        t -= mult
    raise ValueError(f"no tile for {n}")


def _resident(block_shape, index_map):
    return pl.BlockSpec(block_shape, index_map, pipeline_mode=pl.Buffered(1))


def _slab(tm, width, off):
    assert off % width == 0
    return pl.BlockSpec((tm, width), lambda i, _c=off // width: (i, _c))


def _silu(x):
    return x * jax.nn.sigmoid(x)


def _ada_body(cs_ref, w_ref, b_ref, o_ref):
    s = _silu(cs_ref[...]).astype(BF16)
    o_ref[0] = jnp.dot(s, w_ref[0].astype(BF16), preferred_element_type=F32) + b_ref[0]


def _ada(cs, w_ada, b_ada):
    depth, d, n = w_ada.shape
    tn = _tile(n, 1024, LANES)
    return pl.pallas_call(
        _ada_body,
        grid=(depth, n // tn),
        in_specs=[pl.BlockSpec((8, d), lambda l, j: (0, 0)),
                  pl.BlockSpec((1, d, tn), lambda l, j: (l, 0, j)),
                  pl.BlockSpec((1, 1, tn), lambda l, j: (l, 0, j))],
        out_specs=pl.BlockSpec((1, 8, tn), lambda l, j: (l, 0, j)),
        out_shape=jax.ShapeDtypeStruct((depth, 8, n), F32),
        compiler_params=_params(dimension_semantics=("arbitrary", "arbitrary")),
        name="ada",
    )(cs, w_ada, b_ada.reshape(depth, 1, n))


def _modulate_body(x_ref, g_ref, sh_ref, sc_ref, o_ref):
    x = x_ref[...]
    y = x * lax.rsqrt(jnp.mean(x * x, axis=-1, keepdims=True) + EPS) * g_ref[...]
    o_ref[...] = (y * (1.0 + sc_ref[...]) + sh_ref[...]).astype(BF16)


def _modulate(x, g, shift, scale):
    t, d = x.shape
    tm = _tile(t, 512, 16)
    vec = pl.BlockSpec((1, d), lambda i: (0, 0))
    return pl.pallas_call(
        _modulate_body,
        grid=(t // tm,),
        in_specs=[pl.BlockSpec((tm, d), lambda i: (i, 0)), vec, vec, vec],
        out_specs=pl.BlockSpec((tm, d), lambda i: (i, 0)),
        out_shape=jax.ShapeDtypeStruct((t, d), BF16),
        compiler_params=_params(dimension_semantics=("arbitrary",)),
        name="modulate",
    )(x, g, shift, scale)


def _matmul_body(h_ref, w_ref, o_ref):
    o_ref[...] = jnp.dot(h_ref[...], w_ref[...], preferred_element_type=F32).astype(o_ref.dtype)


def _col_tile(n):
    for mult in (2 * LANES, LANES):
        try:
            return _tile(n, 1536, mult)
        except ValueError:
            pass
    raise ValueError(n)


def _matmul(h, w):
    t, d = h.shape
    n = w.shape[1]
    tm = _tile(t, 1024, 16)
    tn = _col_tile(n)
    return pl.pallas_call(
        _matmul_body,
        grid=(t // tm, n // tn),
        in_specs=[pl.BlockSpec((tm, d), lambda i, j: (i, 0)),
                  pl.BlockSpec((d, tn), lambda i, j: (0, j))],
        out_specs=pl.BlockSpec((tm, tn), lambda i, j: (i, j)),
        out_shape=jax.ShapeDtypeStruct((t, n), BF16),
        compiler_params=_params(dimension_semantics=("arbitrary", "arbitrary")),
        name="in_proj",
    )(h, w)


def _head_meansq(x, ones_bd):
    ss = x * x
    hi = ss.astype(BF16)
    lo = (ss - hi.astype(F32)).astype(BF16)
    tot = (jnp.dot(hi, ones_bd, preferred_element_type=F32)
           + jnp.dot(lo, ones_bd, preferred_element_type=F32))
    return tot * (1.0 / HEAD_DIM)


def _rope(y, cos, sin_signed):
    w = y.shape[-1]
    lane = lax.broadcasted_iota(jnp.int32, y.shape, 1)
    nxt = pltpu.roll(y, w - 16, 1)
    prv = pltpu.roll(y, 16, 1)
    return y * cos + jnp.where((lane % 32) < 16, nxt, prv) * sin_signed


def _norm_rope(x, w, bd, rope):
    x = x * lax.rsqrt(_head_meansq(x, bd) + EPS) * w
    if rope is not None:
        reps = x.shape[-1] // LANES
        x = _rope(x, jnp.concatenate([rope[0]] * reps, axis=1), jnp.concatenate([rope[1]] * reps, axis=1))
    return x


def _qk_prep_body(*refs, use_rope, with_q):
    refs = list(refs)
    q_ref = refs.pop(0) if with_q else None
    k_ref = refs.pop(0)
    qn_ref = refs.pop(0) if with_q else None
    kn_ref, bd_ref = refs.pop(0), refs.pop(0)
    rope = (refs.pop(0)[...], refs.pop(0)[...]) if use_rope else None
    bd = bd_ref[...]
    if with_q:
        q = _norm_rope(q_ref[...].astype(F32), qn_ref[...], bd, rope)
        refs.pop(0)[...] = (q * SCORE_SCALE_LOG2).astype(BF16)
    k = _norm_rope(k_ref[...].astype(F32), kn_ref[...], bd[:KV_W, :KV_W], rope)
    refs.pop(0)[...] = k.astype(BF16)


def _qk_prep(p, q_off, k_off, qn, kn, ones_bd, rope_tabs):
    t = p.shape[0]
    tm = _tile(t, 512, 16)
    use_rope, with_q = rope_tabs is not None, q_off is not None
    const = lambda shape: pl.BlockSpec(shape, lambda i: (0, 0))
    in_specs, args, out_specs, out_shape = [], [], [], []
    if with_q:
        in_specs.append(_slab(tm, BRANCH_W, q_off))
        args.append(p)
    in_specs.append(_slab(tm, KV_W, k_off))
    args.append(p)
    if with_q:
        in_specs.append(const((1, BRANCH_W)))
        args.append(qn)
        out_specs.append(pl.BlockSpec((tm, BRANCH_W), lambda i: (i, 0)))
        out_shape.append(jax.ShapeDtypeStruct((t, BRANCH_W), BF16))
    in_specs += [const((1, KV_W)), const((BRANCH_W, BRANCH_W))]
    args += [kn, ones_bd]
    if use_rope:
        in_specs += [pl.BlockSpec((tm, LANES), lambda i: (i, 0))] * 2
        args += list(rope_tabs)
    out_specs.append(pl.BlockSpec((tm, KV_W), lambda i: (i, 0)))
    out_shape.append(jax.ShapeDtypeStruct((t, KV_W), BF16))
    res = pl.pallas_call(
        functools.partial(_qk_prep_body, use_rope=use_rope, with_q=with_q),
        grid=(t // tm,),
        in_specs=in_specs,
        out_specs=out_specs,
        out_shape=out_shape,
        compiler_params=_params(dimension_semantics=("arbitrary",)),
        name="qk_prep",
    )(*args)
    return res if with_q else (None, res[0])


def _rope_tables(seq):
    t = jnp.arange(seq)
    half = HEAD_DIM // 4
    freqs = ROPE_THETA ** (-jnp.arange(half, dtype=F32) / half)
    ang_r = (t // GRID_W).astype(F32)[:, None] * freqs[None, :]
    ang_c = (t % GRID_W).astype(F32)[:, None] * freqs[None, :]
    cos = jnp.concatenate([jnp.cos(ang_r)] * 2 + [jnp.cos(ang_c)] * 2, axis=1)
    sin = jnp.concatenate([-jnp.sin(ang_r), jnp.sin(ang_r), -jnp.sin(ang_c), jnp.sin(ang_c)], axis=1)
    reps = LANES // HEAD_DIM
    return jnp.concatenate([cos] * reps, axis=1), jnp.concatenate([sin] * reps, axis=1)


def _nt_dot(a, b):
    return lax.dot_general(a, b, (((1,), (1,)), ((), ())), preferred_element_type=F32)


def _pair_queries(qp, low, shared_kv_lanes):
    zero = jnp.zeros_like(qp)
    if shared_kv_lanes is None:
        return jnp.where(low, qp, zero), jnp.where(low, zero, qp)
    qr = pltpu.roll(qp, HEAD_DIM, 1)
    if shared_kv_lanes == 0:
        return jnp.where(low, qp, zero), jnp.where(low, qr, zero)
    return jnp.where(low, zero, qr), jnp.where(low, zero, qp)


def _values_with_ones(vc, half):
    lane = lax.broadcasted_iota(jnp.int32, vc.shape, 1)
    keep = (lane < HEAD_DIM) if half == 0 else (lane >= HEAD_DIM)
    ones_lane = HEAD_DIM * (1 - half)
    fill = jnp.where(lane == ones_lane, 1.0, 0.0)
    return jnp.where(keep, vc.astype(F32), fill).astype(vc.dtype)


def _dense_attn_body(*refs, n_src, chunks, kv_grouped, q_scale, tq):
    n_pair = N_HEADS // 2
    q_ref, gate_ref = refs[0], refs[1]
    srcs = [(refs[2 + 2 * i], refs[3 + 2 * i]) for i in range(n_src)]
    o_ref = refs[2 + 2 * n_src]
    scratch = refs[3 + 2 * n_src:]
    m_s, acc_s = scratch[:n_pair], scratch[n_pair:]
    low_q = lax.broadcasted_iota(jnp.int32, (tq, LANES), 1) < HEAD_DIM

    halves = [((2 * p) // (N_HEADS // GQA_KV_HEADS),) * 2 if kv_grouped else (0, 1) for p in range(n_pair)]
    lhs = []
    for p in range(n_pair):
        qp = q_ref[:, LANES * p:LANES * (p + 1)].astype(F32) * q_scale
        qa, qb = _pair_queries(qp, low_q, halves[p][0] if kv_grouped else None)
        lhs.append(jnp.concatenate([qa, qb], axis=0).astype(BF16))
        m_s[p][...] = jnp.full(m_s[p].shape, NEG_BIG, F32)
        acc_s[p][...] = jnp.zeros(acc_s[p].shape, F32)

    def step(p, kc, va, vb):
        s = _nt_dot(lhs[p], kc)
        m_old = m_s[p][...]
        m_new = jnp.maximum(m_old, jnp.max(s, axis=-1, keepdims=True))
        alpha = jnp.exp2(m_old - m_new)
        pr = jnp.exp2(s - m_new).astype(BF16)
        acc = acc_s[p]
        acc[:tq] = alpha[:tq] * acc[:tq] + jnp.dot(pr[:tq], va, preferred_element_type=F32)
        acc[tq:] = alpha[tq:] * acc[tq:] + jnp.dot(pr[tq:], vb, preferred_element_type=F32)
        m_s[p][...] = m_new

    def all_pairs(k_ref, v_ref, rows):
        if kv_grouped:
            kc, vc = k_ref[rows, :], v_ref[rows, :]
            vals = [_values_with_ones(vc, h) for h in range(GQA_KV_HEADS)]
        for p in range(n_pair):
            if kv_grouped:
                step(p, kc, vals[halves[p][0]], vals[halves[p][1]])
            else:
                cols = slice(LANES * p, LANES * (p + 1))
                vc = v_ref[rows, cols]
                step(p, k_ref[rows, cols], _values_with_ones(vc, 0), _values_with_ones(vc, 1))

    for (k_ref, v_ref), ck in zip(srcs, chunks):
        n_chunk = k_ref.shape[0] // ck
        if n_chunk == 1:
            all_pairs(k_ref, v_ref, slice(None))
        else:
            def loop(i, carry, k_ref=k_ref, v_ref=v_ref, ck=ck):
                all_pairs(k_ref, v_ref, pl.ds(pl.multiple_of(i * ck, ck), ck))
                return carry
            lax.fori_loop(0, n_chunk, loop, 0)

    for p in range(n_pair):
        cols = slice(LANES * p, LANES * (p + 1))
        heads = []
        for hd, half in enumerate(halves[p]):
            acc = acc_s[p][hd * tq:(hd + 1) * tq]
            ones_lane = HEAD_DIM * (1 - half)
            o = acc / acc[:, ones_lane:ones_lane + 1]
            heads.append(o if half == hd else pltpu.roll(o, HEAD_DIM, 1))
        o = jnp.where(low_q, heads[0], heads[1])
        o_ref[:, cols] = (o * _silu(gate_ref[:, cols].astype(F32))).astype(BF16)


def _key_chunk(tk):
    for mult in (2 * LANES, LANES, 16):
        try:
            return _tile(tk, 1024, mult)
        except ValueError:
            pass
    raise ValueError(tk)


def _dense_attn(q, q_off, gate, gate_off, kv_srcs, kv_grouped, q_scale):
    t = q.shape[0]
    tq = _tile(t, 256, 16)
    kw = KV_W if kv_grouped else BRANCH_W
    in_specs = [_slab(tq, BRANCH_W, q_off), _slab(tq, BRANCH_W, gate_off)]
    args = [q, gate]
    chunks = []
    for k_arr, k_off, v_arr, v_off in kv_srcs:
        tk = k_arr.shape[0]
        assert k_off % kw == 0 and v_off % kw == 0
        in_specs += [_resident((tk, kw), lambda i, _c=k_off // kw: (0, _c)),
                     _resident((tk, kw), lambda i, _c=v_off // kw: (0, _c))]
        args += [k_arr, v_arr]
        chunks.append(_key_chunk(tk))
    n_pair = N_HEADS // 2
    return pl.pallas_call(
        functools.partial(_dense_attn_body, n_src=len(kv_srcs), chunks=tuple(chunks),
                          kv_grouped=kv_grouped, q_scale=q_scale, tq=tq),
        grid=(t // tq,),
        in_specs=in_specs,
        out_specs=pl.BlockSpec((tq, BRANCH_W), lambda i: (i, 0)),
        out_shape=jax.ShapeDtypeStruct((t, BRANCH_W), BF16),
        scratch_shapes=[pltpu.VMEM((2 * tq, 1), F32)] * n_pair + [pltpu.VMEM((2 * tq, LANES), F32)] * n_pair,
        compiler_params=_params(dimension_semantics=("arbitrary",)),
        name="dense_attn",
    )(*args)


def _na_block_geometry(n_rows):
    rb, band = NA_ROW_BLOCK, NA_BAND_ROWS
    assert n_rows % rb == 0 and n_rows >= band and band >= rb + NA_WIN_ROWS - 1
    nb = n_rows // rb
    r0 = np.arange(nb) * rb
    u0 = np.clip(r0 - NA_WIN_ROWS // 2, 0, n_rows - band)
    r = r0[:, None] + np.arange(rb)[None, :]
    row_start = np.clip(r - NA_WIN_ROWS // 2, 0, n_rows - NA_WIN_ROWS)
    key_row = u0[:, None] + np.arange(band)[None, :]
    rel = key_row[:, None, :] - row_start[:, :, None]
    valid = (rel >= 0) & (rel < NA_WIN_ROWS)
    row_off = np.where(valid, key_row[:, None, :] - r[:, :, None] + NA_WIN_ROWS - 1, 0)
    return u0, row_off, valid


def _na_bias_tables(rpb, n_rows):
    u0, row_off, valid = _na_block_geometry(n_rows)
    nb = row_off.shape[0]
    keys = [row_off[b].tobytes() + valid[b].tobytes() for b in range(nb)]
    cases, case_of = [], []
    for b in range(nb):
        if keys[b] not in [keys[c] for c in cases]:
            cases.append(b)
        case_of.append([keys[c] for c in cases].index(keys[b]))
    assert case_of == [0] + [1] * (nb - 2) + [2], case_of
    cq = np.arange(GRID_W)
    col_start = np.clip(cq - NA_WIN_COLS // 2, 0, GRID_W - NA_WIN_COLS)
    col_ok = (cq[None, :] >= col_start[:, None]) & (cq[None, :] < col_start[:, None] + NA_WIN_COLS)
    col_off = np.clip(cq[None, :] - cq[:, None], -(NA_WIN_COLS - 1), NA_WIN_COLS - 1) + NA_WIN_COLS - 1
    onehot = jnp.asarray(col_off[:, :, None] == np.arange(2 * NA_WIN_COLS - 1), F32)
    tiles = jnp.einsum("hrm,qkm->hrqk", rpb.astype(F32), onehot, precision=lax.Precision.HIGHEST)
    tiles = jnp.where(col_ok[None, None], tiles, NEG_BIG)
    masked = jnp.full((N_HEADS, GRID_W, GRID_W), NEG_BIG, F32)
    blocks = []
    for c in cases:
        rows = [jnp.concatenate([tiles[:, row_off[c, i, j]] if valid[c, i, j] else masked
                                 for j in range(NA_BAND_ROWS)], axis=-1) for i in range(NA_ROW_BLOCK)]
        blocks.append(jnp.stack(rows, axis=1))
    bias = jnp.stack(blocks, axis=0)
    return bias.reshape(3, N_HEADS * NA_ROW_BLOCK * GRID_W, NA_BAND_ROWS * GRID_W)


def _na_body(q_ref, gate_ref, k_ref, v_ref, kc_ref, vc_ref, bias_ref, o_ref, *, n_rows):
    rb, band = NA_ROW_BLOCK, NA_BAND_ROWS
    tq, nk = rb * GRID_W, band * GRID_W
    b = pl.program_id(0)
    u0 = jnp.clip(b * rb - NA_WIN_ROWS // 2, 0, n_rows - band)
    rows = pl.ds(pl.multiple_of(u0 * GRID_W, GRID_W), nk)
    low_q = lax.broadcasted_iota(jnp.int32, (tq, LANES), 1) < HEAD_DIM
    low_b = lax.broadcasted_iota(jnp.int32, (nk, LANES), 1) < HEAD_DIM
    low_c = lax.broadcasted_iota(jnp.int32, (kc_ref.shape[0], LANES), 1) < HEAD_DIM
    for p in range(N_HEADS // 2):
        cols = slice(LANES * p, LANES * (p + 1))
        qp = q_ref[:, cols].astype(F32) * HEAD_DIM ** -0.5
        qa, qb = _pair_queries(qp, low_q, None)
        lhs = jnp.concatenate([qa, qb], axis=0).astype(BF16)
        kb, vb = k_ref[rows, cols], v_ref[rows, cols]
        kc, vc = kc_ref[:, cols], vc_ref[:, cols]
        s_band = _nt_dot(lhs, kb) + bias_ref[0, 2 * tq * p:2 * tq * (p + 1), :]
        s_ctx = _nt_dot(lhs, kc)
        m = jnp.maximum(jnp.max(s_band, axis=-1, keepdims=True), jnp.max(s_ctx, axis=-1, keepdims=True))
        p_band = jnp.exp(s_band - m)
        p_ctx = jnp.exp(s_ctx - m)
        l = jnp.sum(p_band, axis=-1, keepdims=True) + jnp.sum(p_ctx, axis=-1, keepdims=True)
        pb, pc = p_band.astype(BF16), p_ctx.astype(BF16)
        zb, zc = jnp.zeros_like(vb), jnp.zeros_like(vc)
        acc_a = (jnp.dot(pb[:tq], jnp.where(low_b, vb, zb), preferred_element_type=F32)
                 + jnp.dot(pc[:tq], jnp.where(low_c, vc, zc), preferred_element_type=F32))
        acc_b = (jnp.dot(pb[tq:], jnp.where(low_b, zb, vb), preferred_element_type=F32)
                 + jnp.dot(pc[tq:], jnp.where(low_c, zc, vc), preferred_element_type=F32))
        o = jnp.where(low_q, acc_a / l[:tq], acc_b / l[tq:])
        o_ref[:, cols] = (o * _silu(gate_ref[:, cols].astype(F32))).astype(BF16)


def _na_attn(p, off, pc, a_k_ctx_off, a_v_ctx_off, bias):
    s = p.shape[0]
    cn = pc.shape[0]
    n_rows = s // GRID_W
    tq = NA_ROW_BLOCK * GRID_W
    nb = n_rows // NA_ROW_BLOCK
    w = BRANCH_W
    return pl.pallas_call(
        functools.partial(_na_body, n_rows=n_rows),
        grid=(nb,),
        in_specs=[_slab(tq, w, off["a_q"]), _slab(tq, w, off["a_gate"]),
                  _resident((s, w), lambda i, _c=off["a_k"] // w: (0, _c)),
                  _resident((s, w), lambda i, _c=off["a_v"] // w: (0, _c)),
                  _resident((cn, w), lambda i, _c=a_k_ctx_off // w: (0, _c)),
                  _resident((cn, w), lambda i, _c=a_v_ctx_off // w: (0, _c)),
                  pl.BlockSpec((1,) + bias.shape[1:],
                               lambda i: (jnp.where(i == 0, 0, jnp.where(i == nb - 1, 2, 1)), 0, 0))],
        out_specs=pl.BlockSpec((tq, w), lambda i: (i, 0)),
        out_shape=jax.ShapeDtypeStruct((s, w), BF16),
        compiler_params=_params(dimension_semantics=("arbitrary",)),
        name="na_attn",
    )(p, p, p, p, pc, pc, bias)


def _halo_specs(t, tm, width, off):
    assert off % width == 0 and tm % HALO == 0
    c = off // width
    per, last = tm // HALO, t // HALO - 1
    return [pl.BlockSpec((tm, width), lambda i: (i, c)),
            pl.BlockSpec((HALO, width), lambda i: (jnp.maximum(i * per - 1, 0), c)),
            pl.BlockSpec((HALO, width), lambda i: (jnp.minimum((i + 1) * per, last), c))]


def _fill_padded(pad_ref, cur, prev, nxt, tm):
    i, n = pl.program_id(0), pl.num_programs(0)
    pad_ref[0:HALO] = jnp.where(i > 0, prev, jnp.zeros_like(prev))
    pad_ref[HALO:HALO + tm] = cur
    pad_ref[HALO + tm:] = jnp.where(i < n - 1, nxt, jnp.zeros_like(nxt))


def _pool_body(u_ref, up_ref, un_ref, gate_ref, w_ref, sc_ref, o_ref, pad_ref, *, tm, seq):
    _fill_padded(pad_ref, u_ref[...].astype(F32), up_ref[...].astype(F32), un_ref[...].astype(F32), tm)
    t = pl.program_id(0) * tm + lax.broadcasted_iota(jnp.int32, (tm, POOL_GROUP), 0)
    for gi, ksz in enumerate(POOL_SIZES):
        cols = slice(POOL_GROUP * gi, POOL_GROUP * (gi + 1))
        back = ksz // 2
        tot = pad_ref[HALO - back:HALO - back + tm, cols]
        for d in range(1 - back, ksz - back):
            tot = tot + pad_ref[HALO + d:HALO + d + tm, cols]
        lo = jnp.maximum(t - back, 0)
        hi = jnp.minimum(t + (ksz - 1 - back), seq - 1)
        mean = tot / (hi - lo + 1).astype(F32)
        dlt = (mean - pad_ref[HALO:HALO + tm, cols]).astype(BF16)
        y = jnp.dot(dlt, w_ref[gi], preferred_element_type=F32) * sc_ref[:, cols]
        o_ref[:, cols] = (y * _silu(gate_ref[:, cols].astype(F32))).astype(BF16)


def _pool(p, off, w_pool, pool_scale):
    t = p.shape[0]
    tm = _tile(t, 512, HALO)
    w = BRANCH_W
    return pl.pallas_call(
        functools.partial(_pool_body, tm=tm, seq=t),
        grid=(t // tm,),
        in_specs=_halo_specs(t, tm, w, off["b_in"]) + [
            _slab(tm, w, off["b_gate"]),
            pl.BlockSpec(w_pool.shape, lambda i: (0, 0, 0)),
            pl.BlockSpec((1, w), lambda i: (0, 0))],
        out_specs=pl.BlockSpec((tm, w), lambda i: (i, 0)),
        out_shape=jax.ShapeDtypeStruct((t, w), BF16),
        scratch_shapes=[pltpu.VMEM((tm + 2 * HALO, w), F32)],
        compiler_params=_params(dimension_semantics=("arbitrary",)),
        name="pool",
    )(p, p, p, p, w_pool, pool_scale)


def _glu(x):
    x = x.astype(F32)
    return x[:, :BRANCH_W] * jax.nn.sigmoid(x[:, BRANCH_W:])


def _conv_body(x_ref, xp_ref, xn_ref, gate_ref, cw_ref, cb_ref, lg_ref, lb_ref, pw_ref, o_ref, pad_ref, *, tm):
    _fill_padded(pad_ref, _glu(x_ref[...]), _glu(xp_ref[...]), _glu(xn_ref[...]), tm)
    reach = CONV_WIDTH // 2
    y = jnp.zeros((tm, BRANCH_W), F32) + cb_ref[...]
    for j in range(CONV_WIDTH):
        s = HALO - reach + j
        y = y + pad_ref[s:s + tm, :] * cw_ref[j:j + 1, :]
    mu = jnp.mean(y, axis=-1, keepdims=True)
    yc = y - mu
    var = jnp.mean(yc * yc, axis=-1, keepdims=True)
    z = _silu(yc * lax.rsqrt(var + EPS) * lg_ref[...] + lb_ref[...]).astype(BF16)
    out = jnp.dot(z, pw_ref[...], preferred_element_type=F32)
    o_ref[...] = (out * _silu(gate_ref[...].astype(F32))).astype(BF16)


def _conv(p, off, conv_w, conv_b, ln_g, ln_b, w_pw):
    t = p.shape[0]
    tm = _tile(t, 512, HALO)
    w = BRANCH_W
    vec = pl.BlockSpec((1, w), lambda i: (0, 0))
    return pl.pallas_call(
        functools.partial(_conv_body, tm=tm),
        grid=(t // tm,),
        in_specs=_halo_specs(t, tm, 2 * w, off["d_glu"]) + [
            _slab(tm, w, off["d_gate"]),
            pl.BlockSpec(conv_w.shape, lambda i: (0, 0)), vec, vec, vec,
            pl.BlockSpec((w, w), lambda i: (0, 0))],
        out_specs=pl.BlockSpec((tm, w), lambda i: (i, 0)),
        out_shape=jax.ShapeDtypeStruct((t, w), BF16),
        scratch_shapes=[pltpu.VMEM((tm + 2 * HALO, w), F32)],
        compiler_params=_params(dimension_semantics=("arbitrary",)),
        name="conv",
    )(p, p, p, p, conv_w, conv_b, ln_g, ln_b, w_pw)


def _merge_body(m0, m1, m2, m3, o0, o1, o2, o3, wb_ref, y_ref):
    y = None
    for bi, (m_ref, o_ref) in enumerate(((m0, o0), (m1, o1), (m2, o2), (m3, o3))):
        proj = jnp.dot(o_ref[...], wb_ref[bi], preferred_element_type=F32)
        term = jax.nn.sigmoid(m_ref[...].astype(F32)) * proj
        y = term if y is None else y + term
    y_ref[...] = y.astype(BF16)


def _merge(p, merge_off, outs, w_branch):
    t = p.shape[0]
    d = w_branch.shape[-1]
    tm = _tile(t, 256, 16)
    return pl.pallas_call(
        _merge_body,
        grid=(t // tm,),
        in_specs=[_slab(tm, d, merge_off + bi * d) for bi in range(N_BRANCH)]
        + [pl.BlockSpec((tm, BRANCH_W), lambda i: (i, 0))] * N_BRANCH
        + [_resident(w_branch.shape, lambda i: (0, 0, 0))],
        out_specs=pl.BlockSpec((tm, d), lambda i: (i, 0)),
        out_shape=jax.ShapeDtypeStruct((t, d), BF16),
        compiler_params=_params(dimension_semantics=("arbitrary",)),
        name="merge",
    )(p, p, p, p, *outs, w_branch)


def _out_body(y_ref, w_ref, x_ref, g_ref, gate_ref, o_ref):
    y = jnp.dot(y_ref[...], w_ref[...], preferred_element_type=F32)
    yn = y * lax.rsqrt(jnp.mean(y * y, axis=-1, keepdims=True) + EPS) * g_ref[...]
    o_ref[...] = x_ref[...] + gate_ref[...] * yn


def _out_proj(y, w_out, x, g_post, gate):
    t, d = x.shape
    tm = _tile(t, 256, 16)
    vec = pl.BlockSpec((1, d), lambda i: (0, 0))
    row = pl.BlockSpec((tm, d), lambda i: (i, 0))
    return pl.pallas_call(
        _out_body,
        grid=(t // tm,),
        in_specs=[row, _resident((d, d), lambda i: (0, 0)), row, vec, vec],
        out_specs=row,
        out_shape=jax.ShapeDtypeStruct((t, d), F32),
        compiler_params=_params(dimension_semantics=("arbitrary",)),
        name="out_proj",
    )(y, w_out, x, g_post, gate)


def _permute_cast_w_in(w_in_l, d_model):
    width, _, src, _ = _layout(d_model)
    return jnp.concatenate([w_in_l[:, src[n]:src[n] + width[n]] for n, _ in PARTS], axis=1).astype(BF16)


def kernel(x, c, ctx, c_ctx, w_ada, b_ada, g_pre, g_post, w_in, na_rpb, pool_w, pool_scale,
           q_norm, k_norm, conv_w, conv_b, conv_ln_g, conv_ln_b, conv_pw, w_branch, w_out):
    batch, seq, d = x.shape
    assert batch == 1 and seq % GRID_W == 0
    cn = ctx.shape[1]
    depth = w_ada.shape[0]
    width, off, _, n_cols = _layout(d)
    kv_off = off["a_k"]
    kv_rel = {n: off[n] - kv_off for n in ("a_k", "a_v", "c_k", "c_v")}

    xs, cs = x[0], ctx[0]
    cvec = jnp.zeros((8, d), F32).at[0].set(c[0]).at[1].set(c_ctx)
    ada = _ada(cvec, w_ada, b_ada)

    ones_bd = jnp.asarray(np.kron(np.eye(N_HEADS), np.ones((HEAD_DIM, HEAD_DIM))), BF16)
    rope_tabs = _rope_tables(seq)
    row = lambda v: v.reshape(1, -1)
    qn_all = jnp.tile(q_norm, (1, N_HEADS))
    kn_all = jnp.tile(k_norm, (1, GQA_KV_HEADS))

    for l in range(depth):
        last = l == depth - 1
        w_l = _permute_cast_w_in(w_in[l], d)
        wb_l = w_branch[l].astype(BF16)
        wo_l = w_out[l].astype(BF16)
        pw_l = pool_w[l].astype(BF16)
        cpw_l = conv_pw[l].astype(BF16)
        mod = lambda r: (ada[l, r:r + 1, :d], ada[l, r:r + 1, d:2 * d], ada[l, r:r + 1, 2 * d:])
        (shift, scale, gate), (shift_c, scale_c, gate_c) = mod(0), mod(1)
        qn, kn = row(qn_all[l]), row(kn_all[l])

        hc = _modulate(cs, row(g_pre[l]), shift_c, scale_c)
        if last:
            pc = _matmul(hc, w_l[:, kv_off:])
            coff = kv_rel
        else:
            pc = _matmul(hc, w_l)
            coff = off
        qc_ctx, kq_ctx = _qk_prep(pc, None if last else coff["c_q"], coff["c_k"], qn, kn, ones_bd, None)

        h = _modulate(xs, row(g_pre[l]), shift, scale)
        p = _matmul(h, w_l)
        qc, kc = _qk_prep(p, off["c_q"], off["c_k"], qn, kn, ones_bd, rope_tabs)
        k_all = jnp.concatenate([kc, kq_ctx], axis=0)
        v_all = jnp.concatenate([p[:, off["c_v"]:off["c_v"] + KV_W],
                                 pc[:, coff["c_v"]:coff["c_v"] + KV_W]], axis=0)
        o_c = _dense_attn(qc, 0, p, off["c_gate"], [(k_all, 0, v_all, 0)], True, 1.0)
        o_a = _na_attn(p, off, pc, coff["a_k"], coff["a_v"], _na_bias_tables(na_rpb[l], seq // GRID_W))
        o_b = _pool(p, off, pw_l, row(pool_scale[l]))
        o_d = _conv(p, off, conv_w[l], row(conv_b[l]), row(conv_ln_g[l]), row(conv_ln_b[l]), cpw_l)
        y = _merge(p, off["merge"], (o_a, o_b, o_c, o_d), wb_l)
        xs_next = _out_proj(y, wo_l, xs, row(g_post[l]), gate)

        if not last:
            o_a_c = _dense_attn(pc, off["a_q"], pc, off["a_gate"],
                                [(pc, off["a_k"], pc, off["a_v"])], False, SCORE_SCALE_LOG2)
            o_c_c = _dense_attn(qc_ctx, 0, pc, off["c_gate"], [(kq_ctx, 0, pc, off["c_v"])], True, 1.0)
            o_b_c = _pool(pc, off, pw_l, row(pool_scale[l]))
            o_d_c = _conv(pc, off, conv_w[l], row(conv_b[l]), row(conv_ln_g[l]), row(conv_ln_b[l]), cpw_l)
            y_c = _merge(pc, off["merge"], (o_a_c, o_b_c, o_c_c, o_d_c), wb_l)
            cs = _out_proj(y_c, wo_l, cs, row(g_post[l]), gate_c)
        xs = xs_next
    return xs[None]
```

```python
import functools

import numpy as np
import jax
import jax.numpy as jnp
from jax import lax
from jax.experimental import pallas as pl
from jax.experimental.pallas import tpu as pltpu

F32 = jnp.float32
BF16 = jnp.bfloat16

GRID_W = 64
HEAD_DIM = 64
BRANCH_W = 512
N_BRANCH = 4
N_HEADS = BRANCH_W // HEAD_DIM
GQA_KV_HEADS = 2
KV_W = GQA_KV_HEADS * HEAD_DIM
NA_WIN_ROWS = 8
NA_WIN_COLS = 16
POOL_SIZES = (2, 4, 8, 16)
POOL_GROUP = BRANCH_W // len(POOL_SIZES)
ROPE_THETA = 10000.0
CONV_WIDTH = 31
EPS = 1e-6

LANES = 128
SUBLANES = 8
HALO = 16
NA_ROW_BLOCK = 4
NA_BAND_ROWS = 12
NEG_BIG = -1e30
LOG2_E = 1.4426950408889634
SCORE_SCALE_LOG2 = HEAD_DIM ** -0.5 * LOG2_E
V7X_VMEM_LIMIT = 52 * 1024 * 1024

PARTS = (("merge", None), ("d_glu", 2 * BRANCH_W), ("a_q", BRANCH_W), ("c_q", BRANCH_W),
         ("a_gate", BRANCH_W), ("b_gate", BRANCH_W), ("c_gate", BRANCH_W), ("d_gate", BRANCH_W),
         ("b_in", BRANCH_W), ("a_k", BRANCH_W), ("a_v", BRANCH_W), ("c_k", KV_W), ("c_v", KV_W))
SRC_PARTS = ("a_k", "a_v", "c_k", "c_v", "a_q", "c_q", "a_gate", "b_in", "b_gate", "c_gate",
             "d_glu", "d_gate", "merge")


def _layout(d_model):
    width = {n: (N_BRANCH * d_model if w is None else w) for n, w in PARTS}
    off, o = {}, 0
    for n, _ in PARTS:
        off[n] = o
        o += width[n]
    src, o = {}, 0
    for n in SRC_PARTS:
        src[n] = o
        o += width[n]
    return width, off, src, o


def _params(**kw):
    return pltpu.CompilerParams(vmem_limit_bytes=V7X_VMEM_LIMIT, **kw)


def _tile(n, pref, mult):
    if n <= pref:
        return n
    t = (pref // mult) * mult
    while t >= mult:
        if n % t == 0:
            return t
        t -= mult
    raise ValueError(f"no tile for {n}")


def _resident(block_shape, index_map):
    return pl.BlockSpec(block_shape, index_map, pipeline_mode=pl.Buffered(1))


def _slab(tm, width, off):
    assert off % width == 0
    return pl.BlockSpec((tm, width), lambda i, _c=off // width: (i, _c))


def _silu(x):
    return x * jax.nn.sigmoid(x)


def _ada_body(cs_ref, w_ref, b_ref, o_ref):
    s = _silu(cs_ref[...]).astype(BF16)
    o_ref[0] = jnp.dot(s, w_ref[0].astype(BF16), preferred_element_type=F32) + b_ref[0]


def _ada(cs, w_ada, b_ada):
    depth, d, n = w_ada.shape
    tn = _tile(n, 1024, LANES)
    return pl.pallas_call(
        _ada_body,
        grid=(depth, n // tn),
        in_specs=[pl.BlockSpec((8, d), lambda l, j: (0, 0)),
                  pl.BlockSpec((1, d, tn), lambda l, j: (l, 0, j)),
                  pl.BlockSpec((1, 1, tn), lambda l, j: (l, 0, j))],
        out_specs=pl.BlockSpec((1, 8, tn), lambda l, j: (l, 0, j)),
        out_shape=jax.ShapeDtypeStruct((depth, 8, n), F32),
        compiler_params=_params(dimension_semantics=("arbitrary", "arbitrary")),
        name="ada",
    )(cs, w_ada, b_ada.reshape(depth, 1, n))


def _modulate_body(x_ref, g_ref, sh_ref, sc_ref, o_ref):
    x = x_ref[...]
    y = x * lax.rsqrt(jnp.mean(x * x, axis=-1, keepdims=True) + EPS) * g_ref[...]
    o_ref[...] = (y * (1.0 + sc_ref[...]) + sh_ref[...]).astype(BF16)


def _modulate(x, g, shift, scale):
    t, d = x.shape
    tm = _tile(t, 512, 16)
    vec = pl.BlockSpec((1, d), lambda i: (0, 0))
    return pl.pallas_call(
        _modulate_body,
        grid=(t // tm,),
        in_specs=[pl.BlockSpec((tm, d), lambda i: (i, 0)), vec, vec, vec],
        out_specs=pl.BlockSpec((tm, d), lambda i: (i, 0)),
        out_shape=jax.ShapeDtypeStruct((t, d), BF16),
        compiler_params=_params(dimension_semantics=("arbitrary",)),
        name="modulate",
    )(x, g, shift, scale)


def _matmul_body(h_ref, w_ref, o_ref):
    o_ref[...] = jnp.dot(h_ref[...], w_ref[...], preferred_element_type=F32).astype(o_ref.dtype)


def _col_tile(n):
    for mult in (2 * LANES, LANES):
        try:
            return _tile(n, 1536, mult)
        except ValueError:
            pass
    raise ValueError(n)


def _matmul(h, w):
    t, d = h.shape
    n = w.shape[1]
    tm = _tile(t, 1024, 16)
    tn = _col_tile(n)
    return pl.pallas_call(
        _matmul_body,
        grid=(t // tm, n // tn),
        in_specs=[pl.BlockSpec((tm, d), lambda i, j: (i, 0)),
                  pl.BlockSpec((d, tn), lambda i, j: (0, j))],
        out_specs=pl.BlockSpec((tm, tn), lambda i, j: (i, j)),
        out_shape=jax.ShapeDtypeStruct((t, n), BF16),
        compiler_params=_params(dimension_semantics=("arbitrary", "arbitrary")),
        name="in_proj",
    )(h, w)


def _head_meansq(x, ones_bd):
    ss = x * x
    hi = ss.astype(BF16)
    lo = (ss - hi.astype(F32)).astype(BF16)
    tot = (jnp.dot(hi, ones_bd, preferred_element_type=F32)
           + jnp.dot(lo, ones_bd, preferred_element_type=F32))
    return tot * (1.0 / HEAD_DIM)


def _rope(y, cos, sin_signed):
    w = y.shape[-1]
    lane = lax.broadcasted_iota(jnp.int32, y.shape, 1)
    nxt = pltpu.roll(y, w - 16, 1)
    prv = pltpu.roll(y, 16, 1)
    return y * cos + jnp.where((lane % 32) < 16, nxt, prv) * sin_signed


def _norm_rope(x, w, bd, rope):
    x = x * lax.rsqrt(_head_meansq(x, bd) + EPS) * w
    if rope is not None:
        reps = x.shape[-1] // LANES
        x = _rope(x, jnp.concatenate([rope[0]] * reps, axis=1), jnp.concatenate([rope[1]] * reps, axis=1))
    return x


VT_ROWS = HEAD_DIM + 16


def _qk_prep_body(*refs, use_rope, with_q):
    refs = list(refs)
    q_ref = refs.pop(0) if with_q else None
    k_ref, v_ref = refs.pop(0), refs.pop(0)
    qn_ref = refs.pop(0) if with_q else None
    kn_ref, bd_ref = refs.pop(0), refs.pop(0)
    rope = (refs.pop(0)[...], refs.pop(0)[...]) if use_rope else None
    bd = bd_ref[...]
    if with_q:
        q = _norm_rope(q_ref[...].astype(F32), qn_ref[...], bd, rope)
        refs.pop(0)[...] = (q * SCORE_SCALE_LOG2).astype(BF16)
    k = _norm_rope(k_ref[...].astype(F32), kn_ref[...], bd[:KV_W, :KV_W], rope)
    refs.pop(0)[...] = k.astype(BF16)
    vt_ref = refs.pop(0)
    vt = v_ref[...].astype(F32).T
    row = lax.broadcasted_iota(jnp.int32, (VT_ROWS - HEAD_DIM, vt.shape[1]), 0)
    tail = jnp.where(row == 0, 1.0, 0.0)
    for g in range(GQA_KV_HEADS):
        vt_ref[0, g] = jnp.concatenate([vt[HEAD_DIM * g:HEAD_DIM * (g + 1)], tail], axis=0).astype(BF16)


def _qk_prep(p, q_off, k_off, v_off, qn, kn, ones_bd, rope_tabs):
    t = p.shape[0]
    tm = _tile(t, 1024, LANES)
    use_rope, with_q = rope_tabs is not None, q_off is not None
    const = lambda shape: pl.BlockSpec(shape, lambda i: (0, 0))
    in_specs, args, out_specs, out_shape = [], [], [], []
    if with_q:
        in_specs.append(_slab(tm, BRANCH_W, q_off))
        args.append(p)
    in_specs += [_slab(tm, KV_W, k_off), _slab(tm, KV_W, v_off)]
    args += [p, p]
    if with_q:
        in_specs.append(const((1, BRANCH_W)))
        args.append(qn)
        out_specs.append(pl.BlockSpec((tm, BRANCH_W), lambda i: (i, 0)))
        out_shape.append(jax.ShapeDtypeStruct((t, BRANCH_W), BF16))
    in_specs += [const((1, KV_W)), const((BRANCH_W, BRANCH_W))]
    args += [kn, ones_bd]
    if use_rope:
        in_specs += [pl.BlockSpec((tm, LANES), lambda i: (i, 0))] * 2
        args += list(rope_tabs)
    out_specs += [pl.BlockSpec((tm, KV_W), lambda i: (i, 0)),
                  pl.BlockSpec((1, GQA_KV_HEADS, VT_ROWS, tm), lambda i: (i, 0, 0, 0))]
    out_shape += [jax.ShapeDtypeStruct((t, KV_W), BF16),
                  jax.ShapeDtypeStruct((t // tm, GQA_KV_HEADS, VT_ROWS, tm), BF16)]
    res = pl.pallas_call(
        functools.partial(_qk_prep_body, use_rope=use_rope, with_q=with_q),
        grid=(t // tm,),
        in_specs=in_specs,
        out_specs=out_specs,
        out_shape=out_shape,
        compiler_params=_params(dimension_semantics=("arbitrary",)),
        name="qk_prep",
    )(*args)
    return res if with_q else (None, res[0], res[1])


def _rope_tables(seq):
    n_rows = seq // GRID_W
    half = HEAD_DIM // 4
    freqs = ROPE_THETA ** (-jnp.arange(half, dtype=F32) / half)
    ang_r = jnp.arange(n_rows).astype(F32)[:, None] * freqs[None, :]
    ang_c = jnp.arange(GRID_W).astype(F32)[:, None] * freqs[None, :]
    per_row = lambda a: jnp.broadcast_to(a[:, None, :], (n_rows, GRID_W, half)).reshape(seq, half)
    per_col = lambda a: jnp.broadcast_to(a[None, :, :], (n_rows, GRID_W, half)).reshape(seq, half)
    cos_r, sin_r, cos_c, sin_c = (per_row(jnp.cos(ang_r)), per_row(jnp.sin(ang_r)),
                                  per_col(jnp.cos(ang_c)), per_col(jnp.sin(ang_c)))
    cos = jnp.concatenate([cos_r, cos_r, cos_c, cos_c], axis=1)
    sin = jnp.concatenate([-sin_r, sin_r, -sin_c, sin_c], axis=1)
    reps = LANES // HEAD_DIM
    return jnp.concatenate([cos] * reps, axis=1), jnp.concatenate([sin] * reps, axis=1)


def _nt_dot(a, b):
    return lax.dot_general(a, b, (((1,), (1,)), ((), ())), preferred_element_type=F32)


def _pair_queries(qp, low, shared_kv_lanes):
    zero = jnp.zeros_like(qp)
    if shared_kv_lanes is None:
        return jnp.where(low, qp, zero), jnp.where(low, zero, qp)
    qr = pltpu.roll(qp, HEAD_DIM, 1)
    if shared_kv_lanes == 0:
        return jnp.where(low, qp, zero), jnp.where(low, qr, zero)
    return jnp.where(low, zero, qr), jnp.where(low, zero, qp)


def _values_with_ones(vc, half):
    lane = lax.broadcasted_iota(jnp.int32, vc.shape, 1)
    keep = (lane < HEAD_DIM) if half == 0 else (lane >= HEAD_DIM)
    ones_lane = HEAD_DIM * (1 - half)
    fill = jnp.where(lane == ones_lane, 1.0, 0.0)
    return jnp.where(keep, vc.astype(F32), fill).astype(vc.dtype)


def _dense_attn_body(*refs, n_src, chunks, kv_grouped, q_scale, tq):
    n_pair = N_HEADS // 2
    q_ref, gate_ref = refs[0], refs[1]
    srcs = [(refs[2 + 2 * i], refs[3 + 2 * i]) for i in range(n_src)]
    o_ref = refs[2 + 2 * n_src]
    scratch = refs[3 + 2 * n_src:]
    m_s, acc_s = scratch[:n_pair], scratch[n_pair:]
    low_q = lax.broadcasted_iota(jnp.int32, (tq, LANES), 1) < HEAD_DIM

    halves = [((2 * p) // (N_HEADS // GQA_KV_HEADS),) * 2 if kv_grouped else (0, 1) for p in range(n_pair)]
    lhs = []
    for p in range(n_pair):
        qp = q_ref[:, LANES * p:LANES * (p + 1)].astype(F32) * q_scale
        qa, qb = _pair_queries(qp, low_q, halves[p][0] if kv_grouped else None)
        lhs.append(jnp.concatenate([qa, qb], axis=0).astype(BF16))
        m_s[p][...] = jnp.full(m_s[p].shape, NEG_BIG, F32)
        acc_s[p][...] = jnp.zeros(acc_s[p].shape, F32)

    def step(p, kc, va, vb):
        s = _nt_dot(lhs[p], kc)
        m_old = m_s[p][...]
        m_new = jnp.maximum(m_old, jnp.max(s, axis=-1, keepdims=True))
        alpha = jnp.exp2(m_old - m_new)
        pr = jnp.exp2(s - m_new).astype(BF16)
        acc = acc_s[p]
        acc[:tq] = alpha[:tq] * acc[:tq] + jnp.dot(pr[:tq], va, preferred_element_type=F32)
        acc[tq:] = alpha[tq:] * acc[tq:] + jnp.dot(pr[tq:], vb, preferred_element_type=F32)
        m_s[p][...] = m_new

    def all_pairs(k_ref, v_ref, rows):
        if kv_grouped:
            kc, vc = k_ref[rows, :], v_ref[rows, :]
            vals = [_values_with_ones(vc, h) for h in range(GQA_KV_HEADS)]
        for p in range(n_pair):
            if kv_grouped:
                step(p, kc, vals[halves[p][0]], vals[halves[p][1]])
            else:
                cols = slice(LANES * p, LANES * (p + 1))
                vc = v_ref[rows, cols]
                step(p, k_ref[rows, cols], _values_with_ones(vc, 0), _values_with_ones(vc, 1))

    for (k_ref, v_ref), ck in zip(srcs, chunks):
        n_chunk = k_ref.shape[0] // ck
        if n_chunk == 1:
            all_pairs(k_ref, v_ref, slice(None))
        else:
            def loop(i, carry, k_ref=k_ref, v_ref=v_ref, ck=ck):
                all_pairs(k_ref, v_ref, pl.ds(pl.multiple_of(i * ck, ck), ck))
                return carry
            lax.fori_loop(0, n_chunk, loop, 0)

    for p in range(n_pair):
        cols = slice(LANES * p, LANES * (p + 1))
        heads = []
        for hd, half in enumerate(halves[p]):
            acc = acc_s[p][hd * tq:(hd + 1) * tq]
            ones_lane = HEAD_DIM * (1 - half)
            o = acc / acc[:, ones_lane:ones_lane + 1]
            heads.append(o if half == hd else pltpu.roll(o, HEAD_DIM, 1))
        o = jnp.where(low_q, heads[0], heads[1])
        o_ref[:, cols] = (o * _silu(gate_ref[:, cols].astype(F32))).astype(BF16)


def _key_chunk(tk):
    for mult in (2 * LANES, LANES, 16):
        try:
            return _tile(tk, 1024, mult)
        except ValueError:
            pass
    raise ValueError(tk)


def _dense_attn(q, q_off, gate, gate_off, kv_srcs, kv_grouped, q_scale):
    t = q.shape[0]
    tq = _tile(t, 256, 16)
    kw = KV_W if kv_grouped else BRANCH_W
    in_specs = [_slab(tq, BRANCH_W, q_off), _slab(tq, BRANCH_W, gate_off)]
    args = [q, gate]
    chunks = []
    for k_arr, k_off, v_arr, v_off in kv_srcs:
        tk = k_arr.shape[0]
        assert k_off % kw == 0 and v_off % kw == 0
        in_specs += [_resident((tk, kw), lambda i, _c=k_off // kw: (0, _c)),
                     _resident((tk, kw), lambda i, _c=v_off // kw: (0, _c))]
        args += [k_arr, v_arr]
        chunks.append(_key_chunk(tk))
    n_pair = N_HEADS // 2
    return pl.pallas_call(
        functools.partial(_dense_attn_body, n_src=len(kv_srcs), chunks=tuple(chunks),
                          kv_grouped=kv_grouped, q_scale=q_scale, tq=tq),
        grid=(t // tq,),
        in_specs=in_specs,
        out_specs=pl.BlockSpec((tq, BRANCH_W), lambda i: (i, 0)),
        out_shape=jax.ShapeDtypeStruct((t, BRANCH_W), BF16),
        scratch_shapes=[pltpu.VMEM((2 * tq, 1), F32)] * n_pair + [pltpu.VMEM((2 * tq, LANES), F32)] * n_pair,
        compiler_params=_params(dimension_semantics=("arbitrary",)),
        name="dense_attn",
    )(*args)


def _gqa_body(q_ref, gate_ref, k_ref, vt_ref, kx_ref, vtx_ref, o_ref, *scratch, tq, n_chunk):
    n_pair = N_HEADS // 2
    qt_s, m_s, acc_s, sx_s = (scratch[i * n_pair:(i + 1) * n_pair] for i in range(4))
    s_s = [scratch[4 * n_pair + 2 * p:4 * n_pair + 2 * p + 2] for p in range(n_pair)]
    low_q = lax.broadcasted_iota(jnp.int32, (tq, LANES), 1) < HEAD_DIM
    kv_of = [(2 * p) // (N_HEADS // GQA_KV_HEADS) for p in range(n_pair)]

    def scores(p, keys, s_ref):
        s_ref[...] = jnp.dot(keys, qt_s[p][...], preferred_element_type=F32)

    def consume(p, s_ref, vt):
        s = s_ref[...]
        m_old = m_s[p][...]
        m_new = jnp.maximum(m_old, jnp.max(s, axis=0, keepdims=True))
        alpha = jnp.exp2(m_old - m_new)
        pt = jnp.exp2(s - m_new).astype(BF16)
        for hd in range(2):
            cols = slice(hd * tq, (hd + 1) * tq)
            acc_s[p][hd] = alpha[:, cols] * acc_s[p][hd] + jnp.dot(vt, pt[:, cols], preferred_element_type=F32)
        m_s[p][...] = m_new

    def step(i, slot):
        for p in range(n_pair):
            scores(p, k_ref[i + 1], s_s[p][1 - slot])
            consume(p, s_s[p][slot], vt_ref[i, kv_of[p]])

    for p in range(n_pair):
        qp = q_ref[:, LANES * p:LANES * (p + 1)].astype(F32)
        qa, qb = _pair_queries(qp, low_q, kv_of[p])
        qt_s[p][...] = jnp.concatenate([qa, qb], axis=0).T.astype(BF16)
        m_s[p][...] = jnp.full(m_s[p].shape, NEG_BIG, F32)
        acc_s[p][...] = jnp.zeros(acc_s[p].shape, F32)
        scores(p, k_ref[0], s_s[p][0])

    def two_steps(j, carry):
        step(2 * j, 0)
        step(2 * j + 1, 1)
        return carry

    n_step = n_chunk - 1
    lax.fori_loop(0, n_step // 2, two_steps, 0)
    last = n_chunk - 1
    if n_step % 2:
        step(last - 1, 0)
    for p in range(n_pair):
        scores(p, kx_ref[...], sx_s[p])
        consume(p, s_s[p][last % 2], vt_ref[last, kv_of[p]])
    for p in range(n_pair):
        consume(p, sx_s[p], vtx_ref[0, kv_of[p]])
        acc = acc_s[p]
        ot = jnp.concatenate([acc[hd, :HEAD_DIM] / acc[hd, HEAD_DIM:HEAD_DIM + 1] for hd in range(2)], axis=0)
        cols = slice(LANES * p, LANES * (p + 1))
        o_ref[:, cols] = (ot.T * _silu(gate_ref[:, cols].astype(F32))).astype(BF16)


def _gqa_attn(q, gate, gate_off, k, vt, k_extra, vt_extra):
    t = q.shape[0]
    n_chunk, _, _, ck = vt.shape
    tx = k_extra.shape[0]
    assert vt_extra.shape[0] == 1 and k.shape[0] == n_chunk * ck
    tq = _tile(t, 256, LANES)
    n_pair = N_HEADS // 2
    scratch = ([pltpu.VMEM((LANES, 2 * tq), BF16)] * n_pair + [pltpu.VMEM((1, 2 * tq), F32)] * n_pair
               + [pltpu.VMEM((2, VT_ROWS, tq), F32)] * n_pair + [pltpu.VMEM((tx, 2 * tq), F32)] * n_pair
               + [pltpu.VMEM((ck, 2 * tq), F32)] * (2 * n_pair))
    return pl.pallas_call(
        functools.partial(_gqa_body, tq=tq, n_chunk=n_chunk),
        grid=(t // tq,),
        in_specs=[pl.BlockSpec((tq, BRANCH_W), lambda i: (i, 0)), _slab(tq, BRANCH_W, gate_off),
                  _resident((n_chunk, ck, KV_W), lambda i: (0, 0, 0)),
                  _resident(vt.shape, lambda i: (0, 0, 0, 0)),
                  _resident((tx, KV_W), lambda i: (0, 0)),
                  _resident(vt_extra.shape, lambda i: (0, 0, 0, 0))],
        out_specs=pl.BlockSpec((tq, BRANCH_W), lambda i: (i, 0)),
        out_shape=jax.ShapeDtypeStruct((t, BRANCH_W), BF16),
        scratch_shapes=scratch,
        compiler_params=_params(dimension_semantics=("arbitrary",)),
        name="gqa_attn",
    )(q, gate, k.reshape(n_chunk, ck, KV_W), vt, k_extra, vt_extra)


def _na_block_geometry(n_rows):
    rb, band = NA_ROW_BLOCK, NA_BAND_ROWS
    assert n_rows % rb == 0 and n_rows >= band and band >= rb + NA_WIN_ROWS - 1
    nb = n_rows // rb
    r0 = np.arange(nb) * rb
    u0 = np.clip(r0 - NA_WIN_ROWS // 2, 0, n_rows - band)
    r = r0[:, None] + np.arange(rb)[None, :]
    row_start = np.clip(r - NA_WIN_ROWS // 2, 0, n_rows - NA_WIN_ROWS)
    key_row = u0[:, None] + np.arange(band)[None, :]
    rel = key_row[:, None, :] - row_start[:, :, None]
    valid = (rel >= 0) & (rel < NA_WIN_ROWS)
    row_off = np.where(valid, key_row[:, None, :] - r[:, :, None] + NA_WIN_ROWS - 1, 0)
    return u0, row_off, valid


def _na_bias_tables(rpb, n_rows):
    u0, row_off, valid = _na_block_geometry(n_rows)
    nb = row_off.shape[0]
    keys = [row_off[b].tobytes() + valid[b].tobytes() for b in range(nb)]
    cases, case_of = [], []
    for b in range(nb):
        if keys[b] not in [keys[c] for c in cases]:
            cases.append(b)
        case_of.append([keys[c] for c in cases].index(keys[b]))
    assert case_of == [0] + [1] * (nb - 2) + [2], case_of
    cq = np.arange(GRID_W)
    col_start = np.clip(cq - NA_WIN_COLS // 2, 0, GRID_W - NA_WIN_COLS)
    col_ok = (cq[None, :] >= col_start[:, None]) & (cq[None, :] < col_start[:, None] + NA_WIN_COLS)
    col_off = np.clip(cq[None, :] - cq[:, None], -(NA_WIN_COLS - 1), NA_WIN_COLS - 1) + NA_WIN_COLS - 1
    onehot = jnp.asarray(col_off[:, :, None] == np.arange(2 * NA_WIN_COLS - 1), F32)
    tiles = jnp.einsum("lhrm,qkm->lhrqk", rpb.astype(F32), onehot, precision=lax.Precision.HIGHEST)
    tiles = jnp.where(col_ok, tiles * LOG2_E, NEG_BIG)
    masked = jnp.full(tiles.shape[:2] + (GRID_W, GRID_W), NEG_BIG, F32)
    blocks = []
    for c in cases:
        rows = [jnp.concatenate([tiles[:, :, row_off[c, i, j]] if valid[c, i, j] else masked
                                 for j in range(NA_BAND_ROWS)], axis=-1) for i in range(NA_ROW_BLOCK)]
        blocks.append(jnp.stack(rows, axis=2))
    bias = jnp.stack(blocks, axis=1)
    return bias.reshape(rpb.shape[0], 3, N_HEADS * NA_ROW_BLOCK * GRID_W, NA_BAND_ROWS * GRID_W)


def _na_body(q_ref, gate_ref, k_ref, v_ref, kc_ref, vc_ref, bias_ref, o_ref, *, n_rows):
    rb, band = NA_ROW_BLOCK, NA_BAND_ROWS
    tq, nk = rb * GRID_W, band * GRID_W
    b = pl.program_id(0)
    u0 = jnp.clip(b * rb - NA_WIN_ROWS // 2, 0, n_rows - band)
    rows = pl.ds(pl.multiple_of(u0 * GRID_W, GRID_W), nk)
    low_q = lax.broadcasted_iota(jnp.int32, (tq, LANES), 1) < HEAD_DIM
    for p in range(N_HEADS // 2):
        cols = slice(LANES * p, LANES * (p + 1))
        qp = q_ref[:, cols].astype(F32) * SCORE_SCALE_LOG2
        qa, qb = _pair_queries(qp, low_q, None)
        lhs = jnp.concatenate([qa, qb], axis=0).astype(BF16)
        kb, vb = k_ref[rows, cols], v_ref[rows, cols]
        kc, vc = kc_ref[:, cols], vc_ref[:, cols]
        s_band = _nt_dot(lhs, kb) + bias_ref[0, 0, 2 * tq * p:2 * tq * (p + 1), :]
        s_ctx = _nt_dot(lhs, kc)
        m = jnp.maximum(jnp.max(s_band, axis=-1, keepdims=True), jnp.max(s_ctx, axis=-1, keepdims=True))
        pb = jnp.exp2(s_band - m).astype(BF16)
        pc = jnp.exp2(s_ctx - m).astype(BF16)
        heads = []
        for hd in range(2):
            r = slice(hd * tq, (hd + 1) * tq)
            acc = (jnp.dot(pb[r], _values_with_ones(vb, hd), preferred_element_type=F32)
                   + jnp.dot(pc[r], _values_with_ones(vc, hd), preferred_element_type=F32))
            ones_lane = HEAD_DIM * (1 - hd)
            heads.append(acc / acc[:, ones_lane:ones_lane + 1])
        o = jnp.where(low_q, heads[0], heads[1])
        o_ref[:, cols] = (o * _silu(gate_ref[:, cols].astype(F32))).astype(BF16)


def _na_attn(p, off, pc, a_k_ctx_off, a_v_ctx_off, bias, layer):
    s = p.shape[0]
    cn = pc.shape[0]
    n_rows = s // GRID_W
    tq = NA_ROW_BLOCK * GRID_W
    nb = n_rows // NA_ROW_BLOCK
    w = BRANCH_W
    return pl.pallas_call(
        functools.partial(_na_body, n_rows=n_rows),
        grid=(nb,),
        in_specs=[_slab(tq, w, off["a_q"]), _slab(tq, w, off["a_gate"]),
                  _resident((s, w), lambda i, _c=off["a_k"] // w: (0, _c)),
                  _resident((s, w), lambda i, _c=off["a_v"] // w: (0, _c)),
                  _resident((cn, w), lambda i, _c=a_k_ctx_off // w: (0, _c)),
                  _resident((cn, w), lambda i, _c=a_v_ctx_off // w: (0, _c)),
                  pl.BlockSpec((1, 1) + bias.shape[2:],
                               lambda i: (layer, jnp.where(i == 0, 0, jnp.where(i == nb - 1, 2, 1)), 0, 0))],
        out_specs=pl.BlockSpec((tq, w), lambda i: (i, 0)),
        out_shape=jax.ShapeDtypeStruct((s, w), BF16),
        compiler_params=_params(dimension_semantics=("arbitrary",)),
        name="na_attn",
    )(p, p, p, p, pc, pc, bias)


def _halo_specs(t, tm, width, off):
    assert off % width == 0 and tm % HALO == 0
    c = off // width
    per, last = tm // HALO, t // HALO - 1
    return [pl.BlockSpec((tm, width), lambda i: (i, c)),
            pl.BlockSpec((HALO, width), lambda i: (jnp.maximum(i * per - 1, 0), c)),
            pl.BlockSpec((HALO, width), lambda i: (jnp.minimum((i + 1) * per, last), c))]


def _fill_padded(pad_ref, cur, prev, nxt, tm):
    i, n = pl.program_id(0), pl.num_programs(0)
    pad_ref[0:HALO] = jnp.where(i > 0, prev, jnp.zeros_like(prev))
    pad_ref[HALO:HALO + tm] = cur
    pad_ref[HALO + tm:] = jnp.where(i < n - 1, nxt, jnp.zeros_like(nxt))


def _pool_body(u_ref, up_ref, un_ref, gate_ref, w_ref, sc_ref, o_ref, pad_ref, *, tm, seq):
    _fill_padded(pad_ref, u_ref[...].astype(F32), up_ref[...].astype(F32), un_ref[...].astype(F32), tm)
    t = pl.program_id(0) * tm + lax.broadcasted_iota(jnp.int32, (tm, POOL_GROUP), 0)
    for gi, ksz in enumerate(POOL_SIZES):
        cols = slice(POOL_GROUP * gi, POOL_GROUP * (gi + 1))
        back = ksz // 2
        tot = pad_ref[HALO - back:HALO - back + tm, cols]
        for d in range(1 - back, ksz - back):
            tot = tot + pad_ref[HALO + d:HALO + d + tm, cols]
        lo = jnp.maximum(t - back, 0)
        hi = jnp.minimum(t + (ksz - 1 - back), seq - 1)
        mean = tot / (hi - lo + 1).astype(F32)
        dlt = (mean - pad_ref[HALO:HALO + tm, cols]).astype(BF16)
        y = jnp.dot(dlt, w_ref[gi], preferred_element_type=F32) * sc_ref[:, cols]
        o_ref[:, cols] = (y * _silu(gate_ref[:, cols].astype(F32))).astype(BF16)


def _pool(p, off, w_pool, pool_scale):
    t = p.shape[0]
    tm = _tile(t, 512, HALO)
    w = BRANCH_W
    return pl.pallas_call(
        functools.partial(_pool_body, tm=tm, seq=t),
        grid=(t // tm,),
        in_specs=_halo_specs(t, tm, w, off["b_in"]) + [
            _slab(tm, w, off["b_gate"]),
            pl.BlockSpec(w_pool.shape, lambda i: (0, 0, 0)),
            pl.BlockSpec((1, w), lambda i: (0, 0))],
        out_specs=pl.BlockSpec((tm, w), lambda i: (i, 0)),
        out_shape=jax.ShapeDtypeStruct((t, w), BF16),
        scratch_shapes=[pltpu.VMEM((tm + 2 * HALO, w), F32)],
        compiler_params=_params(dimension_semantics=("arbitrary",)),
        name="pool",
    )(p, p, p, p, w_pool, pool_scale)


def _glu(x):
    x = x.astype(F32)
    return x[:, :BRANCH_W] * jax.nn.sigmoid(x[:, BRANCH_W:])


def _conv_body(x_ref, xp_ref, xn_ref, gate_ref, cw_ref, cb_ref, lg_ref, lb_ref, pw_ref, o_ref, pad_ref, sh_ref,
               *, tm):
    _fill_padded(pad_ref, _glu(x_ref[...]), _glu(xp_ref[...]), _glu(xn_ref[...]), tm)
    span = sh_ref.shape[1]
    for b in range(1, SUBLANES):
        sh_ref[b - 1] = pad_ref[b:b + span, :]
    reach = CONV_WIDTH // 2
    y = jnp.zeros((tm, BRANCH_W), F32) + cb_ref[...]
    for j in range(CONV_WIDTH):
        a, b = divmod(HALO - reach + j, SUBLANES)
        rows = slice(SUBLANES * a, SUBLANES * a + tm)
        y = y + (pad_ref[rows, :] if b == 0 else sh_ref[b - 1, rows, :]) * cw_ref[j:j + 1, :]
    mu = jnp.mean(y, axis=-1, keepdims=True)
    yc = y - mu
    var = jnp.mean(yc * yc, axis=-1, keepdims=True)
    z = _silu(yc * lax.rsqrt(var + EPS) * lg_ref[...] + lb_ref[...]).astype(BF16)
    out = jnp.dot(z, pw_ref[...], preferred_element_type=F32)
    o_ref[...] = (out * _silu(gate_ref[...].astype(F32))).astype(BF16)


def _conv(p, off, conv_w, conv_b, ln_g, ln_b, w_pw):
    t = p.shape[0]
    tm = _tile(t, 512, HALO)
    w = BRANCH_W
    vec = pl.BlockSpec((1, w), lambda i: (0, 0))
    return pl.pallas_call(
        functools.partial(_conv_body, tm=tm),
        grid=(t // tm,),
        in_specs=_halo_specs(t, tm, 2 * w, off["d_glu"]) + [
            _slab(tm, w, off["d_gate"]),
            pl.BlockSpec(conv_w.shape, lambda i: (0, 0)), vec, vec, vec,
            pl.BlockSpec((w, w), lambda i: (0, 0))],
        out_specs=pl.BlockSpec((tm, w), lambda i: (i, 0)),
        out_shape=jax.ShapeDtypeStruct((t, w), BF16),
        scratch_shapes=[pltpu.VMEM((tm + 2 * HALO, w), F32),
                        pltpu.VMEM((SUBLANES - 1, tm + 2 * HALO - SUBLANES, w), F32)],
        compiler_params=_params(dimension_semantics=("arbitrary",)),
        name="conv",
    )(p, p, p, p, conv_w, conv_b, ln_g, ln_b, w_pw)


def _merge_body(m0, m1, m2, m3, o0, o1, o2, o3, wb_ref, y_ref):
    y = None
    for bi, (m_ref, o_ref) in enumerate(((m0, o0), (m1, o1), (m2, o2), (m3, o3))):
        proj = jnp.dot(o_ref[...], wb_ref[bi], preferred_element_type=F32)
        term = jax.nn.sigmoid(m_ref[...].astype(F32)) * proj
        y = term if y is None else y + term
    y_ref[...] = y.astype(BF16)


def _merge(p, merge_off, outs, w_branch):
    t = p.shape[0]
    d = w_branch.shape[-1]
    tm = _tile(t, 256, 16)
    return pl.pallas_call(
        _merge_body,
        grid=(t // tm,),
        in_specs=[_slab(tm, d, merge_off + bi * d) for bi in range(N_BRANCH)]
        + [pl.BlockSpec((tm, BRANCH_W), lambda i: (i, 0))] * N_BRANCH
        + [_resident(w_branch.shape, lambda i: (0, 0, 0))],
        out_specs=pl.BlockSpec((tm, d), lambda i: (i, 0)),
        out_shape=jax.ShapeDtypeStruct((t, d), BF16),
        compiler_params=_params(dimension_semantics=("arbitrary",)),
        name="merge",
    )(p, p, p, p, *outs, w_branch)


def _out_body(y_ref, w_ref, x_ref, g_ref, gate_ref, o_ref):
    y = jnp.dot(y_ref[...], w_ref[...], preferred_element_type=F32)
    yn = y * lax.rsqrt(jnp.mean(y * y, axis=-1, keepdims=True) + EPS) * g_ref[...]
    o_ref[...] = x_ref[...] + gate_ref[...] * yn


def _out_proj(y, w_out, x, g_post, gate):
    t, d = x.shape
    tm = _tile(t, 256, 16)
    vec = pl.BlockSpec((1, d), lambda i: (0, 0))
    row = pl.BlockSpec((tm, d), lambda i: (i, 0))
    return pl.pallas_call(
        _out_body,
        grid=(t // tm,),
        in_specs=[row, _resident((d, d), lambda i: (0, 0)), row, vec, vec],
        out_specs=row,
        out_shape=jax.ShapeDtypeStruct((t, d), F32),
        compiler_params=_params(dimension_semantics=("arbitrary",)),
        name="out_proj",
    )(y, w_out, x, g_post, gate)


def _permute_cast_w_in(w_in_l, d_model):
    width, _, src, _ = _layout(d_model)
    return jnp.concatenate([w_in_l[:, src[n]:src[n] + width[n]] for n, _ in PARTS], axis=1).astype(BF16)


def kernel(x, c, ctx, c_ctx, w_ada, b_ada, g_pre, g_post, w_in, na_rpb, pool_w, pool_scale,
           q_norm, k_norm, conv_w, conv_b, conv_ln_g, conv_ln_b, conv_pw, w_branch, w_out):
    batch, seq, d = x.shape
    assert batch == 1 and seq % GRID_W == 0
    cn = ctx.shape[1]
    depth = w_ada.shape[0]
    width, off, _, n_cols = _layout(d)
    kv_off = off["a_k"]
    kv_rel = {n: off[n] - kv_off for n in ("a_k", "a_v", "c_k", "c_v")}

    xs, cs = x[0], ctx[0]
    cvec = jnp.zeros((8, d), F32).at[0].set(c[0]).at[1].set(c_ctx)
    ada = _ada(cvec, w_ada, b_ada)

    ones_bd = jnp.asarray(np.kron(np.eye(N_HEADS), np.ones((HEAD_DIM, HEAD_DIM))), BF16)
    rope_tabs = _rope_tables(seq)
    na_bias = _na_bias_tables(na_rpb, seq // GRID_W)
    row = lambda v: v.reshape(1, -1)
    qn_all = jnp.tile(q_norm, (1, N_HEADS))
    kn_all = jnp.tile(k_norm, (1, GQA_KV_HEADS))

    for l in range(depth):
        last = l == depth - 1
        w_l = _permute_cast_w_in(w_in[l], d)
        wb_l = w_branch[l].astype(BF16)
        wo_l = w_out[l].astype(BF16)
        pw_l = pool_w[l].astype(BF16)
        cpw_l = conv_pw[l].astype(BF16)
        mod = lambda r: (ada[l, r:r + 1, :d], ada[l, r:r + 1, d:2 * d], ada[l, r:r + 1, 2 * d:])
        (shift, scale, gate), (shift_c, scale_c, gate_c) = mod(0), mod(1)
        qn, kn = row(qn_all[l]), row(kn_all[l])

        hc = _modulate(cs, row(g_pre[l]), shift_c, scale_c)
        if last:
            pc = _matmul(hc, w_l[:, kv_off:])
            coff = kv_rel
        else:
            pc = _matmul(hc, w_l)
            coff = off
        qc_ctx, kq_ctx, vt_ctx = _qk_prep(pc, None if last else coff["c_q"], coff["c_k"], coff["c_v"],
                                          qn, kn, ones_bd, None)

        h = _modulate(xs, row(g_pre[l]), shift, scale)
        p = _matmul(h, w_l)
        qc, kc, vt = _qk_prep(p, off["c_q"], off["c_k"], off["c_v"], qn, kn, ones_bd, rope_tabs)
        o_c = _gqa_attn(qc, p, off["c_gate"], kc, vt, kq_ctx, vt_ctx)
        o_a = _na_attn(p, off, pc, coff["a_k"], coff["a_v"], na_bias, l)
        o_b = _pool(p, off, pw_l, row(pool_scale[l]))
        o_d = _conv(p, off, conv_w[l], row(conv_b[l]), row(conv_ln_g[l]), row(conv_ln_b[l]), cpw_l)
        y = _merge(p, off["merge"], (o_a, o_b, o_c, o_d), wb_l)
        xs_next = _out_proj(y, wo_l, xs, row(g_post[l]), gate)

        if not last:
            o_a_c = _dense_attn(pc, off["a_q"], pc, off["a_gate"],
                                [(pc, off["a_k"], pc, off["a_v"])], False, SCORE_SCALE_LOG2)
            o_c_c = _dense_attn(qc_ctx, 0, pc, off["c_gate"], [(kq_ctx, 0, pc, off["c_v"])], True, 1.0)
            o_b_c = _pool(pc, off, pw_l, row(pool_scale[l]))
            o_d_c = _conv(pc, off, conv_w[l], row(conv_b[l]), row(conv_ln_g[l]), row(conv_ln_b[l]), cpw_l)
            y_c = _merge(pc, off["merge"], (o_a_c, o_b_c, o_c_c, o_d_c), wb_l)
            cs = _out_proj(y_c, wo_l, cs, row(g_post[l]), gate_c)
        xs = xs_next
    return xs[None]
```

```python
import functools

import numpy as np
import jax
import jax.numpy as jnp
from jax import lax
from jax.experimental import pallas as pl
from jax.experimental.pallas import tpu as pltpu

F32 = jnp.float32
BF16 = jnp.bfloat16

GRID_W = 64
HEAD_DIM = 64
BRANCH_W = 512
N_BRANCH = 4
N_HEADS = BRANCH_W // HEAD_DIM
GQA_KV_HEADS = 2
KV_W = GQA_KV_HEADS * HEAD_DIM
NA_WIN_ROWS = 8
NA_WIN_COLS = 16
POOL_SIZES = (2, 4, 8, 16)
POOL_GROUP = BRANCH_W // len(POOL_SIZES)
ROPE_THETA = 10000.0
CONV_WIDTH = 31
EPS = 1e-6

LANES = 128
SUBLANES = 8
HALO = 16
NA_ROW_BLOCK = 4
NA_BAND_ROWS = 12
NEG_BIG = -1e30
LOG2_E = 1.4426950408889634
SCORE_SCALE_LOG2 = HEAD_DIM ** -0.5 * LOG2_E
V7X_VMEM_LIMIT = 52 * 1024 * 1024

PARTS = (("a_k", BRANCH_W), ("a_v", BRANCH_W), ("c_k", KV_W), ("c_v", KV_W), ("a_q", BRANCH_W), ("c_q", BRANCH_W),
         ("a_gate", BRANCH_W), ("b_in", BRANCH_W), ("b_gate", BRANCH_W), ("c_gate", BRANCH_W),
         ("d_glu", 2 * BRANCH_W), ("d_gate", BRANCH_W), ("merge", None))
KV_COLS = 2 * BRANCH_W + 2 * KV_W


def _layout(d_model):
    off, o = {}, 0
    for n, w in PARTS:
        off[n] = o
        o += N_BRANCH * d_model if w is None else w
    return off


def _params(**kw):
    return pltpu.CompilerParams(vmem_limit_bytes=V7X_VMEM_LIMIT, **kw)


def _tile(n, pref, mult):
    if n <= pref:
        return n
    t = (pref // mult) * mult
    while t >= mult:
        if n % t == 0:
            return t
        t -= mult
    raise ValueError(f"no tile for {n}")


def _resident(block_shape, index_map):
    return pl.BlockSpec(block_shape, index_map, pipeline_mode=pl.Buffered(1))


def _slab(tm, width, off):
    assert off % LANES == 0
    return pl.BlockSpec((pl.Element(tm), pl.Element(width)), lambda i: (i * tm, off))


def _silu(x):
    return x * jax.nn.sigmoid(x)


def _ada_body(cs_ref, w_ref, b_ref, o_ref):
    s = _silu(cs_ref[...]).astype(BF16)
    o_ref[0] = jnp.dot(s, w_ref[0].astype(BF16), preferred_element_type=F32) + b_ref[0]


def _ada(cs, w_ada, b_ada):
    depth, d, n = w_ada.shape
    tn = _tile(n, 1024, LANES)
    return pl.pallas_call(
        _ada_body,
        grid=(depth, n // tn),
        in_specs=[pl.BlockSpec((8, d), lambda l, j: (0, 0)),
                  pl.BlockSpec((1, d, tn), lambda l, j: (l, 0, j)),
                  pl.BlockSpec((1, 1, tn), lambda l, j: (l, 0, j))],
        out_specs=pl.BlockSpec((1, 8, tn), lambda l, j: (l, 0, j)),
        out_shape=jax.ShapeDtypeStruct((depth, 8, n), F32),
        compiler_params=_params(dimension_semantics=("arbitrary", "arbitrary")),
        name="ada",
    )(cs, w_ada, b_ada.reshape(depth, 1, n))


def _modulate_body(x_ref, g_ref, sh_ref, sc_ref, o_ref):
    x = x_ref[...]
    y = x * lax.rsqrt(jnp.mean(x * x, axis=-1, keepdims=True) + EPS) * g_ref[...]
    o_ref[...] = (y * (1.0 + sc_ref[...]) + sh_ref[...]).astype(BF16)


def _modulate(x, g, shift, scale):
    t, d = x.shape
    tm = _tile(t, 512, 16)
    vec = pl.BlockSpec((1, d), lambda i: (0, 0))
    return pl.pallas_call(
        _modulate_body,
        grid=(t // tm,),
        in_specs=[pl.BlockSpec((tm, d), lambda i: (i, 0)), vec, vec, vec],
        out_specs=pl.BlockSpec((tm, d), lambda i: (i, 0)),
        out_shape=jax.ShapeDtypeStruct((t, d), BF16),
        compiler_params=_params(dimension_semantics=("arbitrary",)),
        name="modulate",
    )(x, g, shift, scale)


def _in_proj_body(h_ref, w_ref, o_ref, wb_ref):
    @pl.when(pl.program_id(1) == 0)
    def _():
        wb_ref[...] = w_ref[0].astype(BF16)

    o_ref[...] = jnp.dot(h_ref[...], wb_ref[...], preferred_element_type=F32).astype(o_ref.dtype)


def _col_tile(n):
    for mult in (2 * LANES, LANES):
        try:
            return _tile(n, 1536, mult)
        except ValueError:
            pass
    raise ValueError(n)


def _in_proj(h, w_in, layer, n_cols=None):
    t, d = h.shape
    n = w_in.shape[2]
    tm = _tile(t, 1024, 16)
    tn = _col_tile(n)
    n_tiles = n // tn if n_cols is None else pl.cdiv(n_cols, tn)
    return pl.pallas_call(
        _in_proj_body,
        grid=(n_tiles, t // tm),
        in_specs=[pl.BlockSpec((tm, d), lambda j, i: (i, 0)),
                  pl.BlockSpec((1, d, tn), lambda j, i: (layer, 0, j))],
        out_specs=pl.BlockSpec((tm, tn), lambda j, i: (i, j)),
        out_shape=jax.ShapeDtypeStruct((t, n_tiles * tn), BF16),
        scratch_shapes=[pltpu.VMEM((d, tn), BF16)],
        compiler_params=_params(dimension_semantics=("arbitrary", "arbitrary")),
        name="in_proj",
    )(h, w_in)


def _head_meansq(x, ones_bd):
    ss = x * x
    hi = ss.astype(BF16)
    lo = (ss - hi.astype(F32)).astype(BF16)
    tot = (jnp.dot(hi, ones_bd, preferred_element_type=F32)
           + jnp.dot(lo, ones_bd, preferred_element_type=F32))
    return tot * (1.0 / HEAD_DIM)


def _rope(y, cos, sin_signed):
    w = y.shape[-1]
    lane = lax.broadcasted_iota(jnp.int32, y.shape, 1)
    nxt = pltpu.roll(y, w - 16, 1)
    prv = pltpu.roll(y, 16, 1)
    return y * cos + jnp.where((lane % 32) < 16, nxt, prv) * sin_signed


def _norm_rope(x, w, bd, rope):
    x = x * lax.rsqrt(_head_meansq(x, bd) + EPS) * w
    if rope is not None:
        reps = x.shape[-1] // LANES
        x = _rope(x, jnp.concatenate([rope[0]] * reps, axis=1), jnp.concatenate([rope[1]] * reps, axis=1))
    return x


VT_ROWS = HEAD_DIM + 16


def _qk_prep_body(*refs, use_rope, with_q):
    refs = list(refs)
    q_ref = refs.pop(0) if with_q else None
    k_ref, v_ref = refs.pop(0), refs.pop(0)
    qn_ref = refs.pop(0) if with_q else None
    kn_ref, bd_ref = refs.pop(0), refs.pop(0)
    rope = (refs.pop(0)[...], refs.pop(0)[...]) if use_rope else None
    bd = bd_ref[...]
    if with_q:
        q = _norm_rope(q_ref[...].astype(F32), qn_ref[...], bd, rope)
        refs.pop(0)[...] = (q * SCORE_SCALE_LOG2).astype(BF16)
    k = _norm_rope(k_ref[...].astype(F32), kn_ref[...], bd[:KV_W, :KV_W], rope)
    refs.pop(0)[...] = k.astype(BF16)
    vt_ref = refs.pop(0)
    vt = v_ref[...].astype(F32).T
    row = lax.broadcasted_iota(jnp.int32, (VT_ROWS - HEAD_DIM, vt.shape[1]), 0)
    tail = jnp.where(row == 0, 1.0, 0.0)
    for g in range(GQA_KV_HEADS):
        vt_ref[0, g] = jnp.concatenate([vt[HEAD_DIM * g:HEAD_DIM * (g + 1)], tail], axis=0).astype(BF16)


def _qk_prep(p, q_off, k_off, v_off, qn, kn, ones_bd, rope_tabs):
    t = p.shape[0]
    tm = _tile(t, 1024, LANES)
    use_rope, with_q = rope_tabs is not None, q_off is not None
    const = lambda shape: pl.BlockSpec(shape, lambda i: (0, 0))
    in_specs, args, out_specs, out_shape = [], [], [], []
    if with_q:
        in_specs.append(_slab(tm, BRANCH_W, q_off))
        args.append(p)
    in_specs += [_slab(tm, KV_W, k_off), _slab(tm, KV_W, v_off)]
    args += [p, p]
    if with_q:
        in_specs.append(const((1, BRANCH_W)))
        args.append(qn)
        out_specs.append(pl.BlockSpec((tm, BRANCH_W), lambda i: (i, 0)))
        out_shape.append(jax.ShapeDtypeStruct((t, BRANCH_W), BF16))
    in_specs += [const((1, KV_W)), const((BRANCH_W, BRANCH_W))]
    args += [kn, ones_bd]
    if use_rope:
        in_specs += [pl.BlockSpec((tm, LANES), lambda i: (i, 0))] * 2
        args += list(rope_tabs)
    out_specs += [pl.BlockSpec((tm, KV_W), lambda i: (i, 0)),
                  pl.BlockSpec((1, GQA_KV_HEADS, VT_ROWS, tm), lambda i: (i, 0, 0, 0))]
    out_shape += [jax.ShapeDtypeStruct((t, KV_W), BF16),
                  jax.ShapeDtypeStruct((t // tm, GQA_KV_HEADS, VT_ROWS, tm), BF16)]
    res = pl.pallas_call(
        functools.partial(_qk_prep_body, use_rope=use_rope, with_q=with_q),
        grid=(t // tm,),
        in_specs=in_specs,
        out_specs=out_specs,
        out_shape=out_shape,
        compiler_params=_params(dimension_semantics=("arbitrary",)),
        name="qk_prep",
    )(*args)
    return res if with_q else (None, res[0], res[1])


def _rope_tables(seq):
    n_rows = seq // GRID_W
    half = HEAD_DIM // 4
    freqs = ROPE_THETA ** (-jnp.arange(half, dtype=F32) / half)
    ang_r = jnp.arange(n_rows).astype(F32)[:, None] * freqs[None, :]
    ang_c = jnp.arange(GRID_W).astype(F32)[:, None] * freqs[None, :]
    per_row = lambda a: jnp.broadcast_to(a[:, None, :], (n_rows, GRID_W, half)).reshape(seq, half)
    per_col = lambda a: jnp.broadcast_to(a[None, :, :], (n_rows, GRID_W, half)).reshape(seq, half)
    cos_r, sin_r, cos_c, sin_c = (per_row(jnp.cos(ang_r)), per_row(jnp.sin(ang_r)),
                                  per_col(jnp.cos(ang_c)), per_col(jnp.sin(ang_c)))
    cos = jnp.concatenate([cos_r, cos_r, cos_c, cos_c], axis=1)
    sin = jnp.concatenate([-sin_r, sin_r, -sin_c, sin_c], axis=1)
    reps = LANES // HEAD_DIM
    return jnp.concatenate([cos] * reps, axis=1), jnp.concatenate([sin] * reps, axis=1)


def _nt_dot(a, b):
    return lax.dot_general(a, b, (((1,), (1,)), ((), ())), preferred_element_type=F32)


def _pair_queries(qp, low, shared_kv_lanes):
    zero = jnp.zeros_like(qp)
    if shared_kv_lanes is None:
        return jnp.where(low, qp, zero), jnp.where(low, zero, qp)
    qr = pltpu.roll(qp, HEAD_DIM, 1)
    if shared_kv_lanes == 0:
        return jnp.where(low, qp, zero), jnp.where(low, qr, zero)
    return jnp.where(low, zero, qr), jnp.where(low, zero, qp)


def _values_with_ones(vc, half):
    lane = lax.broadcasted_iota(jnp.int32, vc.shape, 1)
    keep = (lane < HEAD_DIM) if half == 0 else (lane >= HEAD_DIM)
    ones_lane = HEAD_DIM * (1 - half)
    fill = jnp.where(lane == ones_lane, 1.0, 0.0)
    return jnp.where(keep, vc.astype(F32), fill).astype(vc.dtype)


def _dense_attn_body(*refs, n_src, chunks, kv_grouped, q_scale, tq):
    n_pair = N_HEADS // 2
    q_ref, gate_ref = refs[0], refs[1]
    srcs = [(refs[2 + 2 * i], refs[3 + 2 * i]) for i in range(n_src)]
    o_ref = refs[2 + 2 * n_src]
    scratch = refs[3 + 2 * n_src:]
    m_s, acc_s = scratch[:n_pair], scratch[n_pair:]
    low_q = lax.broadcasted_iota(jnp.int32, (tq, LANES), 1) < HEAD_DIM

    halves = [((2 * p) // (N_HEADS // GQA_KV_HEADS),) * 2 if kv_grouped else (0, 1) for p in range(n_pair)]
    lhs = []
    for p in range(n_pair):
        qp = q_ref[:, LANES * p:LANES * (p + 1)].astype(F32) * q_scale
        qa, qb = _pair_queries(qp, low_q, halves[p][0] if kv_grouped else None)
        lhs.append(jnp.concatenate([qa, qb], axis=0).astype(BF16))
        m_s[p][...] = jnp.full(m_s[p].shape, NEG_BIG, F32)
        acc_s[p][...] = jnp.zeros(acc_s[p].shape, F32)

    def step(p, kc, va, vb):
        s = _nt_dot(lhs[p], kc)
        m_old = m_s[p][...]
        m_new = jnp.maximum(m_old, jnp.max(s, axis=-1, keepdims=True))
        alpha = jnp.exp2(m_old - m_new)
        pr = jnp.exp2(s - m_new).astype(BF16)
        acc = acc_s[p]
        acc[:tq] = alpha[:tq] * acc[:tq] + jnp.dot(pr[:tq], va, preferred_element_type=F32)
        acc[tq:] = alpha[tq:] * acc[tq:] + jnp.dot(pr[tq:], vb, preferred_element_type=F32)
        m_s[p][...] = m_new

    def all_pairs(k_ref, v_ref, rows):
        if kv_grouped:
            kc, vc = k_ref[rows, :], v_ref[rows, :]
            vals = [_values_with_ones(vc, h) for h in range(GQA_KV_HEADS)]
        for p in range(n_pair):
            if kv_grouped:
                step(p, kc, vals[halves[p][0]], vals[halves[p][1]])
            else:
                cols = slice(LANES * p, LANES * (p + 1))
                vc = v_ref[rows, cols]
                step(p, k_ref[rows, cols], _values_with_ones(vc, 0), _values_with_ones(vc, 1))

    for (k_ref, v_ref), ck in zip(srcs, chunks):
        n_chunk = k_ref.shape[0] // ck
        if n_chunk == 1:
            all_pairs(k_ref, v_ref, slice(None))
        else:
            def loop(i, carry, k_ref=k_ref, v_ref=v_ref, ck=ck):
                all_pairs(k_ref, v_ref, pl.ds(pl.multiple_of(i * ck, ck), ck))
                return carry
            lax.fori_loop(0, n_chunk, loop, 0)

    for p in range(n_pair):
        cols = slice(LANES * p, LANES * (p + 1))
        heads = []
        for hd, half in enumerate(halves[p]):
            acc = acc_s[p][hd * tq:(hd + 1) * tq]
            ones_lane = HEAD_DIM * (1 - half)
            o = acc / acc[:, ones_lane:ones_lane + 1]
            heads.append(o if half == hd else pltpu.roll(o, HEAD_DIM, 1))
        o = jnp.where(low_q, heads[0], heads[1])
        o_ref[:, cols] = (o * _silu(gate_ref[:, cols].astype(F32))).astype(BF16)


def _key_chunk(tk):
    for mult in (2 * LANES, LANES, 16):
        try:
            return _tile(tk, 1024, mult)
        except ValueError:
            pass
    raise ValueError(tk)


def _dense_attn(q, q_off, gate, gate_off, kv_srcs, kv_grouped, q_scale):
    t = q.shape[0]
    tq = _tile(t, 256, 16)
    kw = KV_W if kv_grouped else BRANCH_W
    in_specs = [_slab(tq, BRANCH_W, q_off), _slab(tq, BRANCH_W, gate_off)]
    args = [q, gate]
    chunks = []
    for k_arr, k_off, v_arr, v_off in kv_srcs:
        tk = k_arr.shape[0]
        assert k_off % kw == 0 and v_off % kw == 0
        in_specs += [_resident((tk, kw), lambda i, _c=k_off // kw: (0, _c)),
                     _resident((tk, kw), lambda i, _c=v_off // kw: (0, _c))]
        args += [k_arr, v_arr]
        chunks.append(_key_chunk(tk))
    n_pair = N_HEADS // 2
    return pl.pallas_call(
        functools.partial(_dense_attn_body, n_src=len(kv_srcs), chunks=tuple(chunks),
                          kv_grouped=kv_grouped, q_scale=q_scale, tq=tq),
        grid=(t // tq,),
        in_specs=in_specs,
        out_specs=pl.BlockSpec((tq, BRANCH_W), lambda i: (i, 0)),
        out_shape=jax.ShapeDtypeStruct((t, BRANCH_W), BF16),
        scratch_shapes=[pltpu.VMEM((2 * tq, 1), F32)] * n_pair + [pltpu.VMEM((2 * tq, LANES), F32)] * n_pair,
        compiler_params=_params(dimension_semantics=("arbitrary",)),
        name="dense_attn",
    )(*args)


def _gqa_body(q_ref, gate_ref, k_ref, vt_ref, kx_ref, vtx_ref, o_ref, *scratch, tq, n_chunk):
    n_pair = N_HEADS // 2
    qt_s, m_s, acc_s, sx_s = (scratch[i * n_pair:(i + 1) * n_pair] for i in range(4))
    s_s = [scratch[4 * n_pair + 2 * p:4 * n_pair + 2 * p + 2] for p in range(n_pair)]
    low_q = lax.broadcasted_iota(jnp.int32, (tq, LANES), 1) < HEAD_DIM
    kv_of = [(2 * p) // (N_HEADS // GQA_KV_HEADS) for p in range(n_pair)]

    def scores(p, keys, s_ref):
        s_ref[...] = jnp.dot(keys, qt_s[p][...], preferred_element_type=F32)

    def consume(p, s_ref, vt):
        s = s_ref[...]
        m_old = m_s[p][...]
        m_new = jnp.maximum(m_old, jnp.max(s, axis=0, keepdims=True))
        alpha = jnp.exp2(m_old - m_new)
        pt = jnp.exp2(s - m_new).astype(BF16)
        for hd in range(2):
            cols = slice(hd * tq, (hd + 1) * tq)
            acc_s[p][hd] = alpha[:, cols] * acc_s[p][hd] + jnp.dot(vt, pt[:, cols], preferred_element_type=F32)
        m_s[p][...] = m_new

    def step(i, slot):
        for p in range(n_pair):
            scores(p, k_ref[i + 1], s_s[p][1 - slot])
            consume(p, s_s[p][slot], vt_ref[i, kv_of[p]])

    for p in range(n_pair):
        qp = q_ref[:, LANES * p:LANES * (p + 1)].astype(F32)
        qa, qb = _pair_queries(qp, low_q, kv_of[p])
        qt_s[p][...] = jnp.concatenate([qa, qb], axis=0).T.astype(BF16)
        m_s[p][...] = jnp.full(m_s[p].shape, NEG_BIG, F32)
        acc_s[p][...] = jnp.zeros(acc_s[p].shape, F32)
        scores(p, k_ref[0], s_s[p][0])

    def two_steps(j, carry):
        step(2 * j, 0)
        step(2 * j + 1, 1)
        return carry

    n_step = n_chunk - 1
    lax.fori_loop(0, n_step // 2, two_steps, 0)
    last = n_chunk - 1
    if n_step % 2:
        step(last - 1, 0)
    for p in range(n_pair):
        scores(p, kx_ref[...], sx_s[p])
        consume(p, s_s[p][last % 2], vt_ref[last, kv_of[p]])
    for p in range(n_pair):
        consume(p, sx_s[p], vtx_ref[0, kv_of[p]])
        acc = acc_s[p]
        ot = jnp.concatenate([acc[hd, :HEAD_DIM] / acc[hd, HEAD_DIM:HEAD_DIM + 1] for hd in range(2)], axis=0)
        cols = slice(LANES * p, LANES * (p + 1))
        o_ref[:, cols] = (ot.T * _silu(gate_ref[:, cols].astype(F32))).astype(BF16)


def _gqa_attn(q, gate, gate_off, k, vt, k_extra, vt_extra):
    t = q.shape[0]
    n_chunk, _, _, ck = vt.shape
    tx = k_extra.shape[0]
    assert vt_extra.shape[0] == 1 and k.shape[0] == n_chunk * ck
    tq = _tile(t, 256, LANES)
    n_pair = N_HEADS // 2
    scratch = ([pltpu.VMEM((LANES, 2 * tq), BF16)] * n_pair + [pltpu.VMEM((1, 2 * tq), F32)] * n_pair
               + [pltpu.VMEM((2, VT_ROWS, tq), F32)] * n_pair + [pltpu.VMEM((tx, 2 * tq), F32)] * n_pair
               + [pltpu.VMEM((ck, 2 * tq), F32)] * (2 * n_pair))
    return pl.pallas_call(
        functools.partial(_gqa_body, tq=tq, n_chunk=n_chunk),
        grid=(t // tq,),
        in_specs=[pl.BlockSpec((tq, BRANCH_W), lambda i: (i, 0)), _slab(tq, BRANCH_W, gate_off),
                  _resident((n_chunk, ck, KV_W), lambda i: (0, 0, 0)),
                  _resident(vt.shape, lambda i: (0, 0, 0, 0)),
                  _resident((tx, KV_W), lambda i: (0, 0)),
                  _resident(vt_extra.shape, lambda i: (0, 0, 0, 0))],
        out_specs=pl.BlockSpec((tq, BRANCH_W), lambda i: (i, 0)),
        out_shape=jax.ShapeDtypeStruct((t, BRANCH_W), BF16),
        scratch_shapes=scratch,
        compiler_params=_params(dimension_semantics=("arbitrary",)),
        name="gqa_attn",
    )(q, gate, k.reshape(n_chunk, ck, KV_W), vt, k_extra, vt_extra)


def _na_block_geometry(n_rows):
    rb, band = NA_ROW_BLOCK, NA_BAND_ROWS
    assert n_rows % rb == 0 and n_rows >= band and band >= rb + NA_WIN_ROWS - 1
    nb = n_rows // rb
    r0 = np.arange(nb) * rb
    u0 = np.clip(r0 - NA_WIN_ROWS // 2, 0, n_rows - band)
    r = r0[:, None] + np.arange(rb)[None, :]
    row_start = np.clip(r - NA_WIN_ROWS // 2, 0, n_rows - NA_WIN_ROWS)
    key_row = u0[:, None] + np.arange(band)[None, :]
    rel = key_row[:, None, :] - row_start[:, :, None]
    valid = (rel >= 0) & (rel < NA_WIN_ROWS)
    row_off = np.where(valid, key_row[:, None, :] - r[:, :, None] + NA_WIN_ROWS - 1, 0)
    return u0, row_off, valid


def _na_bias_tables(rpb, n_rows):
    u0, row_off, valid = _na_block_geometry(n_rows)
    nb = row_off.shape[0]
    keys = [row_off[b].tobytes() + valid[b].tobytes() for b in range(nb)]
    cases, case_of = [], []
    for b in range(nb):
        if keys[b] not in [keys[c] for c in cases]:
            cases.append(b)
        case_of.append([keys[c] for c in cases].index(keys[b]))
    assert case_of == [0] + [1] * (nb - 2) + [2], case_of
    cq = np.arange(GRID_W)
    col_start = np.clip(cq - NA_WIN_COLS // 2, 0, GRID_W - NA_WIN_COLS)
    col_ok = (cq[None, :] >= col_start[:, None]) & (cq[None, :] < col_start[:, None] + NA_WIN_COLS)
    col_off = np.clip(cq[None, :] - cq[:, None], -(NA_WIN_COLS - 1), NA_WIN_COLS - 1) + NA_WIN_COLS - 1
    onehot = jnp.asarray(col_off[:, :, None] == np.arange(2 * NA_WIN_COLS - 1), F32)
    tiles = jnp.einsum("lhrm,qkm->lhrqk", rpb.astype(F32), onehot, precision=lax.Precision.HIGHEST)
    tiles = jnp.where(col_ok, tiles * LOG2_E, NEG_BIG)
    masked = jnp.full(tiles.shape[:2] + (GRID_W, GRID_W), NEG_BIG, F32)
    blocks = []
    for c in cases:
        rows = [jnp.concatenate([tiles[:, :, row_off[c, i, j]] if valid[c, i, j] else masked
                                 for j in range(NA_BAND_ROWS)], axis=-1) for i in range(NA_ROW_BLOCK)]
        blocks.append(jnp.stack(rows, axis=2))
    bias = jnp.stack(blocks, axis=1)
    return bias.reshape(rpb.shape[0], 3, N_HEADS * NA_ROW_BLOCK * GRID_W, NA_BAND_ROWS * GRID_W)


def _na_body(q_ref, gate_ref, k_ref, v_ref, kc_ref, vc_ref, bias_ref, o_ref, *, n_rows):
    rb, band = NA_ROW_BLOCK, NA_BAND_ROWS
    tq, nk = rb * GRID_W, band * GRID_W
    b = pl.program_id(0)
    u0 = jnp.clip(b * rb - NA_WIN_ROWS // 2, 0, n_rows - band)
    rows = pl.ds(pl.multiple_of(u0 * GRID_W, GRID_W), nk)
    low_q = lax.broadcasted_iota(jnp.int32, (tq, LANES), 1) < HEAD_DIM
    for p in range(N_HEADS // 2):
        cols = slice(LANES * p, LANES * (p + 1))
        qp = q_ref[:, cols].astype(F32) * SCORE_SCALE_LOG2
        qa, qb = _pair_queries(qp, low_q, None)
        lhs = jnp.concatenate([qa, qb], axis=0).astype(BF16)
        kb, vb = k_ref[rows, cols], v_ref[rows, cols]
        kc, vc = kc_ref[:, cols], vc_ref[:, cols]
        s_band = _nt_dot(lhs, kb) + bias_ref[0, 0, 2 * tq * p:2 * tq * (p + 1), :]
        s_ctx = _nt_dot(lhs, kc)
        m = jnp.maximum(jnp.max(s_band, axis=-1, keepdims=True), jnp.max(s_ctx, axis=-1, keepdims=True))
        pb = jnp.exp2(s_band - m).astype(BF16)
        pc = jnp.exp2(s_ctx - m).astype(BF16)
        heads = []
        for hd in range(2):
            r = slice(hd * tq, (hd + 1) * tq)
            acc = (jnp.dot(pb[r], _values_with_ones(vb, hd), preferred_element_type=F32)
                   + jnp.dot(pc[r], _values_with_ones(vc, hd), preferred_element_type=F32))
            ones_lane = HEAD_DIM * (1 - hd)
            heads.append(acc / acc[:, ones_lane:ones_lane + 1])
        o = jnp.where(low_q, heads[0], heads[1])
        o_ref[:, cols] = (o * _silu(gate_ref[:, cols].astype(F32))).astype(BF16)


def _na_attn(p, off, pc, bias, layer):
    s = p.shape[0]
    cn = pc.shape[0]
    n_rows = s // GRID_W
    tq = NA_ROW_BLOCK * GRID_W
    nb = n_rows // NA_ROW_BLOCK
    w = BRANCH_W
    return pl.pallas_call(
        functools.partial(_na_body, n_rows=n_rows),
        grid=(nb,),
        in_specs=[_slab(tq, w, off["a_q"]), _slab(tq, w, off["a_gate"]),
                  _resident((s, w), lambda i, _c=off["a_k"] // w: (0, _c)),
                  _resident((s, w), lambda i, _c=off["a_v"] // w: (0, _c)),
                  _resident((cn, w), lambda i, _c=off["a_k"] // w: (0, _c)),
                  _resident((cn, w), lambda i, _c=off["a_v"] // w: (0, _c)),
                  pl.BlockSpec((1, 1) + bias.shape[2:],
                               lambda i: (layer, jnp.where(i == 0, 0, jnp.where(i == nb - 1, 2, 1)), 0, 0))],
        out_specs=pl.BlockSpec((tq, w), lambda i: (i, 0)),
        out_shape=jax.ShapeDtypeStruct((s, w), BF16),
        compiler_params=_params(dimension_semantics=("arbitrary",)),
        name="na_attn",
    )(p, p, p, p, pc, pc, bias)


def _halo_specs(t, tm, width, off):
    assert off % LANES == 0 and tm % HALO == 0
    block = lambda rows: (pl.Element(rows), pl.Element(width))
    per, last = tm // HALO, t // HALO - 1
    return [pl.BlockSpec(block(tm), lambda i: (i * tm, off)),
            pl.BlockSpec(block(HALO), lambda i: (jnp.maximum(i * per - 1, 0) * HALO, off)),
            pl.BlockSpec(block(HALO), lambda i: (jnp.minimum((i + 1) * per, last) * HALO, off))]


def _fill_padded(pad_ref, cur, prev, nxt, tm):
    i, n = pl.program_id(0), pl.num_programs(0)
    pad_ref[0:HALO] = jnp.where(i > 0, prev, jnp.zeros_like(prev))
    pad_ref[HALO:HALO + tm] = cur
    pad_ref[HALO + tm:] = jnp.where(i < n - 1, nxt, jnp.zeros_like(nxt))


def _pool_body(u_ref, up_ref, un_ref, gate_ref, w_ref, sc_ref, o_ref, pad_ref, *, tm, seq):
    _fill_padded(pad_ref, u_ref[...].astype(F32), up_ref[...].astype(F32), un_ref[...].astype(F32), tm)
    t = pl.program_id(0) * tm + lax.broadcasted_iota(jnp.int32, (tm, POOL_GROUP), 0)
    for gi, ksz in enumerate(POOL_SIZES):
        cols = slice(POOL_GROUP * gi, POOL_GROUP * (gi + 1))
        back = ksz // 2
        tot = pad_ref[HALO - back:HALO - back + tm, cols]
        for d in range(1 - back, ksz - back):
            tot = tot + pad_ref[HALO + d:HALO + d + tm, cols]
        lo = jnp.maximum(t - back, 0)
        hi = jnp.minimum(t + (ksz - 1 - back), seq - 1)
        mean = tot / (hi - lo + 1).astype(F32)
        dlt = (mean - pad_ref[HALO:HALO + tm, cols]).astype(BF16)
        y = jnp.dot(dlt, w_ref[gi], preferred_element_type=F32) * sc_ref[:, cols]
        o_ref[:, cols] = (y * _silu(gate_ref[:, cols].astype(F32))).astype(BF16)


def _pool(p, off, w_pool, pool_scale):
    t = p.shape[0]
    tm = _tile(t, 512, HALO)
    w = BRANCH_W
    return pl.pallas_call(
        functools.partial(_pool_body, tm=tm, seq=t),
        grid=(t // tm,),
        in_specs=_halo_specs(t, tm, w, off["b_in"]) + [
            _slab(tm, w, off["b_gate"]),
            pl.BlockSpec(w_pool.shape, lambda i: (0, 0, 0)),
            pl.BlockSpec((1, w), lambda i: (0, 0))],
        out_specs=pl.BlockSpec((tm, w), lambda i: (i, 0)),
        out_shape=jax.ShapeDtypeStruct((t, w), BF16),
        scratch_shapes=[pltpu.VMEM((tm + 2 * HALO, w), F32)],
        compiler_params=_params(dimension_semantics=("arbitrary",)),
        name="pool",
    )(p, p, p, p, w_pool, pool_scale)


def _glu(x):
    x = x.astype(F32)
    return x[:, :BRANCH_W] * jax.nn.sigmoid(x[:, BRANCH_W:])


def _conv_body(x_ref, xp_ref, xn_ref, gate_ref, cw_ref, cb_ref, lg_ref, lb_ref, pw_ref, o_ref, pad_ref, sh_ref,
               *, tm):
    _fill_padded(pad_ref, _glu(x_ref[...]), _glu(xp_ref[...]), _glu(xn_ref[...]), tm)
    span = sh_ref.shape[1]
    for b in range(1, SUBLANES):
        sh_ref[b - 1] = pad_ref[b:b + span, :]
    reach = CONV_WIDTH // 2
    y = jnp.zeros((tm, BRANCH_W), F32) + cb_ref[...]
    for j in range(CONV_WIDTH):
        a, b = divmod(HALO - reach + j, SUBLANES)
        rows = slice(SUBLANES * a, SUBLANES * a + tm)
        y = y + (pad_ref[rows, :] if b == 0 else sh_ref[b - 1, rows, :]) * cw_ref[j:j + 1, :]
    mu = jnp.mean(y, axis=-1, keepdims=True)
    yc = y - mu
    var = jnp.mean(yc * yc, axis=-1, keepdims=True)
    z = _silu(yc * lax.rsqrt(var + EPS) * lg_ref[...] + lb_ref[...]).astype(BF16)
    out = jnp.dot(z, pw_ref[...], preferred_element_type=F32)
    o_ref[...] = (out * _silu(gate_ref[...].astype(F32))).astype(BF16)


def _conv(p, off, conv_w, conv_b, ln_g, ln_b, w_pw):
    t = p.shape[0]
    tm = _tile(t, 512, HALO)
    w = BRANCH_W
    vec = pl.BlockSpec((1, w), lambda i: (0, 0))
    return pl.pallas_call(
        functools.partial(_conv_body, tm=tm),
        grid=(t // tm,),
        in_specs=_halo_specs(t, tm, 2 * w, off["d_glu"]) + [
            _slab(tm, w, off["d_gate"]),
            pl.BlockSpec(conv_w.shape, lambda i: (0, 0)), vec, vec, vec,
            pl.BlockSpec((w, w), lambda i: (0, 0))],
        out_specs=pl.BlockSpec((tm, w), lambda i: (i, 0)),
        out_shape=jax.ShapeDtypeStruct((t, w), BF16),
        scratch_shapes=[pltpu.VMEM((tm + 2 * HALO, w), F32),
                        pltpu.VMEM((SUBLANES - 1, tm + 2 * HALO - SUBLANES, w), F32)],
        compiler_params=_params(dimension_semantics=("arbitrary",)),
        name="conv",
    )(p, p, p, p, conv_w, conv_b, ln_g, ln_b, w_pw)


def _merge_body(m0, m1, m2, m3, o0, o1, o2, o3, wb_ref, y_ref):
    y = None
    for bi, (m_ref, o_ref) in enumerate(((m0, o0), (m1, o1), (m2, o2), (m3, o3))):
        proj = jnp.dot(o_ref[...], wb_ref[bi], preferred_element_type=F32)
        term = (1.0 + jnp.tanh(0.5 * m_ref[...].astype(F32))) * proj
        y = term if y is None else y + term
    y_ref[...] = (0.5 * y).astype(BF16)


def _merge(p, merge_off, outs, w_branch):
    t = p.shape[0]
    d = w_branch.shape[-1]
    tm = _tile(t, 256, 16)
    return pl.pallas_call(
        _merge_body,
        grid=(t // tm,),
        in_specs=[_slab(tm, d, merge_off + bi * d) for bi in range(N_BRANCH)]
        + [pl.BlockSpec((tm, BRANCH_W), lambda i: (i, 0))] * N_BRANCH
        + [_resident(w_branch.shape, lambda i: (0, 0, 0))],
        out_specs=pl.BlockSpec((tm, d), lambda i: (i, 0)),
        out_shape=jax.ShapeDtypeStruct((t, d), BF16),
        compiler_params=_params(dimension_semantics=("arbitrary",)),
        name="merge",
    )(p, p, p, p, *outs, w_branch)


def _out_body(y_ref, w_ref, x_ref, g_ref, gate_ref, o_ref):
    y = jnp.dot(y_ref[...], w_ref[...], preferred_element_type=F32)
    yn = y * lax.rsqrt(jnp.mean(y * y, axis=-1, keepdims=True) + EPS) * g_ref[...]
    o_ref[...] = x_ref[...] + gate_ref[...] * yn


def _out_proj(y, w_out, x, g_post, gate):
    t, d = x.shape
    tm = _tile(t, 256, 16)
    vec = pl.BlockSpec((1, d), lambda i: (0, 0))
    row = pl.BlockSpec((tm, d), lambda i: (i, 0))
    return pl.pallas_call(
        _out_body,
        grid=(t // tm,),
        in_specs=[row, _resident((d, d), lambda i: (0, 0)), row, vec, vec],
        out_specs=row,
        out_shape=jax.ShapeDtypeStruct((t, d), F32),
        compiler_params=_params(dimension_semantics=("arbitrary",)),
        name="out_proj",
    )(y, w_out, x, g_post, gate)


def kernel(x, c, ctx, c_ctx, w_ada, b_ada, g_pre, g_post, w_in, na_rpb, pool_w, pool_scale,
           q_norm, k_norm, conv_w, conv_b, conv_ln_g, conv_ln_b, conv_pw, w_branch, w_out):
    batch, seq, d = x.shape
    assert batch == 1 and seq % GRID_W == 0
    cn = ctx.shape[1]
    depth = w_ada.shape[0]
    off = _layout(d)

    xs, cs = x[0], ctx[0]
    cvec = jnp.zeros((8, d), F32).at[0].set(c[0]).at[1].set(c_ctx)
    ada = _ada(cvec, w_ada, b_ada)

    ones_bd = jnp.asarray(np.kron(np.eye(N_HEADS), np.ones((HEAD_DIM, HEAD_DIM))), BF16)
    rope_tabs = _rope_tables(seq)
    na_bias = _na_bias_tables(na_rpb, seq // GRID_W)
    row = lambda v: v.reshape(1, -1)
    qn_all = jnp.tile(q_norm, (1, N_HEADS))
    kn_all = jnp.tile(k_norm, (1, GQA_KV_HEADS))

    for l in range(depth):
        last = l == depth - 1
        wb_l = w_branch[l].astype(BF16)
        wo_l = w_out[l].astype(BF16)
        pw_l = pool_w[l].astype(BF16)
        cpw_l = conv_pw[l].astype(BF16)
        mod = lambda r: (ada[l, r:r + 1, :d], ada[l, r:r + 1, d:2 * d], ada[l, r:r + 1, 2 * d:])
        (shift, scale, gate), (shift_c, scale_c, gate_c) = mod(0), mod(1)
        qn, kn = row(qn_all[l]), row(kn_all[l])

        hc = _modulate(cs, row(g_pre[l]), shift_c, scale_c)
        pc = _in_proj(hc, w_in, l, KV_COLS if last else None)
        qc_ctx, kq_ctx, vt_ctx = _qk_prep(pc, None if last else off["c_q"], off["c_k"], off["c_v"],
                                          qn, kn, ones_bd, None)

        h = _modulate(xs, row(g_pre[l]), shift, scale)
        p = _in_proj(h, w_in, l)
        qc, kc, vt = _qk_prep(p, off["c_q"], off["c_k"], off["c_v"], qn, kn, ones_bd, rope_tabs)
        o_c = _gqa_attn(qc, p, off["c_gate"], kc, vt, kq_ctx, vt_ctx)
        o_a = _na_attn(p, off, pc, na_bias, l)
        o_b = _pool(p, off, pw_l, row(pool_scale[l]))
        o_d = _conv(p, off, conv_w[l], row(conv_b[l]), row(conv_ln_g[l]), row(conv_ln_b[l]), cpw_l)
        y = _merge(p, off["merge"], (o_a, o_b, o_c, o_d), wb_l)
        xs_next = _out_proj(y, wo_l, xs, row(g_post[l]), gate)

        if not last:
            o_a_c = _dense_attn(pc, off["a_q"], pc, off["a_gate"],
                                [(pc, off["a_k"], pc, off["a_v"])], False, SCORE_SCALE_LOG2)
            o_c_c = _dense_attn(qc_ctx, 0, pc, off["c_gate"], [(kq_ctx, 0, pc, off["c_v"])], True, 1.0)
            o_b_c = _pool(pc, off, pw_l, row(pool_scale[l]))
            o_d_c = _conv(pc, off, conv_w[l], row(conv_b[l]), row(conv_ln_g[l]), row(conv_ln_b[l]), cpw_l)
            y_c = _merge(pc, off["merge"], (o_a_c, o_b_c, o_c_c, o_d_c), wb_l)
            cs = _out_proj(y_c, wo_l, cs, row(g_post[l]), gate_c)
        xs = xs_next
    return xs[None]
```

```python
import functools

import numpy as np
import jax
import jax.numpy as jnp
from jax import lax
from jax.experimental import pallas as pl
from jax.experimental.pallas import tpu as pltpu

F32 = jnp.float32
BF16 = jnp.bfloat16

GRID_W = 64
HEAD_DIM = 64
BRANCH_W = 512
N_BRANCH = 4
N_HEADS = BRANCH_W // HEAD_DIM
GQA_KV_HEADS = 2
KV_W = GQA_KV_HEADS * HEAD_DIM
NA_WIN_ROWS = 8
NA_WIN_COLS = 16
POOL_SIZES = (2, 4, 8, 16)
POOL_GROUP = BRANCH_W // len(POOL_SIZES)
ROPE_THETA = 10000.0
CONV_WIDTH = 31
EPS = 1e-6

LANES = 128
SUBLANES = 8
HALO = 16
NA_ROW_BLOCK = 4
NA_BAND_ROWS = 12
NEG_BIG = -1e30
LOG2_E = 1.4426950408889634
SCORE_SCALE_LOG2 = HEAD_DIM ** -0.5 * LOG2_E
V7X_VMEM_LIMIT = 52 * 1024 * 1024

PARTS = (("a_k", BRANCH_W), ("a_v", BRANCH_W), ("c_k", KV_W), ("c_v", KV_W), ("a_q", BRANCH_W), ("c_q", BRANCH_W),
         ("a_gate", BRANCH_W), ("b_in", BRANCH_W), ("b_gate", BRANCH_W), ("c_gate", BRANCH_W),
         ("d_glu", 2 * BRANCH_W), ("d_gate", BRANCH_W), ("merge", None))
KV_COLS = 2 * BRANCH_W + 2 * KV_W


def _layout(d_model):
    off, o = {}, 0
    for n, w in PARTS:
        off[n] = o
        o += N_BRANCH * d_model if w is None else w
    return off


def _params(**kw):
    return pltpu.CompilerParams(vmem_limit_bytes=V7X_VMEM_LIMIT, **kw)


def _tile(n, pref, mult):
    if n <= pref:
        return n
    t = (pref // mult) * mult
    while t >= mult:
        if n % t == 0:
            return t
        t -= mult
    raise ValueError(f"no tile for {n}")


def _resident(block_shape, index_map):
    return pl.BlockSpec(block_shape, index_map, pipeline_mode=pl.Buffered(1))


def _slab(tm, width, off):
    assert off % LANES == 0
    return pl.BlockSpec((pl.Element(tm), pl.Element(width)), lambda i: (i * tm, off))


def _silu(x):
    return x * jax.nn.sigmoid(x)


def _ada_body(cs_ref, w_ref, b_ref, o_ref):
    s = _silu(cs_ref[...]).astype(BF16)
    o_ref[0] = jnp.dot(s, w_ref[0].astype(BF16), preferred_element_type=F32) + b_ref[0]


def _ada(cs, w_ada, b_ada):
    depth, d, n = w_ada.shape
    tn = _tile(n, 1024, LANES)
    return pl.pallas_call(
        _ada_body,
        grid=(depth, n // tn),
        in_specs=[pl.BlockSpec((8, d), lambda l, j: (0, 0)),
                  pl.BlockSpec((1, d, tn), lambda l, j: (l, 0, j)),
                  pl.BlockSpec((1, 1, tn), lambda l, j: (l, 0, j))],
        out_specs=pl.BlockSpec((1, 8, tn), lambda l, j: (l, 0, j)),
        out_shape=jax.ShapeDtypeStruct((depth, 8, n), F32),
        compiler_params=_params(dimension_semantics=("arbitrary", "arbitrary")),
        name="ada",
    )(cs, w_ada, b_ada.reshape(depth, 1, n))


def _modulated(x, g, shift, scale):
    y = x * lax.rsqrt(jnp.mean(x * x, axis=-1, keepdims=True) + EPS) * g
    return (y * (1.0 + scale) + shift).astype(BF16)


def _modulate_body(x_ref, g_ref, sh_ref, sc_ref, o_ref):
    o_ref[...] = _modulated(x_ref[...], g_ref[...], sh_ref[...], sc_ref[...])


def _modulate(x, g, shift, scale):
    t, d = x.shape
    tm = _tile(t, 512, 16)
    vec = pl.BlockSpec((1, d), lambda i: (0, 0))
    return pl.pallas_call(
        _modulate_body,
        grid=(t // tm,),
        in_specs=[pl.BlockSpec((tm, d), lambda i: (i, 0)), vec, vec, vec],
        out_specs=pl.BlockSpec((tm, d), lambda i: (i, 0)),
        out_shape=jax.ShapeDtypeStruct((t, d), BF16),
        compiler_params=_params(dimension_semantics=("arbitrary",)),
        name="modulate",
    )(x, g, shift, scale)


def _in_proj_body(h_ref, w_ref, o_ref, wb_ref):
    @pl.when(pl.program_id(1) == 0)
    def _():
        wb_ref[...] = w_ref[0].astype(BF16)

    o_ref[...] = jnp.dot(h_ref[...], wb_ref[...], preferred_element_type=F32).astype(o_ref.dtype)


def _col_tile(n):
    for mult in (2 * LANES, LANES):
        try:
            return _tile(n, 1536, mult)
        except ValueError:
            pass
    raise ValueError(n)


def _in_proj(h, w_in, layer, n_cols=None):
    t, d = h.shape
    n = w_in.shape[2]
    tm = _tile(t, 1024, 16)
    tn = _col_tile(n)
    n_tiles = n // tn if n_cols is None else pl.cdiv(n_cols, tn)
    return pl.pallas_call(
        _in_proj_body,
        grid=(n_tiles, t // tm),
        in_specs=[pl.BlockSpec((tm, d), lambda j, i: (i, 0)),
                  pl.BlockSpec((1, d, tn), lambda j, i: (layer, 0, j))],
        out_specs=pl.BlockSpec((tm, tn), lambda j, i: (i, j)),
        out_shape=jax.ShapeDtypeStruct((t, n_tiles * tn), BF16),
        scratch_shapes=[pltpu.VMEM((d, tn), BF16)],
        compiler_params=_params(dimension_semantics=("arbitrary", "arbitrary")),
        name="in_proj",
    )(h, w_in)


def _head_meansq(x, ones_bd):
    ss = x * x
    hi = ss.astype(BF16)
    lo = (ss - hi.astype(F32)).astype(BF16)
    tot = (jnp.dot(hi, ones_bd, preferred_element_type=F32)
           + jnp.dot(lo, ones_bd, preferred_element_type=F32))
    return tot * (1.0 / HEAD_DIM)


def _rope(y, cos, sin_signed):
    w = y.shape[-1]
    lane = lax.broadcasted_iota(jnp.int32, y.shape, 1)
    nxt = pltpu.roll(y, w - 16, 1)
    prv = pltpu.roll(y, 16, 1)
    return y * cos + jnp.where((lane % 32) < 16, nxt, prv) * sin_signed


def _norm_rope(x, w, bd, rope):
    x = x * lax.rsqrt(_head_meansq(x, bd) + EPS) * w
    if rope is not None:
        reps = x.shape[-1] // LANES
        x = _rope(x, jnp.concatenate([rope[0]] * reps, axis=1), jnp.concatenate([rope[1]] * reps, axis=1))
    return x


VT_ROWS = HEAD_DIM + 16


def _qk_prep_body(*refs, use_rope, with_q):
    refs = list(refs)
    q_ref = refs.pop(0) if with_q else None
    k_ref, v_ref = refs.pop(0), refs.pop(0)
    qn_ref = refs.pop(0) if with_q else None
    kn_ref, bd_ref = refs.pop(0), refs.pop(0)
    rope = (refs.pop(0)[...], refs.pop(0)[...]) if use_rope else None
    bd = bd_ref[...]
    if with_q:
        q = _norm_rope(q_ref[...].astype(F32), qn_ref[...], bd, rope)
        refs.pop(0)[...] = (q * SCORE_SCALE_LOG2).astype(BF16)
    k = _norm_rope(k_ref[...].astype(F32), kn_ref[...], bd[:KV_W, :KV_W], rope)
    refs.pop(0)[...] = k.astype(BF16)
    vt_ref = refs.pop(0)
    vt = v_ref[...].astype(F32).T
    row = lax.broadcasted_iota(jnp.int32, (VT_ROWS - HEAD_DIM, vt.shape[1]), 0)
    tail = jnp.where(row == 0, 1.0, 0.0)
    for g in range(GQA_KV_HEADS):
        vt_ref[0, g] = jnp.concatenate([vt[HEAD_DIM * g:HEAD_DIM * (g + 1)], tail], axis=0).astype(BF16)


def _qk_prep(p, q_off, k_off, v_off, qn, kn, ones_bd, rope_tabs):
    t = p.shape[0]
    tm = _tile(t, 1024, LANES)
    use_rope, with_q = rope_tabs is not None, q_off is not None
    const = lambda shape: pl.BlockSpec(shape, lambda i: (0, 0))
    in_specs, args, out_specs, out_shape = [], [], [], []
    if with_q:
        in_specs.append(_slab(tm, BRANCH_W, q_off))
        args.append(p)
    in_specs += [_slab(tm, KV_W, k_off), _slab(tm, KV_W, v_off)]
    args += [p, p]
    if with_q:
        in_specs.append(const((1, BRANCH_W)))
        args.append(qn)
        out_specs.append(pl.BlockSpec((tm, BRANCH_W), lambda i: (i, 0)))
        out_shape.append(jax.ShapeDtypeStruct((t, BRANCH_W), BF16))
    in_specs += [const((1, KV_W)), const((BRANCH_W, BRANCH_W))]
    args += [kn, ones_bd]
    if use_rope:
        in_specs += [pl.BlockSpec((tm, LANES), lambda i: (i, 0))] * 2
        args += list(rope_tabs)
    out_specs += [pl.BlockSpec((tm, KV_W), lambda i: (i, 0)),
                  pl.BlockSpec((1, GQA_KV_HEADS, VT_ROWS, tm), lambda i: (i, 0, 0, 0))]
    out_shape += [jax.ShapeDtypeStruct((t, KV_W), BF16),
                  jax.ShapeDtypeStruct((t // tm, GQA_KV_HEADS, VT_ROWS, tm), BF16)]
    res = pl.pallas_call(
        functools.partial(_qk_prep_body, use_rope=use_rope, with_q=with_q),
        grid=(t // tm,),
        in_specs=in_specs,
        out_specs=out_specs,
        out_shape=out_shape,
        compiler_params=_params(dimension_semantics=("arbitrary",)),
        name="qk_prep",
    )(*args)
    return res if with_q else (None, res[0], res[1])


def _rope_tables(seq):
    n_rows = seq // GRID_W
    half = HEAD_DIM // 4
    freqs = ROPE_THETA ** (-jnp.arange(half, dtype=F32) / half)
    ang_r = jnp.arange(n_rows).astype(F32)[:, None] * freqs[None, :]
    ang_c = jnp.arange(GRID_W).astype(F32)[:, None] * freqs[None, :]
    per_row = lambda a: jnp.broadcast_to(a[:, None, :], (n_rows, GRID_W, half)).reshape(seq, half)
    per_col = lambda a: jnp.broadcast_to(a[None, :, :], (n_rows, GRID_W, half)).reshape(seq, half)
    cos_r, sin_r, cos_c, sin_c = (per_row(jnp.cos(ang_r)), per_row(jnp.sin(ang_r)),
                                  per_col(jnp.cos(ang_c)), per_col(jnp.sin(ang_c)))
    cos = jnp.concatenate([cos_r, cos_r, cos_c, cos_c], axis=1)
    sin = jnp.concatenate([-sin_r, sin_r, -sin_c, sin_c], axis=1)
    reps = LANES // HEAD_DIM
    return jnp.concatenate([cos] * reps, axis=1), jnp.concatenate([sin] * reps, axis=1)


def _nt_dot(a, b):
    return lax.dot_general(a, b, (((1,), (1,)), ((), ())), preferred_element_type=F32)


def _pair_queries(qp, low, shared_kv_lanes):
    zero = jnp.zeros_like(qp)
    if shared_kv_lanes is None:
        return jnp.where(low, qp, zero), jnp.where(low, zero, qp)
    qr = pltpu.roll(qp, HEAD_DIM, 1)
    if shared_kv_lanes == 0:
        return jnp.where(low, qp, zero), jnp.where(low, qr, zero)
    return jnp.where(low, zero, qr), jnp.where(low, zero, qp)


def _values_with_ones(vc, half):
    lane = lax.broadcasted_iota(jnp.int32, vc.shape, 1)
    keep = (lane < HEAD_DIM) if half == 0 else (lane >= HEAD_DIM)
    ones_lane = HEAD_DIM * (1 - half)
    fill = jnp.where(lane == ones_lane, 1.0, 0.0)
    return jnp.where(keep, vc.astype(F32), fill).astype(vc.dtype)


def _dense_attn_body(*refs, n_src, chunks, kv_grouped, q_scale, tq):
    n_pair = N_HEADS // 2
    q_ref, gate_ref = refs[0], refs[1]
    srcs = [(refs[2 + 2 * i], refs[3 + 2 * i]) for i in range(n_src)]
    o_ref = refs[2 + 2 * n_src]
    scratch = refs[3 + 2 * n_src:]
    m_s, acc_s = scratch[:n_pair], scratch[n_pair:]
    low_q = lax.broadcasted_iota(jnp.int32, (tq, LANES), 1) < HEAD_DIM

    halves = [((2 * p) // (N_HEADS // GQA_KV_HEADS),) * 2 if kv_grouped else (0, 1) for p in range(n_pair)]
    lhs = []
    for p in range(n_pair):
        qp = q_ref[:, LANES * p:LANES * (p + 1)].astype(F32) * q_scale
        qa, qb = _pair_queries(qp, low_q, halves[p][0] if kv_grouped else None)
        lhs.append(jnp.concatenate([qa, qb], axis=0).astype(BF16))
        m_s[p][...] = jnp.full(m_s[p].shape, NEG_BIG, F32)
        acc_s[p][...] = jnp.zeros(acc_s[p].shape, F32)

    def step(p, kc, va, vb):
        s = _nt_dot(lhs[p], kc)
        m_old = m_s[p][...]
        m_new = jnp.maximum(m_old, jnp.max(s, axis=-1, keepdims=True))
        alpha = jnp.exp2(m_old - m_new)
        pr = jnp.exp2(s - m_new).astype(BF16)
        acc = acc_s[p]
        acc[:tq] = alpha[:tq] * acc[:tq] + jnp.dot(pr[:tq], va, preferred_element_type=F32)
        acc[tq:] = alpha[tq:] * acc[tq:] + jnp.dot(pr[tq:], vb, preferred_element_type=F32)
        m_s[p][...] = m_new

    def all_pairs(k_ref, v_ref, rows):
        if kv_grouped:
            kc, vc = k_ref[rows, :], v_ref[rows, :]
            vals = [_values_with_ones(vc, h) for h in range(GQA_KV_HEADS)]
        for p in range(n_pair):
            if kv_grouped:
                step(p, kc, vals[halves[p][0]], vals[halves[p][1]])
            else:
                cols = slice(LANES * p, LANES * (p + 1))
                vc = v_ref[rows, cols]
                step(p, k_ref[rows, cols], _values_with_ones(vc, 0), _values_with_ones(vc, 1))

    for (k_ref, v_ref), ck in zip(srcs, chunks):
        n_chunk = k_ref.shape[0] // ck
        if n_chunk == 1:
            all_pairs(k_ref, v_ref, slice(None))
        else:
            def loop(i, carry, k_ref=k_ref, v_ref=v_ref, ck=ck):
                all_pairs(k_ref, v_ref, pl.ds(pl.multiple_of(i * ck, ck), ck))
                return carry
            lax.fori_loop(0, n_chunk, loop, 0)

    for p in range(n_pair):
        cols = slice(LANES * p, LANES * (p + 1))
        heads = []
        for hd, half in enumerate(halves[p]):
            acc = acc_s[p][hd * tq:(hd + 1) * tq]
            ones_lane = HEAD_DIM * (1 - half)
            o = acc / acc[:, ones_lane:ones_lane + 1]
            heads.append(o if half == hd else pltpu.roll(o, HEAD_DIM, 1))
        o = jnp.where(low_q, heads[0], heads[1])
        o_ref[:, cols] = (o * _silu(gate_ref[:, cols].astype(F32))).astype(BF16)


def _key_chunk(tk):
    for mult in (2 * LANES, LANES, 16):
        try:
            return _tile(tk, 1024, mult)
        except ValueError:
            pass
    raise ValueError(tk)


def _dense_attn(q, q_off, gate, gate_off, kv_srcs, kv_grouped, q_scale):
    t = q.shape[0]
    tq = _tile(t, 256, 16)
    kw = KV_W if kv_grouped else BRANCH_W
    in_specs = [_slab(tq, BRANCH_W, q_off), _slab(tq, BRANCH_W, gate_off)]
    args = [q, gate]
    chunks = []
    for k_arr, k_off, v_arr, v_off in kv_srcs:
        tk = k_arr.shape[0]
        assert k_off % kw == 0 and v_off % kw == 0
        in_specs += [_resident((tk, kw), lambda i, _c=k_off // kw: (0, _c)),
                     _resident((tk, kw), lambda i, _c=v_off // kw: (0, _c))]
        args += [k_arr, v_arr]
        chunks.append(_key_chunk(tk))
    n_pair = N_HEADS // 2
    return pl.pallas_call(
        functools.partial(_dense_attn_body, n_src=len(kv_srcs), chunks=tuple(chunks),
                          kv_grouped=kv_grouped, q_scale=q_scale, tq=tq),
        grid=(t // tq,),
        in_specs=in_specs,
        out_specs=pl.BlockSpec((tq, BRANCH_W), lambda i: (i, 0)),
        out_shape=jax.ShapeDtypeStruct((t, BRANCH_W), BF16),
        scratch_shapes=[pltpu.VMEM((2 * tq, 1), F32)] * n_pair + [pltpu.VMEM((2 * tq, LANES), F32)] * n_pair,
        compiler_params=_params(dimension_semantics=("arbitrary",)),
        name="dense_attn",
    )(*args)


def _gqa_body(q_ref, qn_ref, gate_ref, k_ref, vt_ref, kx_ref, vtx_ref, o_ref, *scratch, tq, n_chunk):
    n_pair = N_HEADS // 2
    qt_s, qtn_s, m_s, acc_s, sx_s = (scratch[i * n_pair:(i + 1) * n_pair] for i in range(5))
    s_s = [scratch[5 * n_pair + 2 * p:5 * n_pair + 2 * p + 2] for p in range(n_pair)]
    low_q = lax.broadcasted_iota(jnp.int32, (tq, LANES), 1) < HEAD_DIM
    kv_of = [(2 * p) // (N_HEADS // GQA_KV_HEADS) for p in range(n_pair)]

    def scores(p, keys, s_ref):
        s_ref[...] = jnp.dot(keys, qt_s[p][...], preferred_element_type=F32)

    def consume(p, s_ref, vt):
        s = s_ref[...]
        m_old = m_s[p][...]
        m_new = jnp.maximum(m_old, jnp.max(s, axis=0, keepdims=True))
        alpha = jnp.exp2(m_old - m_new)
        pt = jnp.exp2(s - m_new).astype(BF16)
        for hd in range(2):
            cols = slice(hd * tq, (hd + 1) * tq)
            acc_s[p][hd] = alpha[:, cols] * acc_s[p][hd] + jnp.dot(vt, pt[:, cols], preferred_element_type=F32)
        m_s[p][...] = m_new

    def step(i, slot):
        for p in range(n_pair):
            scores(p, k_ref[i + 1], s_s[p][1 - slot])
            consume(p, s_s[p][slot], vt_ref[i, kv_of[p]])

    def transposed_queries(p, q_tile_ref):
        qp = q_tile_ref[:, LANES * p:LANES * (p + 1)].astype(F32)
        qa, qb = _pair_queries(qp, low_q, kv_of[p])
        return jnp.concatenate([qa, qb], axis=0).T.astype(BF16)

    is_first = pl.program_id(0) == 0

    @pl.when(is_first)
    def _():
        for p in range(n_pair):
            qt_s[p][...] = transposed_queries(p, q_ref)
            scores(p, k_ref[0], s_s[p][0])

    @pl.when(jnp.logical_not(is_first))
    def _():
        for p in range(n_pair):
            qt_s[p][...] = qtn_s[p][...]

    for p in range(n_pair):
        m_s[p][...] = jnp.full(m_s[p].shape, NEG_BIG, F32)
        acc_s[p][...] = jnp.zeros(acc_s[p].shape, F32)

    def two_steps(j, carry):
        step(2 * j, 0)
        step(2 * j + 1, 1)
        return carry

    n_step = n_chunk - 1
    lax.fori_loop(0, n_step // 2, two_steps, 0)
    last = n_chunk - 1
    if n_step % 2:
        step(last - 1, 0)
    def next_scores(p):
        s_s[p][0][...] = jnp.dot(k_ref[0], qtn_s[p][...], preferred_element_type=F32)

    for p in range(n_pair):
        qtn_s[p][...] = transposed_queries(p, qn_ref)
    for p in range(n_pair):
        scores(p, kx_ref[...], sx_s[p])
        if last % 2:
            next_scores(p)
        consume(p, s_s[p][last % 2], vt_ref[last, kv_of[p]])
    for p in range(n_pair):
        if not last % 2:
            next_scores(p)
        consume(p, sx_s[p], vtx_ref[0, kv_of[p]])
        acc = acc_s[p]
        ot = jnp.concatenate([acc[hd, :HEAD_DIM] / acc[hd, HEAD_DIM:HEAD_DIM + 1] for hd in range(2)], axis=0)
        cols = slice(LANES * p, LANES * (p + 1))
        o_ref[:, cols] = (ot.T * _silu(gate_ref[:, cols].astype(F32))).astype(BF16)


def _gqa_attn(q, gate, gate_off, k, vt, k_extra, vt_extra):
    t = q.shape[0]
    n_chunk, _, _, ck = vt.shape
    tx = k_extra.shape[0]
    assert vt_extra.shape[0] == 1 and k.shape[0] == n_chunk * ck
    tq = _tile(t, 256, LANES)
    n_pair = N_HEADS // 2
    scratch = ([pltpu.VMEM((LANES, 2 * tq), BF16)] * (2 * n_pair) + [pltpu.VMEM((1, 2 * tq), F32)] * n_pair
               + [pltpu.VMEM((2, VT_ROWS, tq), F32)] * n_pair + [pltpu.VMEM((tx, 2 * tq), F32)] * n_pair
               + [pltpu.VMEM((ck, 2 * tq), F32)] * (2 * n_pair))
    return pl.pallas_call(
        functools.partial(_gqa_body, tq=tq, n_chunk=n_chunk),
        grid=(t // tq,),
        in_specs=[pl.BlockSpec((tq, BRANCH_W), lambda i: (i, 0)),
                  pl.BlockSpec((tq, BRANCH_W), lambda i: (jnp.minimum(i + 1, t // tq - 1), 0)),
                  _slab(tq, BRANCH_W, gate_off),
                  _resident((n_chunk, ck, KV_W), lambda i: (0, 0, 0)),
                  _resident(vt.shape, lambda i: (0, 0, 0, 0)),
                  _resident((tx, KV_W), lambda i: (0, 0)),
                  _resident(vt_extra.shape, lambda i: (0, 0, 0, 0))],
        out_specs=pl.BlockSpec((tq, BRANCH_W), lambda i: (i, 0)),
        out_shape=jax.ShapeDtypeStruct((t, BRANCH_W), BF16),
        scratch_shapes=scratch,
        compiler_params=_params(dimension_semantics=("arbitrary",)),
        name="gqa_attn",
    )(q, q, gate, k.reshape(n_chunk, ck, KV_W), vt, k_extra, vt_extra)


def _na_bias_tiles(rpb):
    cq = np.arange(GRID_W)
    col_start = np.clip(cq - NA_WIN_COLS // 2, 0, GRID_W - NA_WIN_COLS)
    col_ok = (cq[None, :] >= col_start[:, None]) & (cq[None, :] < col_start[:, None] + NA_WIN_COLS)
    col_off = np.clip(cq[None, :] - cq[:, None], -(NA_WIN_COLS - 1), NA_WIN_COLS - 1) + NA_WIN_COLS - 1
    onehot = jnp.asarray(col_off[:, :, None] == np.arange(2 * NA_WIN_COLS - 1), F32)
    tiles = jnp.einsum("lhrm,qkm->lhrqk", rpb.astype(F32), onehot, precision=lax.Precision.HIGHEST)
    tiles = jnp.where(col_ok, tiles * LOG2_E, NEG_BIG)
    masked = jnp.full(tiles.shape[:2] + (1, GRID_W, GRID_W), NEG_BIG, F32)
    tiles = jnp.concatenate([tiles, masked], axis=2)
    return jnp.concatenate([tiles, tiles], axis=-1)


def _na_body(q_ref, gate_ref, k_ref, v_ref, kc_ref, vc_ref, tiles_ref, o_ref, *, n_rows):
    rb, band = NA_ROW_BLOCK, NA_BAND_ROWS
    tq, nk = rb * GRID_W, band * GRID_W
    b = pl.program_id(0)
    u0 = jnp.clip(b * rb - NA_WIN_ROWS // 2, 0, n_rows - band)
    rows = pl.ds(pl.multiple_of(u0 * GRID_W, GRID_W), nk)
    low_q = lax.broadcasted_iota(jnp.int32, (tq, LANES), 1) < HEAD_DIM
    low_t = lax.broadcasted_iota(jnp.int32, (GRID_W, LANES), 1) < GRID_W

    def bias_rows(h, i):
        r = b * rb + i
        row_start = jnp.clip(r - NA_WIN_ROWS // 2, 0, n_rows - NA_WIN_ROWS)
        pieces = []
        for j in range(band):
            key_row = u0 + j
            in_window = jnp.logical_and(key_row >= row_start, key_row < row_start + NA_WIN_ROWS)
            ro = jnp.where(in_window, key_row - r + NA_WIN_ROWS - 1, 2 * NA_WIN_ROWS - 1)
            pieces.append(tiles_ref[0, h, ro])
        return jnp.concatenate([jnp.where(low_t, pieces[j], pieces[j + 1]) for j in range(0, band, 2)], axis=1)

    for p in range(N_HEADS // 2):
        bias = jnp.concatenate([bias_rows(2 * p + hd, i) for hd in range(2) for i in range(rb)], axis=0)
        cols = slice(LANES * p, LANES * (p + 1))
        qp = q_ref[:, cols].astype(F32) * SCORE_SCALE_LOG2
        qa, qb = _pair_queries(qp, low_q, None)
        lhs = jnp.concatenate([qa, qb], axis=0).astype(BF16)
        kb, vb = k_ref[rows, cols], v_ref[rows, cols]
        kc, vc = kc_ref[:, cols], vc_ref[:, cols]
        s_band = _nt_dot(lhs, kb) + bias
        s_ctx = _nt_dot(lhs, kc)
        m = jnp.maximum(jnp.max(s_band, axis=-1, keepdims=True), jnp.max(s_ctx, axis=-1, keepdims=True))
        pb = jnp.exp2(s_band - m).astype(BF16)
        pc = jnp.exp2(s_ctx - m).astype(BF16)
        heads = []
        for hd in range(2):
            r = slice(hd * tq, (hd + 1) * tq)
            acc = (jnp.dot(pb[r], _values_with_ones(vb, hd), preferred_element_type=F32)
                   + jnp.dot(pc[r], _values_with_ones(vc, hd), preferred_element_type=F32))
            ones_lane = HEAD_DIM * (1 - hd)
            heads.append(acc / acc[:, ones_lane:ones_lane + 1])
        o = jnp.where(low_q, heads[0], heads[1])
        o_ref[:, cols] = (o * _silu(gate_ref[:, cols].astype(F32))).astype(BF16)


def _na_attn(p, off, pc, bias, layer):
    s = p.shape[0]
    cn = pc.shape[0]
    n_rows = s // GRID_W
    tq = NA_ROW_BLOCK * GRID_W
    assert n_rows % NA_ROW_BLOCK == 0 and n_rows >= NA_BAND_ROWS >= NA_ROW_BLOCK + NA_WIN_ROWS - 1
    assert NA_BAND_ROWS % 2 == 0
    nb = n_rows // NA_ROW_BLOCK
    w = BRANCH_W
    return pl.pallas_call(
        functools.partial(_na_body, n_rows=n_rows),
        grid=(nb,),
        in_specs=[_slab(tq, w, off["a_q"]), _slab(tq, w, off["a_gate"]),
                  _resident((s, w), lambda i, _c=off["a_k"] // w: (0, _c)),
                  _resident((s, w), lambda i, _c=off["a_v"] // w: (0, _c)),
                  _resident((cn, w), lambda i, _c=off["a_k"] // w: (0, _c)),
                  _resident((cn, w), lambda i, _c=off["a_v"] // w: (0, _c)),
                  _resident((1,) + bias.shape[1:], lambda i: (layer, 0, 0, 0, 0))],
        out_specs=pl.BlockSpec((tq, w), lambda i: (i, 0)),
        out_shape=jax.ShapeDtypeStruct((s, w), BF16),
        compiler_params=_params(dimension_semantics=("arbitrary",)),
        name="na_attn",
    )(p, p, p, p, pc, pc, bias)


def _halo_specs(t, tm, width, off):
    assert off % LANES == 0 and tm % HALO == 0
    block = lambda rows: (pl.Element(rows), pl.Element(width))
    per, last = tm // HALO, t // HALO - 1
    return [pl.BlockSpec(block(tm), lambda i: (i * tm, off)),
            pl.BlockSpec(block(HALO), lambda i: (jnp.maximum(i * per - 1, 0) * HALO, off)),
            pl.BlockSpec(block(HALO), lambda i: (jnp.minimum((i + 1) * per, last) * HALO, off))]


def _fill_padded(pad_ref, cur, prev, nxt, tm):
    i, n = pl.program_id(0), pl.num_programs(0)
    pad_ref[0:HALO] = jnp.where(i > 0, prev, jnp.zeros_like(prev))
    pad_ref[HALO:HALO + tm] = cur
    pad_ref[HALO + tm:] = jnp.where(i < n - 1, nxt, jnp.zeros_like(nxt))


def _pool_body(u_ref, up_ref, un_ref, gate_ref, w_ref, sc_ref, o_ref, pad_ref, *, tm, seq):
    _fill_padded(pad_ref, u_ref[...].astype(F32), up_ref[...].astype(F32), un_ref[...].astype(F32), tm)
    t = pl.program_id(0) * tm + lax.broadcasted_iota(jnp.int32, (tm, POOL_GROUP), 0)
    for gi, ksz in enumerate(POOL_SIZES):
        cols = slice(POOL_GROUP * gi, POOL_GROUP * (gi + 1))
        back = ksz // 2
        tot = pad_ref[HALO - back:HALO - back + tm, cols]
        for d in range(1 - back, ksz - back):
            tot = tot + pad_ref[HALO + d:HALO + d + tm, cols]
        lo = jnp.maximum(t - back, 0)
        hi = jnp.minimum(t + (ksz - 1 - back), seq - 1)
        mean = tot / (hi - lo + 1).astype(F32)
        dlt = (mean - pad_ref[HALO:HALO + tm, cols]).astype(BF16)
        y = jnp.dot(dlt, w_ref[gi], preferred_element_type=F32) * sc_ref[:, cols]
        o_ref[:, cols] = (y * _silu(gate_ref[:, cols].astype(F32))).astype(BF16)


def _pool(p, off, w_pool, pool_scale):
    t = p.shape[0]
    tm = _tile(t, 512, HALO)
    w = BRANCH_W
    return pl.pallas_call(
        functools.partial(_pool_body, tm=tm, seq=t),
        grid=(t // tm,),
        in_specs=_halo_specs(t, tm, w, off["b_in"]) + [
            _slab(tm, w, off["b_gate"]),
            pl.BlockSpec(w_pool.shape, lambda i: (0, 0, 0)),
            pl.BlockSpec((1, w), lambda i: (0, 0))],
        out_specs=pl.BlockSpec((tm, w), lambda i: (i, 0)),
        out_shape=jax.ShapeDtypeStruct((t, w), BF16),
        scratch_shapes=[pltpu.VMEM((tm + 2 * HALO, w), F32)],
        compiler_params=_params(dimension_semantics=("arbitrary",)),
        name="pool",
    )(p, p, p, p, w_pool, pool_scale)


def _glu(x):
    x = x.astype(F32)
    return x[:, :BRANCH_W] * jax.nn.sigmoid(x[:, BRANCH_W:])


def _conv_body(x_ref, xp_ref, xn_ref, gate_ref, cw_ref, cb_ref, lg_ref, lb_ref, pw_ref, o_ref, pad_ref, sh_ref,
               *, tm):
    _fill_padded(pad_ref, _glu(x_ref[...]), _glu(xp_ref[...]), _glu(xn_ref[...]), tm)
    span = sh_ref.shape[1]
    for b in range(1, SUBLANES):
        sh_ref[b - 1] = pad_ref[b:b + span, :]
    reach = CONV_WIDTH // 2
    y = jnp.zeros((tm, BRANCH_W), F32) + cb_ref[...]
    for j in range(CONV_WIDTH):
        a, b = divmod(HALO - reach + j, SUBLANES)
        rows = slice(SUBLANES * a, SUBLANES * a + tm)
        y = y + (pad_ref[rows, :] if b == 0 else sh_ref[b - 1, rows, :]) * cw_ref[j:j + 1, :]
    mu = jnp.mean(y, axis=-1, keepdims=True)
    yc = y - mu
    var = jnp.mean(yc * yc, axis=-1, keepdims=True)
    z = _silu(yc * lax.rsqrt(var + EPS) * lg_ref[...] + lb_ref[...]).astype(BF16)
    out = jnp.dot(z, pw_ref[...], preferred_element_type=F32)
    o_ref[...] = (out * _silu(gate_ref[...].astype(F32))).astype(BF16)


def _conv(p, off, conv_w, conv_b, ln_g, ln_b, w_pw):
    t = p.shape[0]
    tm = _tile(t, 512, HALO)
    w = BRANCH_W
    vec = pl.BlockSpec((1, w), lambda i: (0, 0))
    return pl.pallas_call(
        functools.partial(_conv_body, tm=tm),
        grid=(t // tm,),
        in_specs=_halo_specs(t, tm, 2 * w, off["d_glu"]) + [
            _slab(tm, w, off["d_gate"]),
            pl.BlockSpec(conv_w.shape, lambda i: (0, 0)), vec, vec, vec,
            pl.BlockSpec((w, w), lambda i: (0, 0))],
        out_specs=pl.BlockSpec((tm, w), lambda i: (i, 0)),
        out_shape=jax.ShapeDtypeStruct((t, w), BF16),
        scratch_shapes=[pltpu.VMEM((tm + 2 * HALO, w), F32),
                        pltpu.VMEM((SUBLANES - 1, tm + 2 * HALO - SUBLANES, w), F32)],
        compiler_params=_params(dimension_semantics=("arbitrary",)),
        name="conv",
    )(p, p, p, p, conv_w, conv_b, ln_g, ln_b, w_pw)


def _merge_body(m0, m1, m2, m3, o0, o1, o2, o3, wb_ref, y_ref):
    y = None
    for bi, (m_ref, o_ref) in enumerate(((m0, o0), (m1, o1), (m2, o2), (m3, o3))):
        proj = jnp.dot(o_ref[...], wb_ref[bi], preferred_element_type=F32)
        term = (1.0 + jnp.tanh(0.5 * m_ref[...].astype(F32))) * proj
        y = term if y is None else y + term
    y_ref[...] = (0.5 * y).astype(BF16)


def _merge(p, merge_off, outs, w_branch):
    t = p.shape[0]
    d = w_branch.shape[-1]
    tm = _tile(t, 256, 16)
    return pl.pallas_call(
        _merge_body,
        grid=(t // tm,),
        in_specs=[_slab(tm, d, merge_off + bi * d) for bi in range(N_BRANCH)]
        + [pl.BlockSpec((tm, BRANCH_W), lambda i: (i, 0))] * N_BRANCH
        + [_resident(w_branch.shape, lambda i: (0, 0, 0))],
        out_specs=pl.BlockSpec((tm, d), lambda i: (i, 0)),
        out_shape=jax.ShapeDtypeStruct((t, d), BF16),
        compiler_params=_params(dimension_semantics=("arbitrary",)),
        name="merge",
    )(p, p, p, p, *outs, w_branch)


def _out_body(y_ref, w_ref, x_ref, g_ref, gate_ref, *rest):
    y = jnp.dot(y_ref[...], w_ref[...], preferred_element_type=F32)
    yn = y * lax.rsqrt(jnp.mean(y * y, axis=-1, keepdims=True) + EPS) * g_ref[...]
    x_next = x_ref[...] + gate_ref[...] * yn
    if len(rest) == 1:
        rest[0][...] = x_next
    else:
        gn_ref, sh_ref, sc_ref, o_ref, h_ref = rest
        o_ref[...] = x_next
        h_ref[...] = _modulated(x_next, gn_ref[...], sh_ref[...], sc_ref[...])


def _out_proj(y, w_out, x, g_post, gate, next_mod=None):
    t, d = x.shape
    tm = _tile(t, 256, 16)
    vec = pl.BlockSpec((1, d), lambda i: (0, 0))
    row = pl.BlockSpec((tm, d), lambda i: (i, 0))
    n_next = 0 if next_mod is None else 1
    out = pl.pallas_call(
        _out_body,
        grid=(t // tm,),
        in_specs=[row, _resident((d, d), lambda i: (0, 0)), row, vec, vec] + [vec] * (3 * n_next),
        out_specs=[row] * (1 + n_next),
        out_shape=[jax.ShapeDtypeStruct((t, d), F32)] + [jax.ShapeDtypeStruct((t, d), BF16)] * n_next,
        compiler_params=_params(dimension_semantics=("arbitrary",)),
        name="out_proj",
    )(y, w_out, x, g_post, gate, *(next_mod or ()))
    return out if n_next else (out[0], None)


def kernel(x, c, ctx, c_ctx, w_ada, b_ada, g_pre, g_post, w_in, na_rpb, pool_w, pool_scale,
           q_norm, k_norm, conv_w, conv_b, conv_ln_g, conv_ln_b, conv_pw, w_branch, w_out):
    batch, seq, d = x.shape
    assert batch == 1 and seq % GRID_W == 0
    cn = ctx.shape[1]
    depth = w_ada.shape[0]
    off = _layout(d)

    xs, cs = x[0], ctx[0]
    cvec = jnp.zeros((8, d), F32).at[0].set(c[0]).at[1].set(c_ctx)
    ada = _ada(cvec, w_ada, b_ada)

    ones_bd = jnp.asarray(np.kron(np.eye(N_HEADS), np.ones((HEAD_DIM, HEAD_DIM))), BF16)
    rope_tabs = _rope_tables(seq)
    na_bias = _na_bias_tiles(na_rpb)
    row = lambda v: v.reshape(1, -1)
    qn_all = jnp.tile(q_norm, (1, N_HEADS))
    kn_all = jnp.tile(k_norm, (1, GQA_KV_HEADS))

    mod = lambda l, r: tuple(ada[l, r:r + 1, k * d:(k + 1) * d] for k in range(3))
    h = _modulate(xs, row(g_pre[0]), *mod(0, 0)[:2])
    hc = _modulate(cs, row(g_pre[0]), *mod(0, 1)[:2])
    for l in range(depth):
        last = l == depth - 1
        wb_l = w_branch[l].astype(BF16)
        wo_l = w_out[l].astype(BF16)
        pw_l = pool_w[l].astype(BF16)
        cpw_l = conv_pw[l].astype(BF16)
        gate, gate_c = mod(l, 0)[2], mod(l, 1)[2]
        next_mod = lambda r: None if last else (row(g_pre[l + 1]),) + mod(l + 1, r)[:2]
        qn, kn = row(qn_all[l]), row(kn_all[l])

        pc = _in_proj(hc, w_in, l, KV_COLS if last else None)
        qc_ctx, kq_ctx, vt_ctx = _qk_prep(pc, None if last else off["c_q"], off["c_k"], off["c_v"],
                                          qn, kn, ones_bd, None)

        p = _in_proj(h, w_in, l)
        qc, kc, vt = _qk_prep(p, off["c_q"], off["c_k"], off["c_v"], qn, kn, ones_bd, rope_tabs)
        o_c = _gqa_attn(qc, p, off["c_gate"], kc, vt, kq_ctx, vt_ctx)
        o_a = _na_attn(p, off, pc, na_bias, l)
        o_b = _pool(p, off, pw_l, row(pool_scale[l]))
        o_d = _conv(p, off, conv_w[l], row(conv_b[l]), row(conv_ln_g[l]), row(conv_ln_b[l]), cpw_l)
        y = _merge(p, off["merge"], (o_a, o_b, o_c, o_d), wb_l)
        xs, h = _out_proj(y, wo_l, xs, row(g_post[l]), gate, next_mod(0))

        if not last:
            o_a_c = _dense_attn(pc, off["a_q"], pc, off["a_gate"],
                                [(pc, off["a_k"], pc, off["a_v"])], False, SCORE_SCALE_LOG2)
            o_c_c = _dense_attn(qc_ctx, 0, pc, off["c_gate"], [(kq_ctx, 0, pc, off["c_v"])], True, 1.0)
            o_b_c = _pool(pc, off, pw_l, row(pool_scale[l]))
            o_d_c = _conv(pc, off, conv_w[l], row(conv_b[l]), row(conv_ln_g[l]), row(conv_ln_b[l]), cpw_l)
            y_c = _merge(pc, off["merge"], (o_a_c, o_b_c, o_c_c, o_d_c), wb_l)
            cs, hc = _out_proj(y_c, wo_l, cs, row(g_post[l]), gate_c, next_mod(1))
    return xs[None]
```

```python
import functools

import numpy as np
import jax
import jax.numpy as jnp
from jax import lax
from jax.experimental import pallas as pl
from jax.experimental.pallas import tpu as pltpu

F32 = jnp.float32
BF16 = jnp.bfloat16

GRID_W = 64
HEAD_DIM = 64
BRANCH_W = 512
N_BRANCH = 4
N_HEADS = BRANCH_W // HEAD_DIM
GQA_KV_HEADS = 2
KV_W = GQA_KV_HEADS * HEAD_DIM
NA_WIN_ROWS = 8
NA_WIN_COLS = 16
POOL_SIZES = (2, 4, 8, 16)
POOL_GROUP = BRANCH_W // len(POOL_SIZES)
ROPE_THETA = 10000.0
CONV_WIDTH = 31
EPS = 1e-6

LANES = 128
SUBLANES = 8
HALO = 16
NA_ROW_BLOCK = 4
NA_BAND_ROWS = 12
NEG_BIG = -1e30
LOG2_E = 1.4426950408889634
SCORE_SCALE_LOG2 = HEAD_DIM ** -0.5 * LOG2_E
V7X_VMEM_LIMIT = 52 * 1024 * 1024

PARTS = (("a_k", BRANCH_W), ("a_v", BRANCH_W), ("c_k", KV_W), ("c_v", KV_W), ("a_q", BRANCH_W), ("c_q", BRANCH_W),
         ("a_gate", BRANCH_W), ("b_in", BRANCH_W), ("b_gate", BRANCH_W), ("c_gate", BRANCH_W),
         ("d_glu", 2 * BRANCH_W), ("d_gate", BRANCH_W), ("merge", None))
KV_COLS = 2 * BRANCH_W + 2 * KV_W


def _layout(d_model):
    off, o = {}, 0
    for n, w in PARTS:
        off[n] = o
        o += N_BRANCH * d_model if w is None else w
    return off


def _params(**kw):
    return pltpu.CompilerParams(vmem_limit_bytes=V7X_VMEM_LIMIT, **kw)


def _tile(n, pref, mult):
    if n <= pref:
        return n
    t = (pref // mult) * mult
    while t >= mult:
        if n % t == 0:
            return t
        t -= mult
    raise ValueError(f"no tile for {n}")


def _resident(block_shape, index_map):
    return pl.BlockSpec(block_shape, index_map, pipeline_mode=pl.Buffered(1))


def _slab(tm, width, off):
    assert off % LANES == 0
    return pl.BlockSpec((pl.Element(tm), pl.Element(width)), lambda i: (i * tm, off))


def _silu(x):
    return x * jax.nn.sigmoid(x)


def _ada_body(cs_ref, w_ref, b_ref, o_ref):
    s = _silu(cs_ref[...]).astype(BF16)
    o_ref[0] = jnp.dot(s, w_ref[0].astype(BF16), preferred_element_type=F32) + b_ref[0]


def _ada(cs, w_ada, b_ada):
    depth, d, n = w_ada.shape
    tn = _tile(n, 1024, LANES)
    return pl.pallas_call(
        _ada_body,
        grid=(depth, n // tn),
        in_specs=[pl.BlockSpec((8, d), lambda l, j: (0, 0)),
                  pl.BlockSpec((1, d, tn), lambda l, j: (l, 0, j)),
                  pl.BlockSpec((1, 1, tn), lambda l, j: (l, 0, j))],
        out_specs=pl.BlockSpec((1, 8, tn), lambda l, j: (l, 0, j)),
        out_shape=jax.ShapeDtypeStruct((depth, 8, n), F32),
        compiler_params=_params(dimension_semantics=("arbitrary", "arbitrary")),
        name="ada",
    )(cs, w_ada, b_ada.reshape(depth, 1, n))


def _modulated(x, g, shift, scale):
    y = x * lax.rsqrt(jnp.mean(x * x, axis=-1, keepdims=True) + EPS) * g
    return (y * (1.0 + scale) + shift).astype(BF16)


def _modulate_body(x_ref, g_ref, sh_ref, sc_ref, o_ref):
    o_ref[...] = _modulated(x_ref[...], g_ref[...], sh_ref[...], sc_ref[...])


def _modulate(x, g, shift, scale):
    t, d = x.shape
    tm = _tile(t, 512, 16)
    vec = pl.BlockSpec((1, d), lambda i: (0, 0))
    return pl.pallas_call(
        _modulate_body,
        grid=(t // tm,),
        in_specs=[pl.BlockSpec((tm, d), lambda i: (i, 0)), vec, vec, vec],
        out_specs=pl.BlockSpec((tm, d), lambda i: (i, 0)),
        out_shape=jax.ShapeDtypeStruct((t, d), BF16),
        compiler_params=_params(dimension_semantics=("arbitrary",)),
        name="modulate",
    )(x, g, shift, scale)


def _in_proj_body(h_ref, hc_ref, w_ref, o_ref, oc_ref, wb_ref, *, n_lat, n_ctx_tiles):
    j, i = pl.program_id(0), pl.program_id(1)

    @pl.when(i == 0)
    def _():
        wb_ref[...] = w_ref[0].astype(BF16)

    @pl.when(i < n_lat)
    def _():
        o_ref[...] = jnp.dot(h_ref[...], wb_ref[...], preferred_element_type=F32).astype(o_ref.dtype)

    @pl.when(jnp.logical_and(i == n_lat, j < n_ctx_tiles))
    def _():
        oc_ref[...] = jnp.dot(hc_ref[...], wb_ref[...], preferred_element_type=F32).astype(oc_ref.dtype)


def _col_tile(n):
    for mult in (2 * LANES, LANES):
        try:
            return _tile(n, 1536, mult)
        except ValueError:
            pass
    raise ValueError(n)


def _in_proj(h, hc, w_in, layer, ctx_cols=None):
    t, d = h.shape
    tc = hc.shape[0]
    n = w_in.shape[2]
    tm = _tile(t, 1024, 16)
    tn = _col_tile(n)
    n_lat = t // tm
    n_ctx_tiles = n // tn if ctx_cols is None else pl.cdiv(ctx_cols, tn)
    lat_tile = lambda i: jnp.minimum(i, n_lat - 1)
    return pl.pallas_call(
        functools.partial(_in_proj_body, n_lat=n_lat, n_ctx_tiles=n_ctx_tiles),
        grid=(n // tn, n_lat + 1),
        in_specs=[pl.BlockSpec((tm, d), lambda j, i: (lat_tile(i), 0)),
                  _resident((tc, d), lambda j, i: (0, 0)),
                  pl.BlockSpec((1, d, tn), lambda j, i: (layer, 0, j))],
        out_specs=[pl.BlockSpec((tm, tn), lambda j, i: (lat_tile(i), j)),
                   pl.BlockSpec((tc, tn), lambda j, i: (0, jnp.minimum(j, n_ctx_tiles - 1)))],
        out_shape=[jax.ShapeDtypeStruct((t, n), BF16), jax.ShapeDtypeStruct((tc, n_ctx_tiles * tn), BF16)],
        scratch_shapes=[pltpu.VMEM((d, tn), BF16)],
        compiler_params=_params(dimension_semantics=("arbitrary", "arbitrary")),
        name="in_proj",
    )(h, hc, w_in)


def _head_meansq(x, ones_bd):
    ss = x * x
    hi = ss.astype(BF16)
    lo = (ss - hi.astype(F32)).astype(BF16)
    tot = (jnp.dot(hi, ones_bd, preferred_element_type=F32)
           + jnp.dot(lo, ones_bd, preferred_element_type=F32))
    return tot * (1.0 / HEAD_DIM)


def _rope(y, cos, sin_signed):
    w = y.shape[-1]
    lane = lax.broadcasted_iota(jnp.int32, y.shape, 1)
    nxt = pltpu.roll(y, w - 16, 1)
    prv = pltpu.roll(y, 16, 1)
    return y * cos + jnp.where((lane % 32) < 16, nxt, prv) * sin_signed


def _norm_rope(x, w, bd, rope):
    x = x * lax.rsqrt(_head_meansq(x, bd) + EPS) * w
    if rope is not None:
        reps = x.shape[-1] // LANES
        x = _rope(x, jnp.concatenate([rope[0]] * reps, axis=1), jnp.concatenate([rope[1]] * reps, axis=1))
    return x


VT_ROWS = HEAD_DIM + 16


def _qk_prep_body(*refs, use_rope, with_q):
    refs = list(refs)
    q_ref = refs.pop(0) if with_q else None
    k_ref, v_ref = refs.pop(0), refs.pop(0)
    qn_ref = refs.pop(0) if with_q else None
    kn_ref, bd_ref = refs.pop(0), refs.pop(0)
    rope = (refs.pop(0)[...], refs.pop(0)[...]) if use_rope else None
    bd = bd_ref[...]
    if with_q:
        q = _norm_rope(q_ref[...].astype(F32), qn_ref[...], bd, rope)
        refs.pop(0)[...] = (q * SCORE_SCALE_LOG2).astype(BF16)
    k = _norm_rope(k_ref[...].astype(F32), kn_ref[...], bd[:KV_W, :KV_W], rope)
    refs.pop(0)[...] = k.astype(BF16)
    vt_ref = refs.pop(0)
    vt = v_ref[...].astype(F32).T
    row = lax.broadcasted_iota(jnp.int32, (VT_ROWS - HEAD_DIM, vt.shape[1]), 0)
    tail = jnp.where(row == 0, 1.0, 0.0)
    for g in range(GQA_KV_HEADS):
        vt_ref[0, g] = jnp.concatenate([vt[HEAD_DIM * g:HEAD_DIM * (g + 1)], tail], axis=0).astype(BF16)


def _qk_prep(p, q_off, k_off, v_off, qn, kn, ones_bd, rope_tabs):
    t = p.shape[0]
    tm = _tile(t, 1024, LANES)
    use_rope, with_q = rope_tabs is not None, q_off is not None
    const = lambda shape: pl.BlockSpec(shape, lambda i: (0, 0))
    in_specs, args, out_specs, out_shape = [], [], [], []
    if with_q:
        in_specs.append(_slab(tm, BRANCH_W, q_off))
        args.append(p)
    in_specs += [_slab(tm, KV_W, k_off), _slab(tm, KV_W, v_off)]
    args += [p, p]
    if with_q:
        in_specs.append(const((1, BRANCH_W)))
        args.append(qn)
        out_specs.append(pl.BlockSpec((tm, BRANCH_W), lambda i: (i, 0)))
        out_shape.append(jax.ShapeDtypeStruct((t, BRANCH_W), BF16))
    in_specs += [const((1, KV_W)), const((BRANCH_W, BRANCH_W))]
    args += [kn, ones_bd]
    if use_rope:
        in_specs += [pl.BlockSpec((tm, LANES), lambda i: (i, 0))] * 2
        args += list(rope_tabs)
    out_specs += [pl.BlockSpec((tm, KV_W), lambda i: (i, 0)),
                  pl.BlockSpec((1, GQA_KV_HEADS, VT_ROWS, tm), lambda i: (i, 0, 0, 0))]
    out_shape += [jax.ShapeDtypeStruct((t, KV_W), BF16),
                  jax.ShapeDtypeStruct((t // tm, GQA_KV_HEADS, VT_ROWS, tm), BF16)]
    res = pl.pallas_call(
        functools.partial(_qk_prep_body, use_rope=use_rope, with_q=with_q),
        grid=(t // tm,),
        in_specs=in_specs,
        out_specs=out_specs,
        out_shape=out_shape,
        compiler_params=_params(dimension_semantics=("arbitrary",)),
        name="qk_prep",
    )(*args)
    return res if with_q else (None, res[0], res[1])


def _rope_tables(seq):
    half = HEAD_DIM // 4
    t = lax.broadcasted_iota(jnp.int32, (seq, LANES), 0)
    lane = lax.broadcasted_iota(jnp.int32, (seq, LANES), 1)
    pos = jnp.where((lane % HEAD_DIM) < 2 * half, t // GRID_W, t % GRID_W).astype(F32)
    freqs = ROPE_THETA ** (-(lane % half).astype(F32) / half)
    ang = pos * freqs
    sign = jnp.where((lane % (2 * half)) < half, -1.0, 1.0)
    return jnp.cos(ang), jnp.sin(ang) * sign


def _nt_dot(a, b):
    return lax.dot_general(a, b, (((1,), (1,)), ((), ())), preferred_element_type=F32)


def _pair_queries(qp, low, shared_kv_lanes):
    zero = jnp.zeros_like(qp)
    if shared_kv_lanes is None:
        return jnp.where(low, qp, zero), jnp.where(low, zero, qp)
    qr = pltpu.roll(qp, HEAD_DIM, 1)
    if shared_kv_lanes == 0:
        return jnp.where(low, qp, zero), jnp.where(low, qr, zero)
    return jnp.where(low, zero, qr), jnp.where(low, zero, qp)


def _values_with_ones(vc, half):
    lane = lax.broadcasted_iota(jnp.int32, vc.shape, 1)
    keep = (lane < HEAD_DIM) if half == 0 else (lane >= HEAD_DIM)
    ones_lane = HEAD_DIM * (1 - half)
    fill = jnp.where(lane == ones_lane, 1.0, 0.0)
    return jnp.where(keep, vc.astype(F32), fill).astype(vc.dtype)


def _dense_attn_body(*refs, n_src, chunks, kv_grouped, q_scale, tq):
    n_pair = N_HEADS // 2
    q_ref, gate_ref = refs[0], refs[1]
    srcs = [(refs[2 + 2 * i], refs[3 + 2 * i]) for i in range(n_src)]
    o_ref = refs[2 + 2 * n_src]
    scratch = refs[3 + 2 * n_src:]
    m_s, acc_s = scratch[:n_pair], scratch[n_pair:]
    low_q = lax.broadcasted_iota(jnp.int32, (tq, LANES), 1) < HEAD_DIM

    halves = [((2 * p) // (N_HEADS // GQA_KV_HEADS),) * 2 if kv_grouped else (0, 1) for p in range(n_pair)]
    lhs = []
    for p in range(n_pair):
        qp = q_ref[:, LANES * p:LANES * (p + 1)].astype(F32) * q_scale
        qa, qb = _pair_queries(qp, low_q, halves[p][0] if kv_grouped else None)
        lhs.append(jnp.concatenate([qa, qb], axis=0).astype(BF16))
        m_s[p][...] = jnp.full(m_s[p].shape, NEG_BIG, F32)
        acc_s[p][...] = jnp.zeros(acc_s[p].shape, F32)

    def step(p, kc, va, vb):
        s = _nt_dot(lhs[p], kc)
        m_old = m_s[p][...]
        m_new = jnp.maximum(m_old, jnp.max(s, axis=-1, keepdims=True))
        alpha = jnp.exp2(m_old - m_new)
        pr = jnp.exp2(s - m_new).astype(BF16)
        acc = acc_s[p]
        acc[:tq] = alpha[:tq] * acc[:tq] + jnp.dot(pr[:tq], va, preferred_element_type=F32)
        acc[tq:] = alpha[tq:] * acc[tq:] + jnp.dot(pr[tq:], vb, preferred_element_type=F32)
        m_s[p][...] = m_new

    def all_pairs(k_ref, v_ref, rows):
        if kv_grouped:
            kc, vc = k_ref[rows, :], v_ref[rows, :]
            vals = [_values_with_ones(vc, h) for h in range(GQA_KV_HEADS)]
        for p in range(n_pair):
            if kv_grouped:
                step(p, kc, vals[halves[p][0]], vals[halves[p][1]])
            else:
                cols = slice(LANES * p, LANES * (p + 1))
                vc = v_ref[rows, cols]
                step(p, k_ref[rows, cols], _values_with_ones(vc, 0), _values_with_ones(vc, 1))

    for (k_ref, v_ref), ck in zip(srcs, chunks):
        n_chunk = k_ref.shape[0] // ck
        if n_chunk == 1:
            all_pairs(k_ref, v_ref, slice(None))
        else:
            def loop(i, carry, k_ref=k_ref, v_ref=v_ref, ck=ck):
                all_pairs(k_ref, v_ref, pl.ds(pl.multiple_of(i * ck, ck), ck))
                return carry
            lax.fori_loop(0, n_chunk, loop, 0)

    for p in range(n_pair):
        cols = slice(LANES * p, LANES * (p + 1))
        heads = []
        for hd, half in enumerate(halves[p]):
            acc = acc_s[p][hd * tq:(hd + 1) * tq]
            ones_lane = HEAD_DIM * (1 - half)
            o = acc / acc[:, ones_lane:ones_lane + 1]
            heads.append(o if half == hd else pltpu.roll(o, HEAD_DIM, 1))
        o = jnp.where(low_q, heads[0], heads[1])
        o_ref[:, cols] = (o * _silu(gate_ref[:, cols].astype(F32))).astype(BF16)


def _key_chunk(tk):
    for mult in (2 * LANES, LANES, 16):
        try:
            return _tile(tk, 1024, mult)
        except ValueError:
            pass
    raise ValueError(tk)


def _dense_attn(q, q_off, gate, gate_off, kv_srcs, kv_grouped, q_scale):
    t = q.shape[0]
    tq = _tile(t, 256, 16)
    kw = KV_W if kv_grouped else BRANCH_W
    in_specs = [_slab(tq, BRANCH_W, q_off), _slab(tq, BRANCH_W, gate_off)]
    args = [q, gate]
    chunks = []
    for k_arr, k_off, v_arr, v_off in kv_srcs:
        tk = k_arr.shape[0]
        assert k_off % kw == 0 and v_off % kw == 0
        in_specs += [_resident((tk, kw), lambda i, _c=k_off // kw: (0, _c)),
                     _resident((tk, kw), lambda i, _c=v_off // kw: (0, _c))]
        args += [k_arr, v_arr]
        chunks.append(_key_chunk(tk))
    n_pair = N_HEADS // 2
    return pl.pallas_call(
        functools.partial(_dense_attn_body, n_src=len(kv_srcs), chunks=tuple(chunks),
                          kv_grouped=kv_grouped, q_scale=q_scale, tq=tq),
        grid=(t // tq,),
        in_specs=in_specs,
        out_specs=pl.BlockSpec((tq, BRANCH_W), lambda i: (i, 0)),
        out_shape=jax.ShapeDtypeStruct((t, BRANCH_W), BF16),
        scratch_shapes=[pltpu.VMEM((2 * tq, 1), F32)] * n_pair + [pltpu.VMEM((2 * tq, LANES), F32)] * n_pair,
        compiler_params=_params(dimension_semantics=("arbitrary",)),
        name="dense_attn",
    )(*args)


def _gqa_body(q_ref, qn_ref, gate_ref, k_ref, vt_ref, kx_ref, vtx_ref, o_ref, *scratch, tq, n_chunk):
    n_pair = N_HEADS // 2
    qt_s, qtn_s, m_s, acc_s, sx_s = (scratch[i * n_pair:(i + 1) * n_pair] for i in range(5))
    s_s = [scratch[5 * n_pair + 2 * p:5 * n_pair + 2 * p + 2] for p in range(n_pair)]
    low_q = lax.broadcasted_iota(jnp.int32, (tq, LANES), 1) < HEAD_DIM
    kv_of = [(2 * p) // (N_HEADS // GQA_KV_HEADS) for p in range(n_pair)]

    def scores(p, keys, s_ref):
        s_ref[...] = jnp.dot(keys, qt_s[p][...], preferred_element_type=F32)

    def consume(p, s_ref, vt):
        s = s_ref[...]
        m_old = m_s[p][...]
        m_new = jnp.maximum(m_old, jnp.max(s, axis=0, keepdims=True))
        alpha = jnp.exp2(m_old - m_new)
        pt = jnp.exp2(s - m_new).astype(BF16)
        for hd in range(2):
            cols = slice(hd * tq, (hd + 1) * tq)
            acc_s[p][hd] = alpha[:, cols] * acc_s[p][hd] + jnp.dot(vt, pt[:, cols], preferred_element_type=F32)
        m_s[p][...] = m_new

    def step(i, slot):
        for p in range(n_pair):
            scores(p, k_ref[i + 1], s_s[p][1 - slot])
            consume(p, s_s[p][slot], vt_ref[i, kv_of[p]])

    def transposed_queries(p, q_tile_ref):
        qp = q_tile_ref[:, LANES * p:LANES * (p + 1)].astype(F32)
        qa, qb = _pair_queries(qp, low_q, kv_of[p])
        return jnp.concatenate([qa, qb], axis=0).T.astype(BF16)

    is_first = pl.program_id(0) == 0

    @pl.when(is_first)
    def _():
        for p in range(n_pair):
            qt_s[p][...] = transposed_queries(p, q_ref)
            scores(p, k_ref[0], s_s[p][0])

    @pl.when(jnp.logical_not(is_first))
    def _():
        for p in range(n_pair):
            qt_s[p][...] = qtn_s[p][...]

    for p in range(n_pair):
        m_s[p][...] = jnp.full(m_s[p].shape, NEG_BIG, F32)
        acc_s[p][...] = jnp.zeros(acc_s[p].shape, F32)

    def two_steps(j, carry):
        step(2 * j, 0)
        step(2 * j + 1, 1)
        return carry

    n_step = n_chunk - 1
    lax.fori_loop(0, n_step // 2, two_steps, 0)
    last = n_chunk - 1
    if n_step % 2:
        step(last - 1, 0)
    def next_scores(p):
        s_s[p][0][...] = jnp.dot(k_ref[0], qtn_s[p][...], preferred_element_type=F32)

    for p in range(n_pair):
        qtn_s[p][...] = transposed_queries(p, qn_ref)
    for p in range(n_pair):
        scores(p, kx_ref[...], sx_s[p])
        if last % 2:
            next_scores(p)
        consume(p, s_s[p][last % 2], vt_ref[last, kv_of[p]])
    for p in range(n_pair):
        if not last % 2:
            next_scores(p)
        consume(p, sx_s[p], vtx_ref[0, kv_of[p]])
        acc = acc_s[p]
        ot = jnp.concatenate([acc[hd, :HEAD_DIM] / acc[hd, HEAD_DIM:HEAD_DIM + 1] for hd in range(2)], axis=0)
        cols = slice(LANES * p, LANES * (p + 1))
        o_ref[:, cols] = (ot.T * _silu(gate_ref[:, cols].astype(F32))).astype(BF16)


def _gqa_attn(q, gate, gate_off, k, vt, k_extra, vt_extra):
    t = q.shape[0]
    n_chunk, _, _, ck = vt.shape
    tx = k_extra.shape[0]
    assert vt_extra.shape[0] == 1 and k.shape[0] == n_chunk * ck
    tq = _tile(t, 256, LANES)
    n_pair = N_HEADS // 2
    scratch = ([pltpu.VMEM((LANES, 2 * tq), BF16)] * (2 * n_pair) + [pltpu.VMEM((1, 2 * tq), F32)] * n_pair
               + [pltpu.VMEM((2, VT_ROWS, tq), F32)] * n_pair + [pltpu.VMEM((tx, 2 * tq), F32)] * n_pair
               + [pltpu.VMEM((ck, 2 * tq), F32)] * (2 * n_pair))
    return pl.pallas_call(
        functools.partial(_gqa_body, tq=tq, n_chunk=n_chunk),
        grid=(t // tq,),
        in_specs=[pl.BlockSpec((tq, BRANCH_W), lambda i: (i, 0)),
                  pl.BlockSpec((tq, BRANCH_W), lambda i: (jnp.minimum(i + 1, t // tq - 1), 0)),
                  _slab(tq, BRANCH_W, gate_off),
                  _resident((n_chunk, ck, KV_W), lambda i: (0, 0, 0)),
                  _resident(vt.shape, lambda i: (0, 0, 0, 0)),
                  _resident((tx, KV_W), lambda i: (0, 0)),
                  _resident(vt_extra.shape, lambda i: (0, 0, 0, 0))],
        out_specs=pl.BlockSpec((tq, BRANCH_W), lambda i: (i, 0)),
        out_shape=jax.ShapeDtypeStruct((t, BRANCH_W), BF16),
        scratch_shapes=scratch,
        compiler_params=_params(dimension_semantics=("arbitrary",)),
        name="gqa_attn",
    )(q, q, gate, k.reshape(n_chunk, ck, KV_W), vt, k_extra, vt_extra)


def _na_bias_tiles(rpb):
    cq = np.arange(GRID_W)
    col_start = np.clip(cq - NA_WIN_COLS // 2, 0, GRID_W - NA_WIN_COLS)
    col_ok = (cq[None, :] >= col_start[:, None]) & (cq[None, :] < col_start[:, None] + NA_WIN_COLS)
    col_off = np.clip(cq[None, :] - cq[:, None], -(NA_WIN_COLS - 1), NA_WIN_COLS - 1) + NA_WIN_COLS - 1
    onehot = jnp.asarray(col_off[:, :, None] == np.arange(2 * NA_WIN_COLS - 1), F32)
    tiles = jnp.einsum("lhrm,qkm->lhrqk", rpb.astype(F32), onehot, precision=lax.Precision.HIGHEST)
    tiles = jnp.where(col_ok, tiles * LOG2_E, NEG_BIG)
    masked = jnp.full(tiles.shape[:2] + (1, GRID_W, GRID_W), NEG_BIG, F32)
    tiles = jnp.concatenate([tiles, masked], axis=2)
    return jnp.concatenate([tiles, tiles], axis=-1)


def _na_body(q_ref, gate_ref, k_ref, v_ref, kc_ref, vc_ref, tiles_ref, o_ref, *, n_rows):
    rb, band = NA_ROW_BLOCK, NA_BAND_ROWS
    tq, nk = rb * GRID_W, band * GRID_W
    b = pl.program_id(0)
    u0 = jnp.clip(b * rb - NA_WIN_ROWS // 2, 0, n_rows - band)
    rows = pl.ds(pl.multiple_of(u0 * GRID_W, GRID_W), nk)
    low_q = lax.broadcasted_iota(jnp.int32, (tq, LANES), 1) < HEAD_DIM
    low_t = lax.broadcasted_iota(jnp.int32, (GRID_W, LANES), 1) < GRID_W

    def bias_rows(h, i):
        r = b * rb + i
        row_start = jnp.clip(r - NA_WIN_ROWS // 2, 0, n_rows - NA_WIN_ROWS)
        pieces = []
        for j in range(band):
            key_row = u0 + j
            in_window = jnp.logical_and(key_row >= row_start, key_row < row_start + NA_WIN_ROWS)
            ro = jnp.where(in_window, key_row - r + NA_WIN_ROWS - 1, 2 * NA_WIN_ROWS - 1)
            pieces.append(tiles_ref[0, h, ro])
        return jnp.concatenate([jnp.where(low_t, pieces[j], pieces[j + 1]) for j in range(0, band, 2)], axis=1)

    for p in range(N_HEADS // 2):
        bias = jnp.concatenate([bias_rows(2 * p + hd, i) for hd in range(2) for i in range(rb)], axis=0)
        cols = slice(LANES * p, LANES * (p + 1))
        qp = q_ref[:, cols].astype(F32) * SCORE_SCALE_LOG2
        qa, qb = _pair_queries(qp, low_q, None)
        lhs = jnp.concatenate([qa, qb], axis=0).astype(BF16)
        kb, vb = k_ref[rows, cols], v_ref[rows, cols]
        kc, vc = kc_ref[:, cols], vc_ref[:, cols]
        s_band = _nt_dot(lhs, kb) + bias
        s_ctx = _nt_dot(lhs, kc)
        m = jnp.maximum(jnp.max(s_band, axis=-1, keepdims=True), jnp.max(s_ctx, axis=-1, keepdims=True))
        pb = jnp.exp2(s_band - m).astype(BF16)
        pc = jnp.exp2(s_ctx - m).astype(BF16)
        heads = []
        for hd in range(2):
            r = slice(hd * tq, (hd + 1) * tq)
            acc = (jnp.dot(pb[r], _values_with_ones(vb, hd), preferred_element_type=F32)
                   + jnp.dot(pc[r], _values_with_ones(vc, hd), preferred_element_type=F32))
            ones_lane = HEAD_DIM * (1 - hd)
            heads.append(acc / acc[:, ones_lane:ones_lane + 1])
        o = jnp.where(low_q, heads[0], heads[1])
        o_ref[:, cols] = (o * _silu(gate_ref[:, cols].astype(F32))).astype(BF16)


def _na_attn(p, off, pc, bias, layer):
    s = p.shape[0]
    cn = pc.shape[0]
    n_rows = s // GRID_W
    tq = NA_ROW_BLOCK * GRID_W
    assert n_rows % NA_ROW_BLOCK == 0 and n_rows >= NA_BAND_ROWS >= NA_ROW_BLOCK + NA_WIN_ROWS - 1
    assert NA_BAND_ROWS % 2 == 0
    nb = n_rows // NA_ROW_BLOCK
    w = BRANCH_W
    return pl.pallas_call(
        functools.partial(_na_body, n_rows=n_rows),
        grid=(nb,),
        in_specs=[_slab(tq, w, off["a_q"]), _slab(tq, w, off["a_gate"]),
                  _resident((s, w), lambda i, _c=off["a_k"] // w: (0, _c)),
                  _resident((s, w), lambda i, _c=off["a_v"] // w: (0, _c)),
                  _resident((cn, w), lambda i, _c=off["a_k"] // w: (0, _c)),
                  _resident((cn, w), lambda i, _c=off["a_v"] // w: (0, _c)),
                  _resident((1,) + bias.shape[1:], lambda i: (layer, 0, 0, 0, 0))],
        out_specs=pl.BlockSpec((tq, w), lambda i: (i, 0)),
        out_shape=jax.ShapeDtypeStruct((s, w), BF16),
        compiler_params=_params(dimension_semantics=("arbitrary",)),
        name="na_attn",
    )(p, p, p, p, pc, pc, bias)


def _halo_specs(t, tm, width, off):
    assert off % LANES == 0 and tm % HALO == 0
    block = lambda rows: (pl.Element(rows), pl.Element(width))
    per, last = tm // HALO, t // HALO - 1
    return [pl.BlockSpec(block(tm), lambda i: (i * tm, off)),
            pl.BlockSpec(block(HALO), lambda i: (jnp.maximum(i * per - 1, 0) * HALO, off)),
            pl.BlockSpec(block(HALO), lambda i: (jnp.minimum((i + 1) * per, last) * HALO, off))]


def _fill_padded(pad_ref, cur, prev, nxt, tm):
    i, n = pl.program_id(0), pl.num_programs(0)
    pad_ref[0:HALO] = jnp.where(i > 0, prev, jnp.zeros_like(prev))
    pad_ref[HALO:HALO + tm] = cur
    pad_ref[HALO + tm:] = jnp.where(i < n - 1, nxt, jnp.zeros_like(nxt))


def _pool_body(u_ref, up_ref, un_ref, gate_ref, w_ref, sc_ref, o_ref, pad_ref, *, tm, seq):
    _fill_padded(pad_ref, u_ref[...].astype(F32), up_ref[...].astype(F32), un_ref[...].astype(F32), tm)
    t = pl.program_id(0) * tm + lax.broadcasted_iota(jnp.int32, (tm, POOL_GROUP), 0)
    for gi, ksz in enumerate(POOL_SIZES):
        cols = slice(POOL_GROUP * gi, POOL_GROUP * (gi + 1))
        back = ksz // 2
        tot = pad_ref[HALO - back:HALO - back + tm, cols]
        for d in range(1 - back, ksz - back):
            tot = tot + pad_ref[HALO + d:HALO + d + tm, cols]
        lo = jnp.maximum(t - back, 0)
        hi = jnp.minimum(t + (ksz - 1 - back), seq - 1)
        mean = tot / (hi - lo + 1).astype(F32)
        dlt = (mean - pad_ref[HALO:HALO + tm, cols]).astype(BF16)
        y = jnp.dot(dlt, w_ref[0, gi].astype(BF16), preferred_element_type=F32) * sc_ref[:, cols]
        o_ref[:, cols] = (y * _silu(gate_ref[:, cols].astype(F32))).astype(BF16)


def _pool(p, off, w_pool, layer, pool_scale):
    t = p.shape[0]
    tm = _tile(t, 512, HALO)
    w = BRANCH_W
    return pl.pallas_call(
        functools.partial(_pool_body, tm=tm, seq=t),
        grid=(t // tm,),
        in_specs=_halo_specs(t, tm, w, off["b_in"]) + [
            _slab(tm, w, off["b_gate"]),
            pl.BlockSpec((1,) + w_pool.shape[1:], lambda i: (layer, 0, 0, 0)),
            pl.BlockSpec((1, w), lambda i: (0, 0))],
        out_specs=pl.BlockSpec((tm, w), lambda i: (i, 0)),
        out_shape=jax.ShapeDtypeStruct((t, w), BF16),
        scratch_shapes=[pltpu.VMEM((tm + 2 * HALO, w), F32)],
        compiler_params=_params(dimension_semantics=("arbitrary",)),
        name="pool",
    )(p, p, p, p, w_pool, pool_scale)


def _glu(x):
    x = x.astype(F32)
    return x[:, :BRANCH_W] * jax.nn.sigmoid(x[:, BRANCH_W:])


def _conv_body(x_ref, xp_ref, xn_ref, gate_ref, cw_ref, cb_ref, lg_ref, lb_ref, pw_ref, o_ref, pad_ref, sh_ref,
               *, tm):
    _fill_padded(pad_ref, _glu(x_ref[...]), _glu(xp_ref[...]), _glu(xn_ref[...]), tm)
    span = sh_ref.shape[1]
    for b in range(1, SUBLANES):
        sh_ref[b - 1] = pad_ref[b:b + span, :]
    reach = CONV_WIDTH // 2
    y = jnp.zeros((tm, BRANCH_W), F32) + cb_ref[...]
    for j in range(CONV_WIDTH):
        a, b = divmod(HALO - reach + j, SUBLANES)
        rows = slice(SUBLANES * a, SUBLANES * a + tm)
        y = y + (pad_ref[rows, :] if b == 0 else sh_ref[b - 1, rows, :]) * cw_ref[j:j + 1, :]
    mu = jnp.mean(y, axis=-1, keepdims=True)
    yc = y - mu
    var = jnp.mean(yc * yc, axis=-1, keepdims=True)
    z = _silu(yc * lax.rsqrt(var + EPS) * lg_ref[...] + lb_ref[...]).astype(BF16)
    out = jnp.dot(z, pw_ref[0].astype(BF16), preferred_element_type=F32)
    o_ref[...] = (out * _silu(gate_ref[...].astype(F32))).astype(BF16)


def _conv(p, off, conv_w, conv_b, ln_g, ln_b, w_pw, layer):
    t = p.shape[0]
    tm = _tile(t, 512, HALO)
    w = BRANCH_W
    vec = pl.BlockSpec((1, w), lambda i: (0, 0))
    return pl.pallas_call(
        functools.partial(_conv_body, tm=tm),
        grid=(t // tm,),
        in_specs=_halo_specs(t, tm, 2 * w, off["d_glu"]) + [
            _slab(tm, w, off["d_gate"]),
            pl.BlockSpec(conv_w.shape, lambda i: (0, 0)), vec, vec, vec,
            pl.BlockSpec((1, w, w), lambda i: (layer, 0, 0))],
        out_specs=pl.BlockSpec((tm, w), lambda i: (i, 0)),
        out_shape=jax.ShapeDtypeStruct((t, w), BF16),
        scratch_shapes=[pltpu.VMEM((tm + 2 * HALO, w), F32),
                        pltpu.VMEM((SUBLANES - 1, tm + 2 * HALO - SUBLANES, w), F32)],
        compiler_params=_params(dimension_semantics=("arbitrary",)),
        name="conv",
    )(p, p, p, p, conv_w, conv_b, ln_g, ln_b, w_pw)


def _cast_once(w_ref, wb_ref):
    @pl.when(pl.program_id(0) == 0)
    def _():
        wb_ref[...] = w_ref[0].astype(BF16)


def _merge_body(m0, m1, m2, m3, o0, o1, o2, o3, w_ref, y_ref, wb_ref):
    _cast_once(w_ref, wb_ref)
    y = None
    for bi, (m_ref, o_ref) in enumerate(((m0, o0), (m1, o1), (m2, o2), (m3, o3))):
        proj = jnp.dot(o_ref[...], wb_ref[bi], preferred_element_type=F32)
        term = (1.0 + jnp.tanh(0.5 * m_ref[...].astype(F32))) * proj
        y = term if y is None else y + term
    y_ref[...] = (0.5 * y).astype(BF16)


def _merge(p, merge_off, outs, w_branch, layer):
    t = p.shape[0]
    d = w_branch.shape[-1]
    tm = _tile(t, 256, 16)
    return pl.pallas_call(
        _merge_body,
        grid=(t // tm,),
        in_specs=[_slab(tm, d, merge_off + bi * d) for bi in range(N_BRANCH)]
        + [pl.BlockSpec((tm, BRANCH_W), lambda i: (i, 0))] * N_BRANCH
        + [_resident((1,) + w_branch.shape[1:], lambda i: (layer, 0, 0, 0))],
        out_specs=pl.BlockSpec((tm, d), lambda i: (i, 0)),
        out_shape=jax.ShapeDtypeStruct((t, d), BF16),
        scratch_shapes=[pltpu.VMEM(w_branch.shape[1:], BF16)],
        compiler_params=_params(dimension_semantics=("arbitrary",)),
        name="merge",
    )(p, p, p, p, *outs, w_branch)


def _out_body(y_ref, w_ref, x_ref, g_ref, gate_ref, *rest):
    *rest, wb_ref = rest
    _cast_once(w_ref, wb_ref)
    y = jnp.dot(y_ref[...], wb_ref[...], preferred_element_type=F32)
    yn = y * lax.rsqrt(jnp.mean(y * y, axis=-1, keepdims=True) + EPS) * g_ref[...]
    x_next = x_ref[...] + gate_ref[...] * yn
    if len(rest) == 1:
        rest[0][...] = x_next
    else:
        gn_ref, sh_ref, sc_ref, o_ref, h_ref = rest
        o_ref[...] = x_next
        h_ref[...] = _modulated(x_next, gn_ref[...], sh_ref[...], sc_ref[...])


def _out_proj(y, w_out, layer, x, g_post, gate, next_mod=None):
    t, d = x.shape
    tm = _tile(t, 256, 16)
    vec = pl.BlockSpec((1, d), lambda i: (0, 0))
    row = pl.BlockSpec((tm, d), lambda i: (i, 0))
    n_next = 0 if next_mod is None else 1
    out = pl.pallas_call(
        _out_body,
        grid=(t // tm,),
        in_specs=[row, _resident((1, d, d), lambda i: (layer, 0, 0)), row, vec, vec] + [vec] * (3 * n_next),
        scratch_shapes=[pltpu.VMEM((d, d), BF16)],
        out_specs=[row] * (1 + n_next),
        out_shape=[jax.ShapeDtypeStruct((t, d), F32)] + [jax.ShapeDtypeStruct((t, d), BF16)] * n_next,
        compiler_params=_params(dimension_semantics=("arbitrary",)),
        name="out_proj",
    )(y, w_out, x, g_post, gate, *(next_mod or ()))
    return out if n_next else (out[0], None)


def kernel(x, c, ctx, c_ctx, w_ada, b_ada, g_pre, g_post, w_in, na_rpb, pool_w, pool_scale,
           q_norm, k_norm, conv_w, conv_b, conv_ln_g, conv_ln_b, conv_pw, w_branch, w_out):
    batch, seq, d = x.shape
    assert batch == 1 and seq % GRID_W == 0
    cn = ctx.shape[1]
    depth = w_ada.shape[0]
    off = _layout(d)

    xs, cs = x[0], ctx[0]
    cvec = jnp.zeros((8, d), F32).at[0].set(c[0]).at[1].set(c_ctx)
    ada = _ada(cvec, w_ada, b_ada)

    ones_bd = jnp.asarray(np.kron(np.eye(N_HEADS), np.ones((HEAD_DIM, HEAD_DIM))), BF16)
    rope_tabs = _rope_tables(seq)
    na_bias = _na_bias_tiles(na_rpb)
    row = lambda v: v.reshape(1, -1)
    qn_all = jnp.tile(q_norm, (1, N_HEADS))
    kn_all = jnp.tile(k_norm, (1, GQA_KV_HEADS))

    mod = lambda l, r: tuple(ada[l, r:r + 1, k * d:(k + 1) * d] for k in range(3))
    h = _modulate(xs, row(g_pre[0]), *mod(0, 0)[:2])
    hc = _modulate(cs, row(g_pre[0]), *mod(0, 1)[:2])
    for l in range(depth):
        last = l == depth - 1
        conv_args = (conv_w[l], row(conv_b[l]), row(conv_ln_g[l]), row(conv_ln_b[l]), conv_pw, l)
        gate, gate_c = mod(l, 0)[2], mod(l, 1)[2]
        next_mod = lambda r: None if last else (row(g_pre[l + 1]),) + mod(l + 1, r)[:2]
        qn, kn = row(qn_all[l]), row(kn_all[l])

        p, pc = _in_proj(h, hc, w_in, l, KV_COLS if last else None)
        qc_ctx, kq_ctx, vt_ctx = _qk_prep(pc, None if last else off["c_q"], off["c_k"], off["c_v"],
                                          qn, kn, ones_bd, None)

        qc, kc, vt = _qk_prep(p, off["c_q"], off["c_k"], off["c_v"], qn, kn, ones_bd, rope_tabs)
        o_c = _gqa_attn(qc, p, off["c_gate"], kc, vt, kq_ctx, vt_ctx)
        o_a = _na_attn(p, off, pc, na_bias, l)
        o_b = _pool(p, off, pool_w, l, row(pool_scale[l]))
        o_d = _conv(p, off, *conv_args)
        y = _merge(p, off["merge"], (o_a, o_b, o_c, o_d), w_branch, l)
        xs, h = _out_proj(y, w_out, l, xs, row(g_post[l]), gate, next_mod(0))

        if not last:
            o_a_c = _dense_attn(pc, off["a_q"], pc, off["a_gate"],
                                [(pc, off["a_k"], pc, off["a_v"])], False, SCORE_SCALE_LOG2)
            o_c_c = _dense_attn(qc_ctx, 0, pc, off["c_gate"], [(kq_ctx, 0, pc, off["c_v"])], True, 1.0)
            o_b_c = _pool(pc, off, pool_w, l, row(pool_scale[l]))
            o_d_c = _conv(pc, off, *conv_args)
            y_c = _merge(pc, off["merge"], (o_a_c, o_b_c, o_c_c, o_d_c), w_branch, l)
            cs, hc = _out_proj(y_c, w_out, l, cs, row(g_post[l]), gate_c, next_mod(1))
    return xs[None]
```

```python
import functools

import numpy as np
import jax
import jax.numpy as jnp
from jax import lax
from jax.experimental import pallas as pl
from jax.experimental.pallas import tpu as pltpu

F32 = jnp.float32
BF16 = jnp.bfloat16

GRID_W = 64
HEAD_DIM = 64
BRANCH_W = 512
N_BRANCH = 4
N_HEADS = BRANCH_W // HEAD_DIM
GQA_KV_HEADS = 2
KV_W = GQA_KV_HEADS * HEAD_DIM
NA_WIN_ROWS = 8
NA_WIN_COLS = 16
POOL_SIZES = (2, 4, 8, 16)
POOL_GROUP = BRANCH_W // len(POOL_SIZES)
ROPE_THETA = 10000.0
CONV_WIDTH = 31
EPS = 1e-6

LANES = 128
SUBLANES = 8
HALO = 16
NA_ROW_BLOCK = 4
NA_BAND_ROWS = 12
NEG_BIG = -1e30
LOG2_E = 1.4426950408889634
SCORE_SCALE_LOG2 = HEAD_DIM ** -0.5 * LOG2_E
V7X_VMEM_LIMIT = 52 * 1024 * 1024

PARTS = (("a_k", BRANCH_W), ("a_v", BRANCH_W), ("c_k", KV_W), ("c_v", KV_W), ("a_q", BRANCH_W), ("c_q", BRANCH_W),
         ("a_gate", BRANCH_W), ("b_in", BRANCH_W), ("b_gate", BRANCH_W), ("c_gate", BRANCH_W),
         ("d_glu", 2 * BRANCH_W), ("d_gate", BRANCH_W), ("merge", None))
KV_COLS = 2 * BRANCH_W + 2 * KV_W


def _layout(d_model):
    off, o = {}, 0
    for n, w in PARTS:
        off[n] = o
        o += N_BRANCH * d_model if w is None else w
    return off


def _params(**kw):
    return pltpu.CompilerParams(vmem_limit_bytes=V7X_VMEM_LIMIT, **kw)


def _tile(n, pref, mult):
    if n <= pref:
        return n
    t = (pref // mult) * mult
    while t >= mult:
        if n % t == 0:
            return t
        t -= mult
    raise ValueError(f"no tile for {n}")


def _resident(block_shape, index_map):
    return pl.BlockSpec(block_shape, index_map, pipeline_mode=pl.Buffered(1))


def _slab(tm, width, off):
    assert off % LANES == 0
    return pl.BlockSpec((pl.Element(tm), pl.Element(width)), lambda i: (i * tm, off))


def _silu(x):
    return x * jax.nn.sigmoid(x)


def _ada_body(cs_ref, w_ref, b_ref, o_ref):
    s = _silu(cs_ref[...]).astype(BF16)
    o_ref[0] = jnp.dot(s, w_ref[0].astype(BF16), preferred_element_type=F32) + b_ref[0]


def _ada(cs, w_ada, b_ada):
    depth, d, n = w_ada.shape
    tn = _tile(n, 1024, LANES)
    return pl.pallas_call(
        _ada_body,
        grid=(depth, n // tn),
        in_specs=[pl.BlockSpec((8, d), lambda l, j: (0, 0)),
                  pl.BlockSpec((1, d, tn), lambda l, j: (l, 0, j)),
                  pl.BlockSpec((1, 1, tn), lambda l, j: (l, 0, j))],
        out_specs=pl.BlockSpec((1, 8, tn), lambda l, j: (l, 0, j)),
        out_shape=jax.ShapeDtypeStruct((depth, 8, n), F32),
        compiler_params=_params(dimension_semantics=("arbitrary", "arbitrary")),
        name="ada",
    )(cs, w_ada, b_ada.reshape(depth, 1, n))


def _modulated(x, g, shift, scale):
    y = x * lax.rsqrt(jnp.mean(x * x, axis=-1, keepdims=True) + EPS) * g
    return (y * (1.0 + scale) + shift).astype(BF16)


def _modulate_body(x_ref, g_ref, sh_ref, sc_ref, o_ref):
    o_ref[...] = _modulated(x_ref[...], g_ref[...], sh_ref[...], sc_ref[...])


def _modulate(x, g, shift, scale):
    t, d = x.shape
    tm = _tile(t, 512, 16)
    vec = pl.BlockSpec((1, d), lambda i: (0, 0))
    return pl.pallas_call(
        _modulate_body,
        grid=(t // tm,),
        in_specs=[pl.BlockSpec((tm, d), lambda i: (i, 0)), vec, vec, vec],
        out_specs=pl.BlockSpec((tm, d), lambda i: (i, 0)),
        out_shape=jax.ShapeDtypeStruct((t, d), BF16),
        compiler_params=_params(dimension_semantics=("arbitrary",)),
        name="modulate",
    )(x, g, shift, scale)


def _in_proj_body(h_ref, w_ref, o_ref, wb_ref):
    @pl.when(pl.program_id(1) == 0)
    def _():
        wb_ref[...] = w_ref[0].astype(BF16)

    o_ref[...] = jnp.dot(h_ref[...], wb_ref[...], preferred_element_type=F32).astype(o_ref.dtype)


def _col_tile(n):
    for mult in (2 * LANES, LANES):
        try:
            return _tile(n, 1536, mult)
        except ValueError:
            pass
    raise ValueError(n)


def _in_proj(h, w_in, layer, n_cols=None):
    t, d = h.shape
    n = w_in.shape[2]
    tm = _tile(t, 1024, 16)
    tn = _col_tile(n)
    n_tiles = n // tn if n_cols is None else pl.cdiv(n_cols, tn)
    return pl.pallas_call(
        _in_proj_body,
        grid=(n_tiles, t // tm),
        in_specs=[pl.BlockSpec((tm, d), lambda j, i: (i, 0)),
                  pl.BlockSpec((1, d, tn), lambda j, i: (layer, 0, j))],
        out_specs=pl.BlockSpec((tm, tn), lambda j, i: (i, j)),
        out_shape=jax.ShapeDtypeStruct((t, n_tiles * tn), BF16),
        scratch_shapes=[pltpu.VMEM((d, tn), BF16)],
        compiler_params=_params(dimension_semantics=("arbitrary", "arbitrary")),
        name="in_proj",
    )(h, w_in)


def _head_meansq(x, ones_bd):
    ss = x * x
    hi = ss.astype(BF16)
    lo = (ss - hi.astype(F32)).astype(BF16)
    tot = (jnp.dot(hi, ones_bd, preferred_element_type=F32)
           + jnp.dot(lo, ones_bd, preferred_element_type=F32))
    return tot * (1.0 / HEAD_DIM)


def _rope(y, cos, sin_signed):
    w = y.shape[-1]
    lane = lax.broadcasted_iota(jnp.int32, y.shape, 1)
    nxt = pltpu.roll(y, w - 16, 1)
    prv = pltpu.roll(y, 16, 1)
    return y * cos + jnp.where((lane % 32) < 16, nxt, prv) * sin_signed


def _norm_rope(x, w, bd, rope):
    x = x * lax.rsqrt(_head_meansq(x, bd) + EPS) * w
    if rope is not None:
        reps = x.shape[-1] // LANES
        x = _rope(x, jnp.concatenate([rope[0]] * reps, axis=1), jnp.concatenate([rope[1]] * reps, axis=1))
    return x


VT_ROWS = HEAD_DIM + 16


def _qk_prep_body(*refs, use_rope, with_q):
    refs = list(refs)
    q_ref = refs.pop(0) if with_q else None
    k_ref, v_ref = refs.pop(0), refs.pop(0)
    qn_ref = refs.pop(0) if with_q else None
    kn_ref, bd_ref = refs.pop(0), refs.pop(0)
    rope = (refs.pop(0)[...], refs.pop(0)[...]) if use_rope else None
    bd = bd_ref[...]
    if with_q:
        q = _norm_rope(q_ref[...].astype(F32), qn_ref[...], bd, rope)
        refs.pop(0)[...] = (q * SCORE_SCALE_LOG2).astype(BF16)
    k = _norm_rope(k_ref[...].astype(F32), kn_ref[...], bd[:KV_W, :KV_W], rope)
    refs.pop(0)[...] = k.astype(BF16)
    vt_ref = refs.pop(0)
    vt = v_ref[...].astype(F32).T
    row = lax.broadcasted_iota(jnp.int32, (VT_ROWS - HEAD_DIM, vt.shape[1]), 0)
    tail = jnp.where(row == 0, 1.0, 0.0)
    for g in range(GQA_KV_HEADS):
        vt_ref[0, g] = jnp.concatenate([vt[HEAD_DIM * g:HEAD_DIM * (g + 1)], tail], axis=0).astype(BF16)


def _qk_prep(p, q_off, k_off, v_off, qn, kn, ones_bd, rope_tabs):
    t = p.shape[0]
    tm = _tile(t, 1024, LANES)
    use_rope, with_q = rope_tabs is not None, q_off is not None
    const = lambda shape: pl.BlockSpec(shape, lambda i: (0, 0))
    in_specs, args, out_specs, out_shape = [], [], [], []
    if with_q:
        in_specs.append(_slab(tm, BRANCH_W, q_off))
        args.append(p)
    in_specs += [_slab(tm, KV_W, k_off), _slab(tm, KV_W, v_off)]
    args += [p, p]
    if with_q:
        in_specs.append(const((1, BRANCH_W)))
        args.append(qn)
        out_specs.append(pl.BlockSpec((tm, BRANCH_W), lambda i: (i, 0)))
        out_shape.append(jax.ShapeDtypeStruct((t, BRANCH_W), BF16))
    in_specs += [const((1, KV_W)), const((BRANCH_W, BRANCH_W))]
    args += [kn, ones_bd]
    if use_rope:
        in_specs += [pl.BlockSpec((tm, LANES), lambda i: (i, 0))] * 2
        args += list(rope_tabs)
    out_specs += [pl.BlockSpec((tm, KV_W), lambda i: (i, 0)),
                  pl.BlockSpec((1, GQA_KV_HEADS, VT_ROWS, tm), lambda i: (i, 0, 0, 0))]
    out_shape += [jax.ShapeDtypeStruct((t, KV_W), BF16),
                  jax.ShapeDtypeStruct((t // tm, GQA_KV_HEADS, VT_ROWS, tm), BF16)]
    res = pl.pallas_call(
        functools.partial(_qk_prep_body, use_rope=use_rope, with_q=with_q),
        grid=(t // tm,),
        in_specs=in_specs,
        out_specs=out_specs,
        out_shape=out_shape,
        compiler_params=_params(dimension_semantics=("arbitrary",)),
        name="qk_prep",
    )(*args)
    return res if with_q else (None, res[0], res[1])


def _rope_tables(seq):
    half = HEAD_DIM // 4
    t = lax.broadcasted_iota(jnp.int32, (seq, LANES), 0)
    lane = lax.broadcasted_iota(jnp.int32, (seq, LANES), 1)
    pos = jnp.where((lane % HEAD_DIM) < 2 * half, t // GRID_W, t % GRID_W).astype(F32)
    freqs = ROPE_THETA ** (-(lane % half).astype(F32) / half)
    ang = pos * freqs
    sign = jnp.where((lane % (2 * half)) < half, -1.0, 1.0)
    return jnp.cos(ang), jnp.sin(ang) * sign


def _nt_dot(a, b):
    return lax.dot_general(a, b, (((1,), (1,)), ((), ())), preferred_element_type=F32)


def _pair_queries(qp, low, shared_kv_lanes):
    zero = jnp.zeros_like(qp)
    if shared_kv_lanes is None:
        return jnp.where(low, qp, zero), jnp.where(low, zero, qp)
    qr = pltpu.roll(qp, HEAD_DIM, 1)
    if shared_kv_lanes == 0:
        return jnp.where(low, qp, zero), jnp.where(low, qr, zero)
    return jnp.where(low, zero, qr), jnp.where(low, zero, qp)


def _values_with_ones(vc, half):
    lane = lax.broadcasted_iota(jnp.int32, vc.shape, 1)
    keep = (lane < HEAD_DIM) if half == 0 else (lane >= HEAD_DIM)
    ones_lane = HEAD_DIM * (1 - half)
    fill = jnp.where(lane == ones_lane, 1.0, 0.0)
    return jnp.where(keep, vc.astype(F32), fill).astype(vc.dtype)


def _dense_attn_body(*refs, n_src, chunks, kv_grouped, q_scale, tq):
    n_pair = N_HEADS // 2
    q_ref, gate_ref = refs[0], refs[1]
    srcs = [(refs[2 + 2 * i], refs[3 + 2 * i]) for i in range(n_src)]
    o_ref = refs[2 + 2 * n_src]
    scratch = refs[3 + 2 * n_src:]
    m_s, acc_s = scratch[:n_pair], scratch[n_pair:]
    low_q = lax.broadcasted_iota(jnp.int32, (tq, LANES), 1) < HEAD_DIM

    halves = [((2 * p) // (N_HEADS // GQA_KV_HEADS),) * 2 if kv_grouped else (0, 1) for p in range(n_pair)]
    lhs = []
    for p in range(n_pair):
        qp = q_ref[:, LANES * p:LANES * (p + 1)].astype(F32) * q_scale
        qa, qb = _pair_queries(qp, low_q, halves[p][0] if kv_grouped else None)
        lhs.append(jnp.concatenate([qa, qb], axis=0).astype(BF16))
        m_s[p][...] = jnp.full(m_s[p].shape, NEG_BIG, F32)
        acc_s[p][...] = jnp.zeros(acc_s[p].shape, F32)

    def step(p, kc, va, vb):
        s = _nt_dot(lhs[p], kc)
        m_old = m_s[p][...]
        m_new = jnp.maximum(m_old, jnp.max(s, axis=-1, keepdims=True))
        alpha = jnp.exp2(m_old - m_new)
        pr = jnp.exp2(s - m_new).astype(BF16)
        acc = acc_s[p]
        acc[:tq] = alpha[:tq] * acc[:tq] + jnp.dot(pr[:tq], va, preferred_element_type=F32)
        acc[tq:] = alpha[tq:] * acc[tq:] + jnp.dot(pr[tq:], vb, preferred_element_type=F32)
        m_s[p][...] = m_new

    def all_pairs(k_ref, v_ref, rows):
        if kv_grouped:
            kc, vc = k_ref[rows, :], v_ref[rows, :]
            vals = [_values_with_ones(vc, h) for h in range(GQA_KV_HEADS)]
        for p in range(n_pair):
            if kv_grouped:
                step(p, kc, vals[halves[p][0]], vals[halves[p][1]])
            else:
                cols = slice(LANES * p, LANES * (p + 1))
                vc = v_ref[rows, cols]
                step(p, k_ref[rows, cols], _values_with_ones(vc, 0), _values_with_ones(vc, 1))

    for (k_ref, v_ref), ck in zip(srcs, chunks):
        n_chunk = k_ref.shape[0] // ck
        if n_chunk == 1:
            all_pairs(k_ref, v_ref, slice(None))
        else:
            def loop(i, carry, k_ref=k_ref, v_ref=v_ref, ck=ck):
                all_pairs(k_ref, v_ref, pl.ds(pl.multiple_of(i * ck, ck), ck))
                return carry
            lax.fori_loop(0, n_chunk, loop, 0)

    for p in range(n_pair):
        cols = slice(LANES * p, LANES * (p + 1))
        heads = []
        for hd, half in enumerate(halves[p]):
            acc = acc_s[p][hd * tq:(hd + 1) * tq]
            ones_lane = HEAD_DIM * (1 - half)
            o = acc / acc[:, ones_lane:ones_lane + 1]
            heads.append(o if half == hd else pltpu.roll(o, HEAD_DIM, 1))
        o = jnp.where(low_q, heads[0], heads[1])
        o_ref[:, cols] = (o * _silu(gate_ref[:, cols].astype(F32))).astype(BF16)


def _key_chunk(tk):
    for mult in (2 * LANES, LANES, 16):
        try:
            return _tile(tk, 1024, mult)
        except ValueError:
            pass
    raise ValueError(tk)


def _dense_attn(q, q_off, gate, gate_off, kv_srcs, kv_grouped, q_scale):
    t = q.shape[0]
    tq = _tile(t, 256, 16)
    kw = KV_W if kv_grouped else BRANCH_W
    in_specs = [_slab(tq, BRANCH_W, q_off), _slab(tq, BRANCH_W, gate_off)]
    args = [q, gate]
    chunks = []
    for k_arr, k_off, v_arr, v_off in kv_srcs:
        tk = k_arr.shape[0]
        assert k_off % kw == 0 and v_off % kw == 0
        in_specs += [_resident((tk, kw), lambda i, _c=k_off // kw: (0, _c)),
                     _resident((tk, kw), lambda i, _c=v_off // kw: (0, _c))]
        args += [k_arr, v_arr]
        chunks.append(_key_chunk(tk))
    n_pair = N_HEADS // 2
    return pl.pallas_call(
        functools.partial(_dense_attn_body, n_src=len(kv_srcs), chunks=tuple(chunks),
                          kv_grouped=kv_grouped, q_scale=q_scale, tq=tq),
        grid=(t // tq,),
        in_specs=in_specs,
        out_specs=pl.BlockSpec((tq, BRANCH_W), lambda i: (i, 0)),
        out_shape=jax.ShapeDtypeStruct((t, BRANCH_W), BF16),
        scratch_shapes=[pltpu.VMEM((2 * tq, 1), F32)] * n_pair + [pltpu.VMEM((2 * tq, LANES), F32)] * n_pair,
        compiler_params=_params(dimension_semantics=("arbitrary",)),
        name="dense_attn",
    )(*args)


def _gqa_body(q_ref, qn_ref, gate_ref, k_ref, vt_ref, kx_ref, vtx_ref, o_ref, *scratch, tq, n_chunk):
    n_pair = N_HEADS // 2
    qt_s, qtn_s, m_s, acc_s, sx_s = (scratch[i * n_pair:(i + 1) * n_pair] for i in range(5))
    s_s = [scratch[5 * n_pair + 2 * p:5 * n_pair + 2 * p + 2] for p in range(n_pair)]
    mx_s = [scratch[7 * n_pair + 3 * p:7 * n_pair + 3 * p + 3] for p in range(n_pair)]
    s_s = [[(s_s[p][slot], mx_s[p][slot]) for slot in range(2)] for p in range(n_pair)]
    sx_s = [(sx_s[p], mx_s[p][2]) for p in range(n_pair)]
    low_q = lax.broadcasted_iota(jnp.int32, (tq, LANES), 1) < HEAD_DIM
    kv_of = [(2 * p) // (N_HEADS // GQA_KV_HEADS) for p in range(n_pair)]

    def scores(p, keys, buf, qt_ref=None):
        s = jnp.dot(keys, (qt_s[p] if qt_ref is None else qt_ref)[...], preferred_element_type=F32)
        buf[0][...] = s
        buf[1][...] = jnp.max(s, axis=0, keepdims=True)

    def consume(p, buf, vt):
        m_old = m_s[p][...]
        m_new = jnp.maximum(m_old, buf[1][...])
        alpha = jnp.exp2(m_old - m_new)
        pt = jnp.exp2(buf[0][...] - m_new).astype(BF16)
        for hd in range(2):
            cols = slice(hd * tq, (hd + 1) * tq)
            acc_s[p][hd] = alpha[:, cols] * acc_s[p][hd] + jnp.dot(vt, pt[:, cols], preferred_element_type=F32)
        m_s[p][...] = m_new

    def step(i, slot):
        for p in range(n_pair):
            scores(p, k_ref[i + 1], s_s[p][1 - slot])
            consume(p, s_s[p][slot], vt_ref[i, kv_of[p]])

    def transposed_queries(p, q_tile_ref):
        qp = q_tile_ref[:, LANES * p:LANES * (p + 1)].astype(F32)
        qa, qb = _pair_queries(qp, low_q, kv_of[p])
        return jnp.concatenate([qa, qb], axis=0).T.astype(BF16)

    is_first = pl.program_id(0) == 0

    @pl.when(is_first)
    def _():
        for p in range(n_pair):
            qt_s[p][...] = transposed_queries(p, q_ref)
            scores(p, k_ref[0], s_s[p][0])

    @pl.when(jnp.logical_not(is_first))
    def _():
        for p in range(n_pair):
            qt_s[p][...] = qtn_s[p][...]

    for p in range(n_pair):
        m_s[p][...] = jnp.full(m_s[p].shape, NEG_BIG, F32)
        acc_s[p][...] = jnp.zeros(acc_s[p].shape, F32)

    def two_steps(j, carry):
        step(2 * j, 0)
        step(2 * j + 1, 1)
        return carry

    n_step = n_chunk - 1
    lax.fori_loop(0, n_step // 2, two_steps, 0)
    last = n_chunk - 1
    if n_step % 2:
        step(last - 1, 0)
    def next_scores(p):
        scores(p, k_ref[0], s_s[p][0], qtn_s[p])

    for p in range(n_pair):
        qtn_s[p][...] = transposed_queries(p, qn_ref)
    for p in range(n_pair):
        scores(p, kx_ref[...], sx_s[p])
        if last % 2:
            next_scores(p)
        consume(p, s_s[p][last % 2], vt_ref[last, kv_of[p]])
    for p in range(n_pair):
        if not last % 2:
            next_scores(p)
        consume(p, sx_s[p], vtx_ref[0, kv_of[p]])
        acc = acc_s[p]
        ot = jnp.concatenate([acc[hd, :HEAD_DIM] / acc[hd, HEAD_DIM:HEAD_DIM + 1] for hd in range(2)], axis=0)
        cols = slice(LANES * p, LANES * (p + 1))
        o_ref[:, cols] = (ot.T * _silu(gate_ref[:, cols].astype(F32))).astype(BF16)


def _gqa_attn(q, gate, gate_off, k, vt, k_extra, vt_extra):
    t = q.shape[0]
    n_chunk, _, _, ck = vt.shape
    tx = k_extra.shape[0]
    assert vt_extra.shape[0] == 1 and k.shape[0] == n_chunk * ck
    tq = _tile(t, 256, LANES)
    n_pair = N_HEADS // 2
    scratch = ([pltpu.VMEM((LANES, 2 * tq), BF16)] * (2 * n_pair) + [pltpu.VMEM((1, 2 * tq), F32)] * n_pair
               + [pltpu.VMEM((2, VT_ROWS, tq), F32)] * n_pair + [pltpu.VMEM((tx, 2 * tq), F32)] * n_pair
               + [pltpu.VMEM((ck, 2 * tq), F32)] * (2 * n_pair) + [pltpu.VMEM((1, 2 * tq), F32)] * (3 * n_pair))
    return pl.pallas_call(
        functools.partial(_gqa_body, tq=tq, n_chunk=n_chunk),
        grid=(t // tq,),
        in_specs=[pl.BlockSpec((tq, BRANCH_W), lambda i: (i, 0)),
                  pl.BlockSpec((tq, BRANCH_W), lambda i: (jnp.minimum(i + 1, t // tq - 1), 0)),
                  _slab(tq, BRANCH_W, gate_off),
                  _resident((n_chunk, ck, KV_W), lambda i: (0, 0, 0)),
                  _resident(vt.shape, lambda i: (0, 0, 0, 0)),
                  _resident((tx, KV_W), lambda i: (0, 0)),
                  _resident(vt_extra.shape, lambda i: (0, 0, 0, 0))],
        out_specs=pl.BlockSpec((tq, BRANCH_W), lambda i: (i, 0)),
        out_shape=jax.ShapeDtypeStruct((t, BRANCH_W), BF16),
        scratch_shapes=scratch,
        compiler_params=_params(dimension_semantics=("arbitrary",)),
        name="gqa_attn",
    )(q, q, gate, k.reshape(n_chunk, ck, KV_W), vt, k_extra, vt_extra)


def _na_bias_tiles(rpb):
    cq = np.arange(GRID_W)
    col_start = np.clip(cq - NA_WIN_COLS // 2, 0, GRID_W - NA_WIN_COLS)
    col_ok = (cq[None, :] >= col_start[:, None]) & (cq[None, :] < col_start[:, None] + NA_WIN_COLS)
    col_off = np.clip(cq[None, :] - cq[:, None], -(NA_WIN_COLS - 1), NA_WIN_COLS - 1) + NA_WIN_COLS - 1
    onehot = jnp.asarray(col_off[:, :, None] == np.arange(2 * NA_WIN_COLS - 1), F32)
    tiles = jnp.einsum("lhrm,qkm->lhrqk", rpb.astype(F32), onehot, precision=lax.Precision.HIGHEST)
    tiles = jnp.where(col_ok, tiles * LOG2_E, NEG_BIG)
    masked = jnp.full(tiles.shape[:2] + (1, GRID_W, GRID_W), NEG_BIG, F32)
    tiles = jnp.concatenate([tiles, masked], axis=2)
    return jnp.concatenate([tiles, tiles], axis=-1)


def _na_body(q_ref, gate_ref, k_ref, v_ref, kc_ref, vc_ref, tiles_ref, o_ref, *, n_rows):
    rb, band = NA_ROW_BLOCK, NA_BAND_ROWS
    tq, nk = rb * GRID_W, band * GRID_W
    b = pl.program_id(0)
    u0 = jnp.clip(b * rb - NA_WIN_ROWS // 2, 0, n_rows - band)
    rows = pl.ds(pl.multiple_of(u0 * GRID_W, GRID_W), nk)
    low_q = lax.broadcasted_iota(jnp.int32, (tq, LANES), 1) < HEAD_DIM
    low_t = lax.broadcasted_iota(jnp.int32, (GRID_W, LANES), 1) < GRID_W

    def bias_rows(h, i):
        r = b * rb + i
        row_start = jnp.clip(r - NA_WIN_ROWS // 2, 0, n_rows - NA_WIN_ROWS)
        pieces = []
        for j in range(band):
            key_row = u0 + j
            in_window = jnp.logical_and(key_row >= row_start, key_row < row_start + NA_WIN_ROWS)
            ro = jnp.where(in_window, key_row - r + NA_WIN_ROWS - 1, 2 * NA_WIN_ROWS - 1)
            pieces.append(tiles_ref[0, h, ro])
        return jnp.concatenate([jnp.where(low_t, pieces[j], pieces[j + 1]) for j in range(0, band, 2)], axis=1)

    for p in range(N_HEADS // 2):
        bias = jnp.concatenate([bias_rows(2 * p + hd, i) for hd in range(2) for i in range(rb)], axis=0)
        cols = slice(LANES * p, LANES * (p + 1))
        qp = q_ref[:, cols].astype(F32) * SCORE_SCALE_LOG2
        qa, qb = _pair_queries(qp, low_q, None)
        lhs = jnp.concatenate([qa, qb], axis=0).astype(BF16)
        kb, vb = k_ref[rows, cols], v_ref[rows, cols]
        kc, vc = kc_ref[:, cols], vc_ref[:, cols]
        s_band = _nt_dot(lhs, kb) + bias
        s_ctx = _nt_dot(lhs, kc)
        m = jnp.maximum(jnp.max(s_band, axis=-1, keepdims=True), jnp.max(s_ctx, axis=-1, keepdims=True))
        pb = jnp.exp2(s_band - m).astype(BF16)
        pc = jnp.exp2(s_ctx - m).astype(BF16)
        heads = []
        for hd in range(2):
            r = slice(hd * tq, (hd + 1) * tq)
            acc = (jnp.dot(pb[r], _values_with_ones(vb, hd), preferred_element_type=F32)
                   + jnp.dot(pc[r], _values_with_ones(vc, hd), preferred_element_type=F32))
            ones_lane = HEAD_DIM * (1 - hd)
            heads.append(acc / acc[:, ones_lane:ones_lane + 1])
        o = jnp.where(low_q, heads[0], heads[1])
        o_ref[:, cols] = (o * _silu(gate_ref[:, cols].astype(F32))).astype(BF16)


def _na_attn(p, off, pc, bias, layer):
    s = p.shape[0]
    cn = pc.shape[0]
    n_rows = s // GRID_W
    tq = NA_ROW_BLOCK * GRID_W
    assert n_rows % NA_ROW_BLOCK == 0 and n_rows >= NA_BAND_ROWS >= NA_ROW_BLOCK + NA_WIN_ROWS - 1
    assert NA_BAND_ROWS % 2 == 0
    nb = n_rows // NA_ROW_BLOCK
    w = BRANCH_W
    return pl.pallas_call(
        functools.partial(_na_body, n_rows=n_rows),
        grid=(nb,),
        in_specs=[_slab(tq, w, off["a_q"]), _slab(tq, w, off["a_gate"]),
                  _resident((s, w), lambda i, _c=off["a_k"] // w: (0, _c)),
                  _resident((s, w), lambda i, _c=off["a_v"] // w: (0, _c)),
                  _resident((cn, w), lambda i, _c=off["a_k"] // w: (0, _c)),
                  _resident((cn, w), lambda i, _c=off["a_v"] // w: (0, _c)),
                  _resident((1,) + bias.shape[1:], lambda i: (layer, 0, 0, 0, 0))],
        out_specs=pl.BlockSpec((tq, w), lambda i: (i, 0)),
        out_shape=jax.ShapeDtypeStruct((s, w), BF16),
        compiler_params=_params(dimension_semantics=("arbitrary",)),
        name="na_attn",
    )(p, p, p, p, pc, pc, bias)


def _halo_specs(t, tm, width, off):
    assert off % LANES == 0 and tm % HALO == 0
    block = lambda rows: (pl.Element(rows), pl.Element(width))
    per, last = tm // HALO, t // HALO - 1
    return [pl.BlockSpec(block(tm), lambda i: (i * tm, off)),
            pl.BlockSpec(block(HALO), lambda i: (jnp.maximum(i * per - 1, 0) * HALO, off)),
            pl.BlockSpec(block(HALO), lambda i: (jnp.minimum((i + 1) * per, last) * HALO, off))]


def _fill_padded(pad_ref, cur, prev, nxt, tm):
    i, n = pl.program_id(0), pl.num_programs(0)
    pad_ref[0:HALO] = jnp.where(i > 0, prev, jnp.zeros_like(prev))
    pad_ref[HALO:HALO + tm] = cur
    pad_ref[HALO + tm:] = jnp.where(i < n - 1, nxt, jnp.zeros_like(nxt))


def _pool_body(u_ref, up_ref, un_ref, gate_ref, w_ref, sc_ref, o_ref, pad_ref, *, tm, seq):
    _fill_padded(pad_ref, u_ref[...].astype(F32), up_ref[...].astype(F32), un_ref[...].astype(F32), tm)
    t = pl.program_id(0) * tm + lax.broadcasted_iota(jnp.int32, (tm, POOL_GROUP), 0)
    for gi, ksz in enumerate(POOL_SIZES):
        cols = slice(POOL_GROUP * gi, POOL_GROUP * (gi + 1))
        back = ksz // 2
        tot = pad_ref[HALO - back:HALO - back + tm, cols]
        for d in range(1 - back, ksz - back):
            tot = tot + pad_ref[HALO + d:HALO + d + tm, cols]
        lo = jnp.maximum(t - back, 0)
        hi = jnp.minimum(t + (ksz - 1 - back), seq - 1)
        mean = tot / (hi - lo + 1).astype(F32)
        dlt = (mean - pad_ref[HALO:HALO + tm, cols]).astype(BF16)
        y = jnp.dot(dlt, w_ref[0, gi].astype(BF16), preferred_element_type=F32) * sc_ref[:, cols]
        o_ref[:, cols] = (y * _silu(gate_ref[:, cols].astype(F32))).astype(BF16)


def _pool(p, off, w_pool, layer, pool_scale):
    t = p.shape[0]
    tm = _tile(t, 512, HALO)
    w = BRANCH_W
    return pl.pallas_call(
        functools.partial(_pool_body, tm=tm, seq=t),
        grid=(t // tm,),
        in_specs=_halo_specs(t, tm, w, off["b_in"]) + [
            _slab(tm, w, off["b_gate"]),
            pl.BlockSpec((1,) + w_pool.shape[1:], lambda i: (layer, 0, 0, 0)),
            pl.BlockSpec((1, w), lambda i: (0, 0))],
        out_specs=pl.BlockSpec((tm, w), lambda i: (i, 0)),
        out_shape=jax.ShapeDtypeStruct((t, w), BF16),
        scratch_shapes=[pltpu.VMEM((tm + 2 * HALO, w), F32)],
        compiler_params=_params(dimension_semantics=("arbitrary",)),
        name="pool",
    )(p, p, p, p, w_pool, pool_scale)


def _glu(x):
    x = x.astype(F32)
    return x[:, :BRANCH_W] * jax.nn.sigmoid(x[:, BRANCH_W:])


def _conv_body(x_ref, xp_ref, xn_ref, gate_ref, cw_ref, cb_ref, lg_ref, lb_ref, pw_ref, o_ref, pad_ref, sh_ref,
               *, tm):
    _fill_padded(pad_ref, _glu(x_ref[...]), _glu(xp_ref[...]), _glu(xn_ref[...]), tm)
    span = sh_ref.shape[1]
    for b in range(1, SUBLANES):
        sh_ref[b - 1] = pad_ref[b:b + span, :]
    reach = CONV_WIDTH // 2
    y = jnp.zeros((tm, BRANCH_W), F32) + cb_ref[...]
    for j in range(CONV_WIDTH):
        a, b = divmod(HALO - reach + j, SUBLANES)
        rows = slice(SUBLANES * a, SUBLANES * a + tm)
        y = y + (pad_ref[rows, :] if b == 0 else sh_ref[b - 1, rows, :]) * cw_ref[j:j + 1, :]
    mu = jnp.mean(y, axis=-1, keepdims=True)
    yc = y - mu
    var = jnp.mean(yc * yc, axis=-1, keepdims=True)
    z = _silu(yc * lax.rsqrt(var + EPS) * lg_ref[...] + lb_ref[...]).astype(BF16)
    out = jnp.dot(z, pw_ref[0].astype(BF16), preferred_element_type=F32)
    o_ref[...] = (out * _silu(gate_ref[...].astype(F32))).astype(BF16)


def _conv(p, off, conv_w, conv_b, ln_g, ln_b, w_pw, layer):
    t = p.shape[0]
    tm = _tile(t, 512, HALO)
    w = BRANCH_W
    vec = pl.BlockSpec((1, w), lambda i: (0, 0))
    return pl.pallas_call(
        functools.partial(_conv_body, tm=tm),
        grid=(t // tm,),
        in_specs=_halo_specs(t, tm, 2 * w, off["d_glu"]) + [
            _slab(tm, w, off["d_gate"]),
            pl.BlockSpec(conv_w.shape, lambda i: (0, 0)), vec, vec, vec,
            pl.BlockSpec((1, w, w), lambda i: (layer, 0, 0))],
        out_specs=pl.BlockSpec((tm, w), lambda i: (i, 0)),
        out_shape=jax.ShapeDtypeStruct((t, w), BF16),
        scratch_shapes=[pltpu.VMEM((tm + 2 * HALO, w), F32),
                        pltpu.VMEM((SUBLANES - 1, tm + 2 * HALO - SUBLANES, w), F32)],
        compiler_params=_params(dimension_semantics=("arbitrary",)),
        name="conv",
    )(p, p, p, p, conv_w, conv_b, ln_g, ln_b, w_pw)


def _cast_once(w_ref, wb_ref):
    @pl.when(pl.program_id(0) == 0)
    def _():
        wb_ref[...] = w_ref[0].astype(BF16)


def _merge_body(m0, m1, m2, m3, o0, o1, o2, o3, w_ref, y_ref, wb_ref):
    _cast_once(w_ref, wb_ref)
    y = None
    for bi, (m_ref, o_ref) in enumerate(((m0, o0), (m1, o1), (m2, o2), (m3, o3))):
        proj = jnp.dot(o_ref[...], wb_ref[bi], preferred_element_type=F32)
        term = (1.0 + jnp.tanh(0.5 * m_ref[...].astype(F32))) * proj
        y = term if y is None else y + term
    y_ref[...] = (0.5 * y).astype(BF16)


def _merge(p, merge_off, outs, w_branch, layer):
    t = p.shape[0]
    d = w_branch.shape[-1]
    tm = _tile(t, 256, 16)
    return pl.pallas_call(
        _merge_body,
        grid=(t // tm,),
        in_specs=[_slab(tm, d, merge_off + bi * d) for bi in range(N_BRANCH)]
        + [pl.BlockSpec((tm, BRANCH_W), lambda i: (i, 0))] * N_BRANCH
        + [_resident((1,) + w_branch.shape[1:], lambda i: (layer, 0, 0, 0))],
        out_specs=pl.BlockSpec((tm, d), lambda i: (i, 0)),
        out_shape=jax.ShapeDtypeStruct((t, d), BF16),
        scratch_shapes=[pltpu.VMEM(w_branch.shape[1:], BF16)],
        compiler_params=_params(dimension_semantics=("arbitrary",)),
        name="merge",
    )(p, p, p, p, *outs, w_branch)


def _out_body(y_ref, w_ref, x_ref, g_ref, gate_ref, *rest):
    *rest, wb_ref = rest
    _cast_once(w_ref, wb_ref)
    y = jnp.dot(y_ref[...], wb_ref[...], preferred_element_type=F32)
    yn = y * lax.rsqrt(jnp.mean(y * y, axis=-1, keepdims=True) + EPS) * g_ref[...]
    x_next = x_ref[...] + gate_ref[...] * yn
    if len(rest) == 1:
        rest[0][...] = x_next
    else:
        gn_ref, sh_ref, sc_ref, o_ref, h_ref = rest
        o_ref[...] = x_next
        h_ref[...] = _modulated(x_next, gn_ref[...], sh_ref[...], sc_ref[...])


def _out_proj(y, w_out, layer, x, g_post, gate, next_mod=None):
    t, d = x.shape
    tm = _tile(t, 256, 16)
    vec = pl.BlockSpec((1, d), lambda i: (0, 0))
    row = pl.BlockSpec((tm, d), lambda i: (i, 0))
    n_next = 0 if next_mod is None else 1
    out = pl.pallas_call(
        _out_body,
        grid=(t // tm,),
        in_specs=[row, _resident((1, d, d), lambda i: (layer, 0, 0)), row, vec, vec] + [vec] * (3 * n_next),
        scratch_shapes=[pltpu.VMEM((d, d), BF16)],
        out_specs=[row] * (1 + n_next),
        out_shape=[jax.ShapeDtypeStruct((t, d), F32)] + [jax.ShapeDtypeStruct((t, d), BF16)] * n_next,
        compiler_params=_params(dimension_semantics=("arbitrary",)),
        name="out_proj",
    )(y, w_out, x, g_post, gate, *(next_mod or ()))
    return out if n_next else (out[0], None)


def kernel(x, c, ctx, c_ctx, w_ada, b_ada, g_pre, g_post, w_in, na_rpb, pool_w, pool_scale,
           q_norm, k_norm, conv_w, conv_b, conv_ln_g, conv_ln_b, conv_pw, w_branch, w_out):
    batch, seq, d = x.shape
    assert batch == 1 and seq % GRID_W == 0
    cn = ctx.shape[1]
    depth = w_ada.shape[0]
    off = _layout(d)

    xs, cs = x[0], ctx[0]
    cvec = jnp.zeros((8, d), F32).at[0].set(c[0]).at[1].set(c_ctx)
    ada = _ada(cvec, w_ada, b_ada)

    ones_bd = jnp.asarray(np.kron(np.eye(N_HEADS), np.ones((HEAD_DIM, HEAD_DIM))), BF16)
    rope_tabs = _rope_tables(seq)
    na_bias = _na_bias_tiles(na_rpb)
    row = lambda v: v.reshape(1, -1)
    qn_all = jnp.tile(q_norm, (1, N_HEADS))
    kn_all = jnp.tile(k_norm, (1, GQA_KV_HEADS))

    mod = lambda l, r: tuple(ada[l, r:r + 1, k * d:(k + 1) * d] for k in range(3))
    h = _modulate(xs, row(g_pre[0]), *mod(0, 0)[:2])
    hc = _modulate(cs, row(g_pre[0]), *mod(0, 1)[:2])
    for l in range(depth):
        last = l == depth - 1
        conv_args = (conv_w[l], row(conv_b[l]), row(conv_ln_g[l]), row(conv_ln_b[l]), conv_pw, l)
        gate, gate_c = mod(l, 0)[2], mod(l, 1)[2]
        next_mod = lambda r: None if last else (row(g_pre[l + 1]),) + mod(l + 1, r)[:2]
        qn, kn = row(qn_all[l]), row(kn_all[l])

        pc = _in_proj(hc, w_in, l, KV_COLS if last else None)
        qc_ctx, kq_ctx, vt_ctx = _qk_prep(pc, None if last else off["c_q"], off["c_k"], off["c_v"],
                                          qn, kn, ones_bd, None)

        p = _in_proj(h, w_in, l)
        qc, kc, vt = _qk_prep(p, off["c_q"], off["c_k"], off["c_v"], qn, kn, ones_bd, rope_tabs)
        o_c = _gqa_attn(qc, p, off["c_gate"], kc, vt, kq_ctx, vt_ctx)
        o_a = _na_attn(p, off, pc, na_bias, l)
        o_b = _pool(p, off, pool_w, l, row(pool_scale[l]))
        o_d = _conv(p, off, *conv_args)
        y = _merge(p, off["merge"], (o_a, o_b, o_c, o_d), w_branch, l)
        xs, h = _out_proj(y, w_out, l, xs, row(g_post[l]), gate, next_mod(0))

        if not last:
            o_a_c = _dense_attn(pc, off["a_q"], pc, off["a_gate"],
                                [(pc, off["a_k"], pc, off["a_v"])], False, SCORE_SCALE_LOG2)
            o_c_c = _dense_attn(qc_ctx, 0, pc, off["c_gate"], [(kq_ctx, 0, pc, off["c_v"])], True, 1.0)
            o_b_c = _pool(pc, off, pool_w, l, row(pool_scale[l]))
            o_d_c = _conv(pc, off, *conv_args)
            y_c = _merge(pc, off["merge"], (o_a_c, o_b_c, o_c_c, o_d_c), w_branch, l)
            cs, hc = _out_proj(y_c, w_out, l, cs, row(g_post[l]), gate_c, next_mod(1))
    return xs[None]
```

```python
import functools

import numpy as np
import jax
import jax.numpy as jnp
from jax import lax
from jax.experimental import pallas as pl
from jax.experimental.pallas import tpu as pltpu

F32 = jnp.float32
BF16 = jnp.bfloat16

GRID_W = 64
HEAD_DIM = 64
BRANCH_W = 512
N_BRANCH = 4
N_HEADS = BRANCH_W // HEAD_DIM
GQA_KV_HEADS = 2
KV_W = GQA_KV_HEADS * HEAD_DIM
NA_WIN_ROWS = 8
NA_WIN_COLS = 16
POOL_SIZES = (2, 4, 8, 16)
POOL_GROUP = BRANCH_W // len(POOL_SIZES)
ROPE_THETA = 10000.0
CONV_WIDTH = 31
EPS = 1e-6

LANES = 128
SUBLANES = 8
HALO = 16
NA_ROW_BLOCK = 4
NA_BAND_ROWS = 12
NEG_BIG = -1e30
LOG2_E = 1.4426950408889634
SCORE_SCALE_LOG2 = HEAD_DIM ** -0.5 * LOG2_E
V7X_VMEM_LIMIT = 52 * 1024 * 1024
IN_PROJ_VMEM_LIMIT = 58 * 1024 * 1024

PARTS = (("a_k", BRANCH_W), ("a_v", BRANCH_W), ("c_k", KV_W), ("c_v", KV_W), ("a_q", BRANCH_W), ("c_q", BRANCH_W),
         ("a_gate", BRANCH_W), ("b_in", BRANCH_W), ("b_gate", BRANCH_W), ("c_gate", BRANCH_W),
         ("d_glu", 2 * BRANCH_W), ("d_gate", BRANCH_W), ("merge", None))
KV_COLS = 2 * BRANCH_W + 2 * KV_W


def _layout(d_model):
    off, o = {}, 0
    for n, w in PARTS:
        off[n] = o
        o += N_BRANCH * d_model if w is None else w
    return off


def _params(vmem_limit=V7X_VMEM_LIMIT, **kw):
    return pltpu.CompilerParams(vmem_limit_bytes=vmem_limit, **kw)


def _tile(n, pref, mult):
    if n <= pref:
        return n
    t = (pref // mult) * mult
    while t >= mult:
        if n % t == 0:
            return t
        t -= mult
    raise ValueError(f"no tile for {n}")


def _resident(block_shape, index_map):
    return pl.BlockSpec(block_shape, index_map, pipeline_mode=pl.Buffered(1))


def _slab(tm, width, off):
    assert off % LANES == 0
    return pl.BlockSpec((pl.Element(tm), pl.Element(width)), lambda i: (i * tm, off))


def _silu(x):
    return x * jax.nn.sigmoid(x)


def _ada_body(cs_ref, w_ref, b_ref, o_ref):
    s = _silu(cs_ref[...]).astype(BF16)
    o_ref[0] = jnp.dot(s, w_ref[0].astype(BF16), preferred_element_type=F32) + b_ref[0]


def _ada(cs, w_ada, b_ada):
    depth, d, n = w_ada.shape
    tn = _tile(n, 1024, LANES)
    return pl.pallas_call(
        _ada_body,
        grid=(depth, n // tn),
        in_specs=[pl.BlockSpec((8, d), lambda l, j: (0, 0)),
                  pl.BlockSpec((1, d, tn), lambda l, j: (l, 0, j)),
                  pl.BlockSpec((1, 1, tn), lambda l, j: (l, 0, j))],
        out_specs=pl.BlockSpec((1, 8, tn), lambda l, j: (l, 0, j)),
        out_shape=jax.ShapeDtypeStruct((depth, 8, n), F32),
        compiler_params=_params(dimension_semantics=("arbitrary", "arbitrary")),
        name="ada",
    )(cs, w_ada, b_ada.reshape(depth, 1, n))


def _modulated(x, g, shift, scale):
    y = x * lax.rsqrt(jnp.mean(x * x, axis=-1, keepdims=True) + EPS) * g
    return (y * (1.0 + scale) + shift).astype(BF16)


def _modulate_body(x_ref, g_ref, sh_ref, sc_ref, o_ref):
    o_ref[...] = _modulated(x_ref[...], g_ref[...], sh_ref[...], sc_ref[...])


def _modulate(x, g, shift, scale):
    t, d = x.shape
    tm = _tile(t, 512, 16)
    vec = pl.BlockSpec((1, d), lambda i: (0, 0))
    return pl.pallas_call(
        _modulate_body,
        grid=(t // tm,),
        in_specs=[pl.BlockSpec((tm, d), lambda i: (i, 0)), vec, vec, vec],
        out_specs=pl.BlockSpec((tm, d), lambda i: (i, 0)),
        out_shape=jax.ShapeDtypeStruct((t, d), BF16),
        compiler_params=_params(dimension_semantics=("arbitrary",)),
        name="modulate",
    )(x, g, shift, scale)


def _in_proj_body(h_ref, w_ref, o_ref, wb_ref):
    @pl.when(pl.program_id(1) == 0)
    def _():
        wb_ref[...] = w_ref[0].astype(BF16)

    half = h_ref.shape[0] // 2
    for r in (slice(0, half), slice(half, 2 * half)):
        o_ref[r, :] = jnp.dot(h_ref[r, :], wb_ref[...], preferred_element_type=F32).astype(o_ref.dtype)


def _col_tile(n):
    for mult in (2 * LANES, LANES):
        try:
            return _tile(n, 1536, mult)
        except ValueError:
            pass
    raise ValueError(n)


def _in_proj(h, w_in, layer, n_cols=None):
    t, d = h.shape
    n = w_in.shape[2]
    tm = _tile(t, 2048, 16)
    tn = _col_tile(n)
    n_tiles = n // tn if n_cols is None else pl.cdiv(n_cols, tn)
    return pl.pallas_call(
        _in_proj_body,
        grid=(n_tiles, t // tm),
        in_specs=[pl.BlockSpec((tm, d), lambda j, i: (i, 0)),
                  pl.BlockSpec((1, d, tn), lambda j, i: (layer, 0, j))],
        out_specs=pl.BlockSpec((tm, tn), lambda j, i: (i, j)),
        out_shape=jax.ShapeDtypeStruct((t, n_tiles * tn), BF16),
        scratch_shapes=[pltpu.VMEM((d, tn), BF16)],
        compiler_params=_params(vmem_limit=IN_PROJ_VMEM_LIMIT, dimension_semantics=("arbitrary", "arbitrary")),
        name="in_proj",
    )(h, w_in)


def _head_meansq(x, ones_bd):
    ss = x * x
    hi = ss.astype(BF16)
    lo = (ss - hi.astype(F32)).astype(BF16)
    tot = (jnp.dot(hi, ones_bd, preferred_element_type=F32)
           + jnp.dot(lo, ones_bd, preferred_element_type=F32))
    return tot * (1.0 / HEAD_DIM)


def _rope(y, cos, sin_signed):
    w = y.shape[-1]
    lane = lax.broadcasted_iota(jnp.int32, y.shape, 1)
    nxt = pltpu.roll(y, w - 16, 1)
    prv = pltpu.roll(y, 16, 1)
    return y * cos + jnp.where((lane % 32) < 16, nxt, prv) * sin_signed


def _norm_rope(x, w, bd, rope):
    x = x * lax.rsqrt(_head_meansq(x, bd) + EPS) * w
    if rope is not None:
        reps = x.shape[-1] // LANES
        x = _rope(x, jnp.concatenate([rope[0]] * reps, axis=1), jnp.concatenate([rope[1]] * reps, axis=1))
    return x


VT_ROWS = HEAD_DIM + 16


def _qk_prep_body(*refs, use_rope, with_q):
    refs = list(refs)
    q_ref = refs.pop(0) if with_q else None
    k_ref, v_ref = refs.pop(0), refs.pop(0)
    qn_ref = refs.pop(0) if with_q else None
    kn_ref, bd_ref = refs.pop(0), refs.pop(0)
    rope = (refs.pop(0)[...], refs.pop(0)[...]) if use_rope else None
    bd = bd_ref[...]
    if with_q:
        q = _norm_rope(q_ref[...].astype(F32), qn_ref[...], bd, rope)
        refs.pop(0)[...] = (q * SCORE_SCALE_LOG2).astype(BF16)
    k = _norm_rope(k_ref[...].astype(F32), kn_ref[...], bd[:KV_W, :KV_W], rope)
    refs.pop(0)[...] = k.astype(BF16)
    vt_ref = refs.pop(0)
    vt = v_ref[...].astype(F32).T
    row = lax.broadcasted_iota(jnp.int32, (VT_ROWS - HEAD_DIM, vt.shape[1]), 0)
    tail = jnp.where(row == 0, 1.0, 0.0)
    for g in range(GQA_KV_HEADS):
        vt_ref[0, g] = jnp.concatenate([vt[HEAD_DIM * g:HEAD_DIM * (g + 1)], tail], axis=0).astype(BF16)


def _qk_prep(p, q_off, k_off, v_off, qn, kn, ones_bd, rope_tabs):
    t = p.shape[0]
    tm = _tile(t, 1024, LANES)
    use_rope, with_q = rope_tabs is not None, q_off is not None
    const = lambda shape: pl.BlockSpec(shape, lambda i: (0, 0))
    in_specs, args, out_specs, out_shape = [], [], [], []
    if with_q:
        in_specs.append(_slab(tm, BRANCH_W, q_off))
        args.append(p)
    in_specs += [_slab(tm, KV_W, k_off), _slab(tm, KV_W, v_off)]
    args += [p, p]
    if with_q:
        in_specs.append(const((1, BRANCH_W)))
        args.append(qn)
        out_specs.append(pl.BlockSpec((tm, BRANCH_W), lambda i: (i, 0)))
        out_shape.append(jax.ShapeDtypeStruct((t, BRANCH_W), BF16))
    in_specs += [const((1, KV_W)), const((BRANCH_W, BRANCH_W))]
    args += [kn, ones_bd]
    if use_rope:
        in_specs += [pl.BlockSpec((tm, LANES), lambda i: (i, 0))] * 2
        args += list(rope_tabs)
    out_specs += [pl.BlockSpec((tm, KV_W), lambda i: (i, 0)),
                  pl.BlockSpec((1, GQA_KV_HEADS, VT_ROWS, tm), lambda i: (i, 0, 0, 0))]
    out_shape += [jax.ShapeDtypeStruct((t, KV_W), BF16),
                  jax.ShapeDtypeStruct((t // tm, GQA_KV_HEADS, VT_ROWS, tm), BF16)]
    res = pl.pallas_call(
        functools.partial(_qk_prep_body, use_rope=use_rope, with_q=with_q),
        grid=(t // tm,),
        in_specs=in_specs,
        out_specs=out_specs,
        out_shape=out_shape,
        compiler_params=_params(dimension_semantics=("arbitrary",)),
        name="qk_prep",
    )(*args)
    return res if with_q else (None, res[0], res[1])


def _rope_tables(seq):
    half = HEAD_DIM // 4
    t = lax.broadcasted_iota(jnp.int32, (seq, LANES), 0)
    lane = lax.broadcasted_iota(jnp.int32, (seq, LANES), 1)
    pos = jnp.where((lane % HEAD_DIM) < 2 * half, t // GRID_W, t % GRID_W).astype(F32)
    freqs = ROPE_THETA ** (-(lane % half).astype(F32) / half)
    ang = pos * freqs
    sign = jnp.where((lane % (2 * half)) < half, -1.0, 1.0)
    return jnp.cos(ang), jnp.sin(ang) * sign


def _nt_dot(a, b):
    return lax.dot_general(a, b, (((1,), (1,)), ((), ())), preferred_element_type=F32)


def _pair_queries(qp, low, shared_kv_lanes):
    zero = jnp.zeros_like(qp)
    if shared_kv_lanes is None:
        return jnp.where(low, qp, zero), jnp.where(low, zero, qp)
    qr = pltpu.roll(qp, HEAD_DIM, 1)
    if shared_kv_lanes == 0:
        return jnp.where(low, qp, zero), jnp.where(low, qr, zero)
    return jnp.where(low, zero, qr), jnp.where(low, zero, qp)


def _values_with_ones(vc, half):
    lane = lax.broadcasted_iota(jnp.int32, vc.shape, 1)
    keep = (lane < HEAD_DIM) if half == 0 else (lane >= HEAD_DIM)
    ones_lane = HEAD_DIM * (1 - half)
    fill = jnp.where(lane == ones_lane, 1.0, 0.0)
    return jnp.where(keep, vc.astype(F32), fill).astype(vc.dtype)


def _dense_attn_body(*refs, n_src, chunks, kv_grouped, q_scale, tq):
    n_pair = N_HEADS // 2
    q_ref, gate_ref = refs[0], refs[1]
    srcs = [(refs[2 + 2 * i], refs[3 + 2 * i]) for i in range(n_src)]
    o_ref = refs[2 + 2 * n_src]
    scratch = refs[3 + 2 * n_src:]
    m_s, acc_s = scratch[:n_pair], scratch[n_pair:]
    low_q = lax.broadcasted_iota(jnp.int32, (tq, LANES), 1) < HEAD_DIM

    halves = [((2 * p) // (N_HEADS // GQA_KV_HEADS),) * 2 if kv_grouped else (0, 1) for p in range(n_pair)]
    lhs = []
    for p in range(n_pair):
        qp = q_ref[:, LANES * p:LANES * (p + 1)].astype(F32) * q_scale
        qa, qb = _pair_queries(qp, low_q, halves[p][0] if kv_grouped else None)
        lhs.append(jnp.concatenate([qa, qb], axis=0).astype(BF16))
        m_s[p][...] = jnp.full(m_s[p].shape, NEG_BIG, F32)
        acc_s[p][...] = jnp.zeros(acc_s[p].shape, F32)

    def step(p, kc, va, vb):
        s = _nt_dot(lhs[p], kc)
        m_old = m_s[p][...]
        m_new = jnp.maximum(m_old, jnp.max(s, axis=-1, keepdims=True))
        alpha = jnp.exp2(m_old - m_new)
        pr = jnp.exp2(s - m_new).astype(BF16)
        acc = acc_s[p]
        acc[:tq] = alpha[:tq] * acc[:tq] + jnp.dot(pr[:tq], va, preferred_element_type=F32)
        acc[tq:] = alpha[tq:] * acc[tq:] + jnp.dot(pr[tq:], vb, preferred_element_type=F32)
        m_s[p][...] = m_new

    def all_pairs(k_ref, v_ref, rows):
        if kv_grouped:
            kc, vc = k_ref[rows, :], v_ref[rows, :]
            vals = [_values_with_ones(vc, h) for h in range(GQA_KV_HEADS)]
        for p in range(n_pair):
            if kv_grouped:
                step(p, kc, vals[halves[p][0]], vals[halves[p][1]])
            else:
                cols = slice(LANES * p, LANES * (p + 1))
                vc = v_ref[rows, cols]
                step(p, k_ref[rows, cols], _values_with_ones(vc, 0), _values_with_ones(vc, 1))

    for (k_ref, v_ref), ck in zip(srcs, chunks):
        n_chunk = k_ref.shape[0] // ck
        if n_chunk == 1:
            all_pairs(k_ref, v_ref, slice(None))
        else:
            def loop(i, carry, k_ref=k_ref, v_ref=v_ref, ck=ck):
                all_pairs(k_ref, v_ref, pl.ds(pl.multiple_of(i * ck, ck), ck))
                return carry
            lax.fori_loop(0, n_chunk, loop, 0)

    for p in range(n_pair):
        cols = slice(LANES * p, LANES * (p + 1))
        heads = []
        for hd, half in enumerate(halves[p]):
            acc = acc_s[p][hd * tq:(hd + 1) * tq]
            ones_lane = HEAD_DIM * (1 - half)
            o = acc / acc[:, ones_lane:ones_lane + 1]
            heads.append(o if half == hd else pltpu.roll(o, HEAD_DIM, 1))
        o = jnp.where(low_q, heads[0], heads[1])
        o_ref[:, cols] = (o * _silu(gate_ref[:, cols].astype(F32))).astype(BF16)


def _key_chunk(tk):
    for mult in (2 * LANES, LANES, 16):
        try:
            return _tile(tk, 1024, mult)
        except ValueError:
            pass
    raise ValueError(tk)


def _dense_attn(q, q_off, gate, gate_off, kv_srcs, kv_grouped, q_scale):
    t = q.shape[0]
    tq = _tile(t, 256, 16)
    kw = KV_W if kv_grouped else BRANCH_W
    in_specs = [_slab(tq, BRANCH_W, q_off), _slab(tq, BRANCH_W, gate_off)]
    args = [q, gate]
    chunks = []
    for k_arr, k_off, v_arr, v_off in kv_srcs:
        tk = k_arr.shape[0]
        assert k_off % kw == 0 and v_off % kw == 0
        in_specs += [_resident((tk, kw), lambda i, _c=k_off // kw: (0, _c)),
                     _resident((tk, kw), lambda i, _c=v_off // kw: (0, _c))]
        args += [k_arr, v_arr]
        chunks.append(_key_chunk(tk))
    n_pair = N_HEADS // 2
    return pl.pallas_call(
        functools.partial(_dense_attn_body, n_src=len(kv_srcs), chunks=tuple(chunks),
                          kv_grouped=kv_grouped, q_scale=q_scale, tq=tq),
        grid=(t // tq,),
        in_specs=in_specs,
        out_specs=pl.BlockSpec((tq, BRANCH_W), lambda i: (i, 0)),
        out_shape=jax.ShapeDtypeStruct((t, BRANCH_W), BF16),
        scratch_shapes=[pltpu.VMEM((2 * tq, 1), F32)] * n_pair + [pltpu.VMEM((2 * tq, LANES), F32)] * n_pair,
        compiler_params=_params(dimension_semantics=("arbitrary",)),
        name="dense_attn",
    )(*args)


def _gqa_body(q_ref, qn_ref, gate_ref, k_ref, vt_ref, kx_ref, vtx_ref, o_ref, *scratch, tq, n_chunk):
    n_pair = N_HEADS // 2
    qt_s, qtn_s, m_s, acc_s, sx_s = (scratch[i * n_pair:(i + 1) * n_pair] for i in range(5))
    s_s = [scratch[5 * n_pair + 2 * p:5 * n_pair + 2 * p + 2] for p in range(n_pair)]
    mx_s = [scratch[7 * n_pair + 3 * p:7 * n_pair + 3 * p + 3] for p in range(n_pair)]
    s_s = [[(s_s[p][slot], mx_s[p][slot]) for slot in range(2)] for p in range(n_pair)]
    sx_s = [(sx_s[p], mx_s[p][2]) for p in range(n_pair)]
    low_q = lax.broadcasted_iota(jnp.int32, (tq, LANES), 1) < HEAD_DIM
    kv_of = [(2 * p) // (N_HEADS // GQA_KV_HEADS) for p in range(n_pair)]

    def scores(p, keys, buf, qt_ref=None):
        s = jnp.dot(keys, (qt_s[p] if qt_ref is None else qt_ref)[...], preferred_element_type=F32)
        buf[0][...] = s
        buf[1][...] = jnp.max(s, axis=0, keepdims=True)

    def consume(p, buf, vt):
        m_old = m_s[p][...]
        m_new = jnp.maximum(m_old, buf[1][...])
        alpha = jnp.exp2(m_old - m_new)
        pt = jnp.exp2(buf[0][...] - m_new).astype(BF16)
        for hd in range(2):
            cols = slice(hd * tq, (hd + 1) * tq)
            acc_s[p][hd] = alpha[:, cols] * acc_s[p][hd] + jnp.dot(vt, pt[:, cols], preferred_element_type=F32)
        m_s[p][...] = m_new

    def step(i, slot):
        for p in range(n_pair):
            scores(p, k_ref[i + 1], s_s[p][1 - slot])
            consume(p, s_s[p][slot], vt_ref[i, kv_of[p]])

    def transposed_queries(p, q_tile_ref):
        qp = q_tile_ref[:, LANES * p:LANES * (p + 1)].astype(F32)
        qa, qb = _pair_queries(qp, low_q, kv_of[p])
        return jnp.concatenate([qa, qb], axis=0).T.astype(BF16)

    is_first = pl.program_id(0) == 0

    @pl.when(is_first)
    def _():
        for p in range(n_pair):
            qt_s[p][...] = transposed_queries(p, q_ref)
            scores(p, k_ref[0], s_s[p][0])

    @pl.when(jnp.logical_not(is_first))
    def _():
        for p in range(n_pair):
            qt_s[p][...] = qtn_s[p][...]

    for p in range(n_pair):
        m_s[p][...] = jnp.full(m_s[p].shape, NEG_BIG, F32)
        acc_s[p][...] = jnp.zeros(acc_s[p].shape, F32)

    def two_steps(j, carry):
        step(2 * j, 0)
        step(2 * j + 1, 1)
        return carry

    n_step = n_chunk - 1
    lax.fori_loop(0, n_step // 2, two_steps, 0)
    last = n_chunk - 1
    if n_step % 2:
        step(last - 1, 0)
    def next_scores(p):
        scores(p, k_ref[0], s_s[p][0], qtn_s[p])

    for p in range(n_pair):
        qtn_s[p][...] = transposed_queries(p, qn_ref)
    for p in range(n_pair):
        scores(p, kx_ref[...], sx_s[p])
        if last % 2:
            next_scores(p)
        consume(p, s_s[p][last % 2], vt_ref[last, kv_of[p]])
    for p in range(n_pair):
        if not last % 2:
            next_scores(p)
        consume(p, sx_s[p], vtx_ref[0, kv_of[p]])
        acc = acc_s[p]
        ot = jnp.concatenate([acc[hd, :HEAD_DIM] / acc[hd, HEAD_DIM:HEAD_DIM + 1] for hd in range(2)], axis=0)
        cols = slice(LANES * p, LANES * (p + 1))
        o_ref[:, cols] = (ot.T * _silu(gate_ref[:, cols].astype(F32))).astype(BF16)


def _gqa_attn(q, gate, gate_off, k, vt, k_extra, vt_extra):
    t = q.shape[0]
    n_chunk, _, _, ck = vt.shape
    tx = k_extra.shape[0]
    assert vt_extra.shape[0] == 1 and k.shape[0] == n_chunk * ck
    tq = _tile(t, 256, LANES)
    n_pair = N_HEADS // 2
    scratch = ([pltpu.VMEM((LANES, 2 * tq), BF16)] * (2 * n_pair) + [pltpu.VMEM((1, 2 * tq), F32)] * n_pair
               + [pltpu.VMEM((2, VT_ROWS, tq), F32)] * n_pair + [pltpu.VMEM((tx, 2 * tq), F32)] * n_pair
               + [pltpu.VMEM((ck, 2 * tq), F32)] * (2 * n_pair) + [pltpu.VMEM((1, 2 * tq), F32)] * (3 * n_pair))
    return pl.pallas_call(
        functools.partial(_gqa_body, tq=tq, n_chunk=n_chunk),
        grid=(t // tq,),
        in_specs=[pl.BlockSpec((tq, BRANCH_W), lambda i: (i, 0)),
                  pl.BlockSpec((tq, BRANCH_W), lambda i: (jnp.minimum(i + 1, t // tq - 1), 0)),
                  _slab(tq, BRANCH_W, gate_off),
                  _resident((n_chunk, ck, KV_W), lambda i: (0, 0, 0)),
                  _resident(vt.shape, lambda i: (0, 0, 0, 0)),
                  _resident((tx, KV_W), lambda i: (0, 0)),
                  _resident(vt_extra.shape, lambda i: (0, 0, 0, 0))],
        out_specs=pl.BlockSpec((tq, BRANCH_W), lambda i: (i, 0)),
        out_shape=jax.ShapeDtypeStruct((t, BRANCH_W), BF16),
        scratch_shapes=scratch,
        compiler_params=_params(dimension_semantics=("arbitrary",)),
        name="gqa_attn",
    )(q, q, gate, k.reshape(n_chunk, ck, KV_W), vt, k_extra, vt_extra)


def _na_bias_tiles(rpb):
    cq = np.arange(GRID_W)
    col_start = np.clip(cq - NA_WIN_COLS // 2, 0, GRID_W - NA_WIN_COLS)
    col_ok = (cq[None, :] >= col_start[:, None]) & (cq[None, :] < col_start[:, None] + NA_WIN_COLS)
    col_off = np.clip(cq[None, :] - cq[:, None], -(NA_WIN_COLS - 1), NA_WIN_COLS - 1) + NA_WIN_COLS - 1
    onehot = jnp.asarray(col_off[:, :, None] == np.arange(2 * NA_WIN_COLS - 1), F32)
    tiles = jnp.einsum("lhrm,qkm->lhrqk", rpb.astype(F32), onehot, precision=lax.Precision.HIGHEST)
    tiles = jnp.where(col_ok, tiles * LOG2_E, NEG_BIG)
    masked = jnp.full(tiles.shape[:2] + (1, GRID_W, GRID_W), NEG_BIG, F32)
    tiles = jnp.concatenate([tiles, masked], axis=2)
    return jnp.concatenate([tiles, tiles], axis=-1)


def _na_body(q_ref, gate_ref, k_ref, v_ref, kc_ref, vc_ref, tiles_ref, o_ref, *, n_rows):
    rb, band = NA_ROW_BLOCK, NA_BAND_ROWS
    tq, nk = rb * GRID_W, band * GRID_W
    b = pl.program_id(0)
    u0 = jnp.clip(b * rb - NA_WIN_ROWS // 2, 0, n_rows - band)
    rows = pl.ds(pl.multiple_of(u0 * GRID_W, GRID_W), nk)
    low_q = lax.broadcasted_iota(jnp.int32, (tq, LANES), 1) < HEAD_DIM
    low_t = lax.broadcasted_iota(jnp.int32, (GRID_W, LANES), 1) < GRID_W

    def bias_rows(h, i):
        r = b * rb + i
        row_start = jnp.clip(r - NA_WIN_ROWS // 2, 0, n_rows - NA_WIN_ROWS)
        pieces = []
        for j in range(band):
            key_row = u0 + j
            in_window = jnp.logical_and(key_row >= row_start, key_row < row_start + NA_WIN_ROWS)
            ro = jnp.where(in_window, key_row - r + NA_WIN_ROWS - 1, 2 * NA_WIN_ROWS - 1)
            pieces.append(tiles_ref[0, h, ro])
        return jnp.concatenate([jnp.where(low_t, pieces[j], pieces[j + 1]) for j in range(0, band, 2)], axis=1)

    for p in range(N_HEADS // 2):
        bias = jnp.concatenate([bias_rows(2 * p + hd, i) for hd in range(2) for i in range(rb)], axis=0)
        cols = slice(LANES * p, LANES * (p + 1))
        qp = q_ref[:, cols].astype(F32) * SCORE_SCALE_LOG2
        qa, qb = _pair_queries(qp, low_q, None)
        lhs = jnp.concatenate([qa, qb], axis=0).astype(BF16)
        kb, vb = k_ref[rows, cols], v_ref[rows, cols]
        kc, vc = kc_ref[:, cols], vc_ref[:, cols]
        s_band = _nt_dot(lhs, kb) + bias
        s_ctx = _nt_dot(lhs, kc)
        m = jnp.maximum(jnp.max(s_band, axis=-1, keepdims=True), jnp.max(s_ctx, axis=-1, keepdims=True))
        pb = jnp.exp2(s_band - m).astype(BF16)
        pc = jnp.exp2(s_ctx - m).astype(BF16)
        heads = []
        for hd in range(2):
            r = slice(hd * tq, (hd + 1) * tq)
            acc = (jnp.dot(pb[r], _values_with_ones(vb, hd), preferred_element_type=F32)
                   + jnp.dot(pc[r], _values_with_ones(vc, hd), preferred_element_type=F32))
            ones_lane = HEAD_DIM * (1 - hd)
            heads.append(acc / acc[:, ones_lane:ones_lane + 1])
        o = jnp.where(low_q, heads[0], heads[1])
        o_ref[:, cols] = (o * _silu(gate_ref[:, cols].astype(F32))).astype(BF16)


def _na_attn(p, off, pc, bias, layer):
    s = p.shape[0]
    cn = pc.shape[0]
    n_rows = s // GRID_W
    tq = NA_ROW_BLOCK * GRID_W
    assert n_rows % NA_ROW_BLOCK == 0 and n_rows >= NA_BAND_ROWS >= NA_ROW_BLOCK + NA_WIN_ROWS - 1
    assert NA_BAND_ROWS % 2 == 0
    nb = n_rows // NA_ROW_BLOCK
    w = BRANCH_W
    return pl.pallas_call(
        functools.partial(_na_body, n_rows=n_rows),
        grid=(nb,),
        in_specs=[_slab(tq, w, off["a_q"]), _slab(tq, w, off["a_gate"]),
                  _resident((s, w), lambda i, _c=off["a_k"] // w: (0, _c)),
                  _resident((s, w), lambda i, _c=off["a_v"] // w: (0, _c)),
                  _resident((cn, w), lambda i, _c=off["a_k"] // w: (0, _c)),
                  _resident((cn, w), lambda i, _c=off["a_v"] // w: (0, _c)),
                  _resident((1,) + bias.shape[1:], lambda i: (layer, 0, 0, 0, 0))],
        out_specs=pl.BlockSpec((tq, w), lambda i: (i, 0)),
        out_shape=jax.ShapeDtypeStruct((s, w), BF16),
        compiler_params=_params(dimension_semantics=("arbitrary",)),
        name="na_attn",
    )(p, p, p, p, pc, pc, bias)


def _halo_specs(t, tm, width, off):
    assert off % LANES == 0 and tm % HALO == 0
    block = lambda rows: (pl.Element(rows), pl.Element(width))
    per, last = tm // HALO, t // HALO - 1
    return [pl.BlockSpec(block(tm), lambda i: (i * tm, off)),
            pl.BlockSpec(block(HALO), lambda i: (jnp.maximum(i * per - 1, 0) * HALO, off)),
            pl.BlockSpec(block(HALO), lambda i: (jnp.minimum((i + 1) * per, last) * HALO, off))]


def _fill_padded(pad_ref, cur, prev, nxt, tm):
    i, n = pl.program_id(0), pl.num_programs(0)
    pad_ref[0:HALO] = jnp.where(i > 0, prev, jnp.zeros_like(prev))
    pad_ref[HALO:HALO + tm] = cur
    pad_ref[HALO + tm:] = jnp.where(i < n - 1, nxt, jnp.zeros_like(nxt))


def _pool_body(u_ref, up_ref, un_ref, gate_ref, w_ref, sc_ref, o_ref, pad_ref, *, tm, seq):
    _fill_padded(pad_ref, u_ref[...].astype(F32), up_ref[...].astype(F32), un_ref[...].astype(F32), tm)
    t = pl.program_id(0) * tm + lax.broadcasted_iota(jnp.int32, (tm, POOL_GROUP), 0)
    for gi, ksz in enumerate(POOL_SIZES):
        cols = slice(POOL_GROUP * gi, POOL_GROUP * (gi + 1))
        back = ksz // 2
        tot = pad_ref[HALO - back:HALO - back + tm, cols]
        for d in range(1 - back, ksz - back):
            tot = tot + pad_ref[HALO + d:HALO + d + tm, cols]
        lo = jnp.maximum(t - back, 0)
        hi = jnp.minimum(t + (ksz - 1 - back), seq - 1)
        mean = tot / (hi - lo + 1).astype(F32)
        dlt = (mean - pad_ref[HALO:HALO + tm, cols]).astype(BF16)
        y = jnp.dot(dlt, w_ref[0, gi].astype(BF16), preferred_element_type=F32) * sc_ref[:, cols]
        o_ref[:, cols] = (y * _silu(gate_ref[:, cols].astype(F32))).astype(BF16)


def _pool(p, off, w_pool, layer, pool_scale):
    t = p.shape[0]
    tm = _tile(t, 512, HALO)
    w = BRANCH_W
    return pl.pallas_call(
        functools.partial(_pool_body, tm=tm, seq=t),
        grid=(t // tm,),
        in_specs=_halo_specs(t, tm, w, off["b_in"]) + [
            _slab(tm, w, off["b_gate"]),
            pl.BlockSpec((1,) + w_pool.shape[1:], lambda i: (layer, 0, 0, 0)),
            pl.BlockSpec((1, w), lambda i: (0, 0))],
        out_specs=pl.BlockSpec((tm, w), lambda i: (i, 0)),
        out_shape=jax.ShapeDtypeStruct((t, w), BF16),
        scratch_shapes=[pltpu.VMEM((tm + 2 * HALO, w), F32)],
        compiler_params=_params(dimension_semantics=("arbitrary",)),
        name="pool",
    )(p, p, p, p, w_pool, pool_scale)


def _glu(x):
    x = x.astype(F32)
    return x[:, :BRANCH_W] * jax.nn.sigmoid(x[:, BRANCH_W:])


def _conv_body(x_ref, xp_ref, xn_ref, gate_ref, cw_ref, cb_ref, lg_ref, lb_ref, pw_ref, o_ref, pad_ref, sh_ref,
               *, tm):
    _fill_padded(pad_ref, _glu(x_ref[...]), _glu(xp_ref[...]), _glu(xn_ref[...]), tm)
    span = sh_ref.shape[1]
    for b in range(1, SUBLANES):
        sh_ref[b - 1] = pad_ref[b:b + span, :]
    reach = CONV_WIDTH // 2
    y = jnp.zeros((tm, BRANCH_W), F32) + cb_ref[...]
    for j in range(CONV_WIDTH):
        a, b = divmod(HALO - reach + j, SUBLANES)
        rows = slice(SUBLANES * a, SUBLANES * a + tm)
        y = y + (pad_ref[rows, :] if b == 0 else sh_ref[b - 1, rows, :]) * cw_ref[j:j + 1, :]
    mu = jnp.mean(y, axis=-1, keepdims=True)
    yc = y - mu
    var = jnp.mean(yc * yc, axis=-1, keepdims=True)
    z = _silu(yc * lax.rsqrt(var + EPS) * lg_ref[...] + lb_ref[...]).astype(BF16)
    out = jnp.dot(z, pw_ref[0].astype(BF16), preferred_element_type=F32)
    o_ref[...] = (out * _silu(gate_ref[...].astype(F32))).astype(BF16)


def _conv(p, off, conv_w, conv_b, ln_g, ln_b, w_pw, layer):
    t = p.shape[0]
    tm = _tile(t, 512, HALO)
    w = BRANCH_W
    vec = pl.BlockSpec((1, w), lambda i: (0, 0))
    return pl.pallas_call(
        functools.partial(_conv_body, tm=tm),
        grid=(t // tm,),
        in_specs=_halo_specs(t, tm, 2 * w, off["d_glu"]) + [
            _slab(tm, w, off["d_gate"]),
            pl.BlockSpec(conv_w.shape, lambda i: (0, 0)), vec, vec, vec,
            pl.BlockSpec((1, w, w), lambda i: (layer, 0, 0))],
        out_specs=pl.BlockSpec((tm, w), lambda i: (i, 0)),
        out_shape=jax.ShapeDtypeStruct((t, w), BF16),
        scratch_shapes=[pltpu.VMEM((tm + 2 * HALO, w), F32),
                        pltpu.VMEM((SUBLANES - 1, tm + 2 * HALO - SUBLANES, w), F32)],
        compiler_params=_params(dimension_semantics=("arbitrary",)),
        name="conv",
    )(p, p, p, p, conv_w, conv_b, ln_g, ln_b, w_pw)


def _cast_once(w_ref, wb_ref):
    @pl.when(pl.program_id(0) == 0)
    def _():
        wb_ref[...] = w_ref[0].astype(BF16)


def _merge_body(m0, m1, m2, m3, o0, o1, o2, o3, w_ref, y_ref, wb_ref):
    _cast_once(w_ref, wb_ref)
    y = None
    for bi, (m_ref, o_ref) in enumerate(((m0, o0), (m1, o1), (m2, o2), (m3, o3))):
        proj = jnp.dot(o_ref[...], wb_ref[bi], preferred_element_type=F32)
        term = (1.0 + jnp.tanh(0.5 * m_ref[...].astype(F32))) * proj
        y = term if y is None else y + term
    y_ref[...] = (0.5 * y).astype(BF16)


def _merge(p, merge_off, outs, w_branch, layer):
    t = p.shape[0]
    d = w_branch.shape[-1]
    tm = _tile(t, 256, 16)
    return pl.pallas_call(
        _merge_body,
        grid=(t // tm,),
        in_specs=[_slab(tm, d, merge_off + bi * d) for bi in range(N_BRANCH)]
        + [pl.BlockSpec((tm, BRANCH_W), lambda i: (i, 0))] * N_BRANCH
        + [_resident((1,) + w_branch.shape[1:], lambda i: (layer, 0, 0, 0))],
        out_specs=pl.BlockSpec((tm, d), lambda i: (i, 0)),
        out_shape=jax.ShapeDtypeStruct((t, d), BF16),
        scratch_shapes=[pltpu.VMEM(w_branch.shape[1:], BF16)],
        compiler_params=_params(dimension_semantics=("arbitrary",)),
        name="merge",
    )(p, p, p, p, *outs, w_branch)


def _out_body(y_ref, w_ref, x_ref, g_ref, gate_ref, *rest):
    *rest, wb_ref = rest
    _cast_once(w_ref, wb_ref)
    y = jnp.dot(y_ref[...], wb_ref[...], preferred_element_type=F32)
    yn = y * lax.rsqrt(jnp.mean(y * y, axis=-1, keepdims=True) + EPS) * g_ref[...]
    x_next = x_ref[...] + gate_ref[...] * yn
    if len(rest) == 1:
        rest[0][...] = x_next
    else:
        gn_ref, sh_ref, sc_ref, o_ref, h_ref = rest
        o_ref[...] = x_next
        h_ref[...] = _modulated(x_next, gn_ref[...], sh_ref[...], sc_ref[...])


def _out_proj(y, w_out, layer, x, g_post, gate, next_mod=None):
    t, d = x.shape
    tm = _tile(t, 256, 16)
    vec = pl.BlockSpec((1, d), lambda i: (0, 0))
    row = pl.BlockSpec((tm, d), lambda i: (i, 0))
    n_next = 0 if next_mod is None else 1
    out = pl.pallas_call(
        _out_body,
        grid=(t // tm,),
        in_specs=[row, _resident((1, d, d), lambda i: (layer, 0, 0)), row, vec, vec] + [vec] * (3 * n_next),
        scratch_shapes=[pltpu.VMEM((d, d), BF16)],
        out_specs=[row] * (1 + n_next),
        out_shape=[jax.ShapeDtypeStruct((t, d), F32)] + [jax.ShapeDtypeStruct((t, d), BF16)] * n_next,
        compiler_params=_params(dimension_semantics=("arbitrary",)),
        name="out_proj",
    )(y, w_out, x, g_post, gate, *(next_mod or ()))
    return out if n_next else (out[0], None)


def kernel(x, c, ctx, c_ctx, w_ada, b_ada, g_pre, g_post, w_in, na_rpb, pool_w, pool_scale,
           q_norm, k_norm, conv_w, conv_b, conv_ln_g, conv_ln_b, conv_pw, w_branch, w_out):
    batch, seq, d = x.shape
    assert batch == 1 and seq % GRID_W == 0
    cn = ctx.shape[1]
    depth = w_ada.shape[0]
    off = _layout(d)

    xs, cs = x[0], ctx[0]
    cvec = jnp.zeros((8, d), F32).at[0].set(c[0]).at[1].set(c_ctx)
    ada = _ada(cvec, w_ada, b_ada)

    ones_bd = jnp.asarray(np.kron(np.eye(N_HEADS), np.ones((HEAD_DIM, HEAD_DIM))), BF16)
    rope_tabs = _rope_tables(seq)
    na_bias = _na_bias_tiles(na_rpb)
    row = lambda v: v.reshape(1, -1)
    qn_all = jnp.tile(q_norm, (1, N_HEADS))
    kn_all = jnp.tile(k_norm, (1, GQA_KV_HEADS))

    mod = lambda l, r: tuple(ada[l, r:r + 1, k * d:(k + 1) * d] for k in range(3))
    h = _modulate(xs, row(g_pre[0]), *mod(0, 0)[:2])
    hc = _modulate(cs, row(g_pre[0]), *mod(0, 1)[:2])
    for l in range(depth):
        last = l == depth - 1
        conv_args = (conv_w[l], row(conv_b[l]), row(conv_ln_g[l]), row(conv_ln_b[l]), conv_pw, l)
        gate, gate_c = mod(l, 0)[2], mod(l, 1)[2]
        next_mod = lambda r: None if last else (row(g_pre[l + 1]),) + mod(l + 1, r)[:2]
        qn, kn = row(qn_all[l]), row(kn_all[l])

        pc = _in_proj(hc, w_in, l, KV_COLS if last else None)
        qc_ctx, kq_ctx, vt_ctx = _qk_prep(pc, None if last else off["c_q"], off["c_k"], off["c_v"],
                                          qn, kn, ones_bd, None)

        p = _in_proj(h, w_in, l)
        qc, kc, vt = _qk_prep(p, off["c_q"], off["c_k"], off["c_v"], qn, kn, ones_bd, rope_tabs)
        o_c = _gqa_attn(qc, p, off["c_gate"], kc, vt, kq_ctx, vt_ctx)
        o_a = _na_attn(p, off, pc, na_bias, l)
        o_b = _pool(p, off, pool_w, l, row(pool_scale[l]))
        o_d = _conv(p, off, *conv_args)
        y = _merge(p, off["merge"], (o_a, o_b, o_c, o_d), w_branch, l)
        xs, h = _out_proj(y, w_out, l, xs, row(g_post[l]), gate, next_mod(0))

        if not last:
            o_a_c = _dense_attn(pc, off["a_q"], pc, off["a_gate"],
                                [(pc, off["a_k"], pc, off["a_v"])], False, SCORE_SCALE_LOG2)
            o_c_c = _dense_attn(qc_ctx, 0, pc, off["c_gate"], [(kq_ctx, 0, pc, off["c_v"])], True, 1.0)
            o_b_c = _pool(pc, off, pool_w, l, row(pool_scale[l]))
            o_d_c = _conv(pc, off, *conv_args)
            y_c = _merge(pc, off["merge"], (o_a_c, o_b_c, o_c_c, o_d_c), w_branch, l)
            cs, hc = _out_proj(y_c, w_out, l, cs, row(g_post[l]), gate_c, next_mod(1))
    return xs[None]
```

```python
import functools

import numpy as np
import jax
import jax.numpy as jnp
from jax import lax
from jax.experimental import pallas as pl
from jax.experimental.pallas import tpu as pltpu

F32 = jnp.float32
BF16 = jnp.bfloat16

GRID_W = 64
HEAD_DIM = 64
BRANCH_W = 512
N_BRANCH = 4
N_HEADS = BRANCH_W // HEAD_DIM
GQA_KV_HEADS = 2
KV_W = GQA_KV_HEADS * HEAD_DIM
NA_WIN_ROWS = 8
NA_WIN_COLS = 16
POOL_SIZES = (2, 4, 8, 16)
POOL_GROUP = BRANCH_W // len(POOL_SIZES)
ROPE_THETA = 10000.0
CONV_WIDTH = 31
EPS = 1e-6

LANES = 128
SUBLANES = 8
PACKED_ROWS = 16
HALO = PACKED_ROWS
ROPE_HALF = HEAD_DIM // 4
NA_ROW_BLOCK = 4
NA_BAND_ROWS = 12
NEG_BIG = -1e30
LOG2_E = 1.4426950408889634
SCORE_SCALE_LOG2 = HEAD_DIM ** -0.5 * LOG2_E
V7X_VMEM_BYTES = 64 * 1024 * 1024
V7X_VMEM_LIMIT = V7X_VMEM_BYTES * 13 // 16
IN_PROJ_VMEM_LIMIT = V7X_VMEM_BYTES * 29 // 32

PARTS = (("a_k", BRANCH_W), ("a_v", BRANCH_W), ("c_k", KV_W), ("c_v", KV_W), ("a_q", BRANCH_W), ("c_q", BRANCH_W),
         ("a_gate", BRANCH_W), ("b_in", BRANCH_W), ("b_gate", BRANCH_W), ("c_gate", BRANCH_W),
         ("d_glu", 2 * BRANCH_W), ("d_gate", BRANCH_W), ("merge", None))
KV_COLS = 2 * BRANCH_W + 2 * KV_W


def _layout(d_model):
    off, o = {}, 0
    for n, w in PARTS:
        off[n] = o
        o += N_BRANCH * d_model if w is None else w
    return off


def _params(vmem_limit=V7X_VMEM_LIMIT, **kw):
    return pltpu.CompilerParams(vmem_limit_bytes=vmem_limit, **kw)


def _tile(n, pref, mult):
    if n <= pref:
        return n
    t = (pref // mult) * mult
    while t >= mult:
        if n % t == 0:
            return t
        t -= mult
    raise ValueError(f"no tile for {n}")


def _resident(block_shape, index_map):
    return pl.BlockSpec(block_shape, index_map, pipeline_mode=pl.Buffered(1))


def _slab(tm, width, off):
    assert off % LANES == 0
    return pl.BlockSpec((pl.Element(tm), pl.Element(width)), lambda i: (i * tm, off))


def _silu(x):
    return x * jax.nn.sigmoid(x)


def _ada_body(cs_ref, w_ref, b_ref, o_ref):
    s = _silu(cs_ref[...]).astype(BF16)
    o_ref[0] = jnp.dot(s, w_ref[0].astype(BF16), preferred_element_type=F32) + b_ref[0]


def _ada(cs, w_ada, b_ada):
    depth, d, n = w_ada.shape
    tn = _tile(n, 1024, LANES)
    return pl.pallas_call(
        _ada_body,
        grid=(depth, n // tn),
        in_specs=[pl.BlockSpec((8, d), lambda l, j: (0, 0)),
                  pl.BlockSpec((1, d, tn), lambda l, j: (l, 0, j)),
                  pl.BlockSpec((1, 1, tn), lambda l, j: (l, 0, j))],
        out_specs=pl.BlockSpec((1, 8, tn), lambda l, j: (l, 0, j)),
        out_shape=jax.ShapeDtypeStruct((depth, 8, n), F32),
        compiler_params=_params(dimension_semantics=("arbitrary", "arbitrary")),
        name="ada",
    )(cs, w_ada, b_ada.reshape(depth, 1, n))


def _modulated(x, g, shift, scale):
    y = x * lax.rsqrt(jnp.mean(x * x, axis=-1, keepdims=True) + EPS) * g
    return (y * (1.0 + scale) + shift).astype(BF16)


def _modulate_body(x_ref, g_ref, sh_ref, sc_ref, o_ref):
    o_ref[...] = _modulated(x_ref[...], g_ref[...], sh_ref[...], sc_ref[...])


def _modulate(x, g, shift, scale):
    t, d = x.shape
    tm = _tile(t, 512, PACKED_ROWS)
    vec = pl.BlockSpec((1, d), lambda i: (0, 0))
    return pl.pallas_call(
        _modulate_body,
        grid=(t // tm,),
        in_specs=[pl.BlockSpec((tm, d), lambda i: (i, 0)), vec, vec, vec],
        out_specs=pl.BlockSpec((tm, d), lambda i: (i, 0)),
        out_shape=jax.ShapeDtypeStruct((t, d), BF16),
        compiler_params=_params(dimension_semantics=("arbitrary",)),
        name="modulate",
    )(x, g, shift, scale)


def _in_proj_body(h_ref, w_ref, o_ref, wb_ref):
    @pl.when(pl.program_id(1) == 0)
    def _():
        wb_ref[...] = w_ref[0].astype(BF16)

    half = h_ref.shape[0] // 2
    for r in (slice(0, half), slice(half, 2 * half)):
        o_ref[r, :] = jnp.dot(h_ref[r, :], wb_ref[...], preferred_element_type=F32).astype(o_ref.dtype)


def _col_tile(n):
    for mult in (2 * LANES, LANES):
        try:
            return _tile(n, 1536, mult)
        except ValueError:
            pass
    raise ValueError(n)


def _in_proj(h, w_in, layer, n_cols=None):
    t, d = h.shape
    n = w_in.shape[2]
    tm = _tile(t, 2048, 2 * PACKED_ROWS)
    tn = _col_tile(n)
    n_tiles = n // tn if n_cols is None else pl.cdiv(n_cols, tn)
    return pl.pallas_call(
        _in_proj_body,
        grid=(n_tiles, t // tm),
        in_specs=[pl.BlockSpec((tm, d), lambda j, i: (i, 0)),
                  pl.BlockSpec((1, d, tn), lambda j, i: (layer, 0, j))],
        out_specs=pl.BlockSpec((tm, tn), lambda j, i: (i, j)),
        out_shape=jax.ShapeDtypeStruct((t, n_tiles * tn), BF16),
        scratch_shapes=[pltpu.VMEM((d, tn), BF16)],
        compiler_params=_params(vmem_limit=IN_PROJ_VMEM_LIMIT, dimension_semantics=("arbitrary", "arbitrary")),
        name="in_proj",
    )(h, w_in)


def _head_meansq(x, ones_bd):
    ss = x * x
    hi = ss.astype(BF16)
    lo = (ss - hi.astype(F32)).astype(BF16)
    tot = (jnp.dot(hi, ones_bd, preferred_element_type=F32)
           + jnp.dot(lo, ones_bd, preferred_element_type=F32))
    return tot * (1.0 / HEAD_DIM)


def _rope(y, cos, sin_signed):
    w = y.shape[-1]
    lane = lax.broadcasted_iota(jnp.int32, y.shape, 1)
    nxt = pltpu.roll(y, w - ROPE_HALF, 1)
    prv = pltpu.roll(y, ROPE_HALF, 1)
    return y * cos + jnp.where((lane % (2 * ROPE_HALF)) < ROPE_HALF, nxt, prv) * sin_signed


def _norm_rope(x, w, bd, rope):
    x = x * lax.rsqrt(_head_meansq(x, bd) + EPS) * w
    if rope is not None:
        reps = x.shape[-1] // LANES
        x = _rope(x, jnp.concatenate([rope[0]] * reps, axis=1), jnp.concatenate([rope[1]] * reps, axis=1))
    return x


VT_ROWS = HEAD_DIM + PACKED_ROWS


def _qk_prep_body(*refs, use_rope, with_q):
    refs = list(refs)
    q_ref = refs.pop(0) if with_q else None
    k_ref, v_ref = refs.pop(0), refs.pop(0)
    qn_ref = refs.pop(0) if with_q else None
    kn_ref, bd_ref = refs.pop(0), refs.pop(0)
    rope = (refs.pop(0)[...], refs.pop(0)[...]) if use_rope else None
    bd = bd_ref[...]
    if with_q:
        q = _norm_rope(q_ref[...].astype(F32), qn_ref[...], bd, rope)
        refs.pop(0)[...] = (q * SCORE_SCALE_LOG2).astype(BF16)
    k = _norm_rope(k_ref[...].astype(F32), kn_ref[...], bd[:KV_W, :KV_W], rope)
    refs.pop(0)[...] = k.astype(BF16)
    vt_ref = refs.pop(0)
    vt = v_ref[...].astype(F32).T
    row = lax.broadcasted_iota(jnp.int32, (VT_ROWS - HEAD_DIM, vt.shape[1]), 0)
    tail = jnp.where(row == 0, 1.0, 0.0)
    for g in range(GQA_KV_HEADS):
        vt_ref[0, g] = jnp.concatenate([vt[HEAD_DIM * g:HEAD_DIM * (g + 1)], tail], axis=0).astype(BF16)


def _qk_prep(p, q_off, k_off, v_off, qn, kn, ones_bd, rope_tabs):
    t = p.shape[0]
    tm = _tile(t, 1024, LANES)
    use_rope, with_q = rope_tabs is not None, q_off is not None
    const = lambda shape: pl.BlockSpec(shape, lambda i: (0, 0))
    in_specs, args, out_specs, out_shape = [], [], [], []
    if with_q:
        in_specs.append(_slab(tm, BRANCH_W, q_off))
        args.append(p)
    in_specs += [_slab(tm, KV_W, k_off), _slab(tm, KV_W, v_off)]
    args += [p, p]
    if with_q:
        in_specs.append(const((1, BRANCH_W)))
        args.append(qn)
        out_specs.append(pl.BlockSpec((tm, BRANCH_W), lambda i: (i, 0)))
        out_shape.append(jax.ShapeDtypeStruct((t, BRANCH_W), BF16))
    in_specs += [const((1, KV_W)), const((BRANCH_W, BRANCH_W))]
    args += [kn, ones_bd]
    if use_rope:
        in_specs += [pl.BlockSpec((tm, LANES), lambda i: (i, 0))] * 2
        args += list(rope_tabs)
    out_specs += [pl.BlockSpec((tm, KV_W), lambda i: (i, 0)),
                  pl.BlockSpec((1, GQA_KV_HEADS, VT_ROWS, tm), lambda i: (i, 0, 0, 0))]
    out_shape += [jax.ShapeDtypeStruct((t, KV_W), BF16),
                  jax.ShapeDtypeStruct((t // tm, GQA_KV_HEADS, VT_ROWS, tm), BF16)]
    res = pl.pallas_call(
        functools.partial(_qk_prep_body, use_rope=use_rope, with_q=with_q),
        grid=(t // tm,),
        in_specs=in_specs,
        out_specs=out_specs,
        out_shape=out_shape,
        compiler_params=_params(dimension_semantics=("arbitrary",)),
        name="qk_prep",
    )(*args)
    return res if with_q else (None, res[0], res[1])


def _rope_tables(seq):
    half = ROPE_HALF
    t = lax.broadcasted_iota(jnp.int32, (seq, LANES), 0)
    lane = lax.broadcasted_iota(jnp.int32, (seq, LANES), 1)
    pos = jnp.where((lane % HEAD_DIM) < 2 * half, t // GRID_W, t % GRID_W).astype(F32)
    freqs = ROPE_THETA ** (-(lane % half).astype(F32) / half)
    ang = pos * freqs
    sign = jnp.where((lane % (2 * half)) < half, -1.0, 1.0)
    return jnp.cos(ang), jnp.sin(ang) * sign


def _nt_dot(a, b):
    return lax.dot_general(a, b, (((1,), (1,)), ((), ())), preferred_element_type=F32)


def _pair_queries(qp, low, shared_kv_lanes):
    zero = jnp.zeros_like(qp)
    if shared_kv_lanes is None:
        return jnp.where(low, qp, zero), jnp.where(low, zero, qp)
    qr = pltpu.roll(qp, HEAD_DIM, 1)
    if shared_kv_lanes == 0:
        return jnp.where(low, qp, zero), jnp.where(low, qr, zero)
    return jnp.where(low, zero, qr), jnp.where(low, zero, qp)


def _values_with_ones(vc, half):
    lane = lax.broadcasted_iota(jnp.int32, vc.shape, 1)
    keep = (lane < HEAD_DIM) if half == 0 else (lane >= HEAD_DIM)
    ones_lane = HEAD_DIM * (1 - half)
    fill = jnp.where(lane == ones_lane, 1.0, 0.0)
    return jnp.where(keep, vc.astype(F32), fill).astype(vc.dtype)


def _dense_attn_body(*refs, n_src, chunks, kv_grouped, q_scale, tq):
    n_pair = N_HEADS // 2
    q_ref, gate_ref = refs[0], refs[1]
    srcs = [(refs[2 + 2 * i], refs[3 + 2 * i]) for i in range(n_src)]
    o_ref = refs[2 + 2 * n_src]
    scratch = refs[3 + 2 * n_src:]
    m_s, acc_s = scratch[:n_pair], scratch[n_pair:]
    low_q = lax.broadcasted_iota(jnp.int32, (tq, LANES), 1) < HEAD_DIM

    halves = [((2 * p) // (N_HEADS // GQA_KV_HEADS),) * 2 if kv_grouped else (0, 1) for p in range(n_pair)]
    lhs = []
    for p in range(n_pair):
        qp = q_ref[:, LANES * p:LANES * (p + 1)].astype(F32) * q_scale
        qa, qb = _pair_queries(qp, low_q, halves[p][0] if kv_grouped else None)
        lhs.append(jnp.concatenate([qa, qb], axis=0).astype(BF16))
        m_s[p][...] = jnp.full(m_s[p].shape, NEG_BIG, F32)
        acc_s[p][...] = jnp.zeros(acc_s[p].shape, F32)

    def step(p, kc, va, vb):
        s = _nt_dot(lhs[p], kc)
        m_old = m_s[p][...]
        m_new = jnp.maximum(m_old, jnp.max(s, axis=-1, keepdims=True))
        alpha = jnp.exp2(m_old - m_new)
        pr = jnp.exp2(s - m_new).astype(BF16)
        acc = acc_s[p]
        acc[:tq] = alpha[:tq] * acc[:tq] + jnp.dot(pr[:tq], va, preferred_element_type=F32)
        acc[tq:] = alpha[tq:] * acc[tq:] + jnp.dot(pr[tq:], vb, preferred_element_type=F32)
        m_s[p][...] = m_new

    def all_pairs(k_ref, v_ref, rows):
        if kv_grouped:
            kc, vc = k_ref[rows, :], v_ref[rows, :]
            vals = [_values_with_ones(vc, h) for h in range(GQA_KV_HEADS)]
        for p in range(n_pair):
            if kv_grouped:
                step(p, kc, vals[halves[p][0]], vals[halves[p][1]])
            else:
                cols = slice(LANES * p, LANES * (p + 1))
                vc = v_ref[rows, cols]
                step(p, k_ref[rows, cols], _values_with_ones(vc, 0), _values_with_ones(vc, 1))

    for (k_ref, v_ref), ck in zip(srcs, chunks):
        n_chunk = k_ref.shape[0] // ck
        if n_chunk == 1:
            all_pairs(k_ref, v_ref, slice(None))
        else:
            def loop(i, carry, k_ref=k_ref, v_ref=v_ref, ck=ck):
                all_pairs(k_ref, v_ref, pl.ds(pl.multiple_of(i * ck, ck), ck))
                return carry
            lax.fori_loop(0, n_chunk, loop, 0)

    for p in range(n_pair):
        cols = slice(LANES * p, LANES * (p + 1))
        heads = []
        for hd, half in enumerate(halves[p]):
            acc = acc_s[p][hd * tq:(hd + 1) * tq]
            ones_lane = HEAD_DIM * (1 - half)
            o = acc / acc[:, ones_lane:ones_lane + 1]
            heads.append(o if half == hd else pltpu.roll(o, HEAD_DIM, 1))
        o = jnp.where(low_q, heads[0], heads[1])
        o_ref[:, cols] = (o * _silu(gate_ref[:, cols].astype(F32))).astype(BF16)


def _key_chunk(tk):
    for mult in (2 * LANES, LANES, PACKED_ROWS):
        try:
            return _tile(tk, 1024, mult)
        except ValueError:
            pass
    raise ValueError(tk)


def _dense_attn(q, q_off, gate, gate_off, kv_srcs, kv_grouped, q_scale):
    t = q.shape[0]
    tq = _tile(t, 256, PACKED_ROWS)
    kw = KV_W if kv_grouped else BRANCH_W
    in_specs = [_slab(tq, BRANCH_W, q_off), _slab(tq, BRANCH_W, gate_off)]
    args = [q, gate]
    chunks = []
    for k_arr, k_off, v_arr, v_off in kv_srcs:
        tk = k_arr.shape[0]
        assert k_off % kw == 0 and v_off % kw == 0
        in_specs += [_resident((tk, kw), lambda i, _c=k_off // kw: (0, _c)),
                     _resident((tk, kw), lambda i, _c=v_off // kw: (0, _c))]
        args += [k_arr, v_arr]
        chunks.append(_key_chunk(tk))
    n_pair = N_HEADS // 2
    return pl.pallas_call(
        functools.partial(_dense_attn_body, n_src=len(kv_srcs), chunks=tuple(chunks),
                          kv_grouped=kv_grouped, q_scale=q_scale, tq=tq),
        grid=(t // tq,),
        in_specs=in_specs,
        out_specs=pl.BlockSpec((tq, BRANCH_W), lambda i: (i, 0)),
        out_shape=jax.ShapeDtypeStruct((t, BRANCH_W), BF16),
        scratch_shapes=[pltpu.VMEM((2 * tq, 1), F32)] * n_pair + [pltpu.VMEM((2 * tq, LANES), F32)] * n_pair,
        compiler_params=_params(dimension_semantics=("arbitrary",)),
        name="dense_attn",
    )(*args)


def _gqa_body(q_ref, qn_ref, gate_ref, k_ref, vt_ref, kx_ref, vtx_ref, o_ref, *scratch, tq, n_chunk):
    n_pair = N_HEADS // 2
    qt_s, qtn_s, m_s, acc_s, sx_s = (scratch[i * n_pair:(i + 1) * n_pair] for i in range(5))
    s_s = [scratch[5 * n_pair + 2 * p:5 * n_pair + 2 * p + 2] for p in range(n_pair)]
    mx_s = [scratch[7 * n_pair + 3 * p:7 * n_pair + 3 * p + 3] for p in range(n_pair)]
    s_s = [[(s_s[p][slot], mx_s[p][slot]) for slot in range(2)] for p in range(n_pair)]
    sx_s = [(sx_s[p], mx_s[p][2]) for p in range(n_pair)]
    low_q = lax.broadcasted_iota(jnp.int32, (tq, LANES), 1) < HEAD_DIM
    kv_of = [(2 * p) // (N_HEADS // GQA_KV_HEADS) for p in range(n_pair)]

    def scores(p, keys, buf, qt_ref=None):
        s = jnp.dot(keys, (qt_s[p] if qt_ref is None else qt_ref)[...], preferred_element_type=F32)
        buf[0][...] = s
        buf[1][...] = jnp.max(s, axis=0, keepdims=True)

    def consume(p, buf, vt):
        m_old = m_s[p][...]
        m_new = jnp.maximum(m_old, buf[1][...])
        alpha = jnp.exp2(m_old - m_new)
        pt = jnp.exp2(buf[0][...] - m_new).astype(BF16)
        for hd in range(2):
            cols = slice(hd * tq, (hd + 1) * tq)
            acc_s[p][hd] = alpha[:, cols] * acc_s[p][hd] + jnp.dot(vt, pt[:, cols], preferred_element_type=F32)
        m_s[p][...] = m_new

    def step(i, slot):
        for p in range(n_pair):
            scores(p, k_ref[i + 1], s_s[p][1 - slot])
            consume(p, s_s[p][slot], vt_ref[i, kv_of[p]])

    def transposed_queries(p, q_tile_ref):
        qp = q_tile_ref[:, LANES * p:LANES * (p + 1)].astype(F32)
        qa, qb = _pair_queries(qp, low_q, kv_of[p])
        return jnp.concatenate([qa, qb], axis=0).T.astype(BF16)

    is_first = pl.program_id(0) == 0

    @pl.when(is_first)
    def _():
        for p in range(n_pair):
            qt_s[p][...] = transposed_queries(p, q_ref)
            scores(p, k_ref[0], s_s[p][0])

    @pl.when(jnp.logical_not(is_first))
    def _():
        for p in range(n_pair):
            qt_s[p][...] = qtn_s[p][...]

    for p in range(n_pair):
        m_s[p][...] = jnp.full(m_s[p].shape, NEG_BIG, F32)
        acc_s[p][...] = jnp.zeros(acc_s[p].shape, F32)

    def two_steps(j, carry):
        step(2 * j, 0)
        step(2 * j + 1, 1)
        return carry

    n_step = n_chunk - 1
    lax.fori_loop(0, n_step // 2, two_steps, 0)
    last = n_chunk - 1
    if n_step % 2:
        step(last - 1, 0)
    def next_scores(p):
        scores(p, k_ref[0], s_s[p][0], qtn_s[p])

    for p in range(n_pair):
        qtn_s[p][...] = transposed_queries(p, qn_ref)
    for p in range(n_pair):
        scores(p, kx_ref[...], sx_s[p])
        if last % 2:
            next_scores(p)
        consume(p, s_s[p][last % 2], vt_ref[last, kv_of[p]])
    for p in range(n_pair):
        if not last % 2:
            next_scores(p)
        consume(p, sx_s[p], vtx_ref[0, kv_of[p]])
        acc = acc_s[p]
        ot = jnp.concatenate([acc[hd, :HEAD_DIM] / acc[hd, HEAD_DIM:HEAD_DIM + 1] for hd in range(2)], axis=0)
        cols = slice(LANES * p, LANES * (p + 1))
        o_ref[:, cols] = (ot.T * _silu(gate_ref[:, cols].astype(F32))).astype(BF16)


def _gqa_attn(q, gate, gate_off, k, vt, k_extra, vt_extra):
    t = q.shape[0]
    n_chunk, _, _, ck = vt.shape
    tx = k_extra.shape[0]
    assert vt_extra.shape[0] == 1 and k.shape[0] == n_chunk * ck
    tq = _tile(t, 256, LANES)
    n_pair = N_HEADS // 2
    scratch = ([pltpu.VMEM((LANES, 2 * tq), BF16)] * (2 * n_pair) + [pltpu.VMEM((1, 2 * tq), F32)] * n_pair
               + [pltpu.VMEM((2, VT_ROWS, tq), F32)] * n_pair + [pltpu.VMEM((tx, 2 * tq), F32)] * n_pair
               + [pltpu.VMEM((ck, 2 * tq), F32)] * (2 * n_pair) + [pltpu.VMEM((1, 2 * tq), F32)] * (3 * n_pair))
    return pl.pallas_call(
        functools.partial(_gqa_body, tq=tq, n_chunk=n_chunk),
        grid=(t // tq,),
        in_specs=[pl.BlockSpec((tq, BRANCH_W), lambda i: (i, 0)),
                  pl.BlockSpec((tq, BRANCH_W), lambda i: (jnp.minimum(i + 1, t // tq - 1), 0)),
                  _slab(tq, BRANCH_W, gate_off),
                  _resident((n_chunk, ck, KV_W), lambda i: (0, 0, 0)),
                  _resident(vt.shape, lambda i: (0, 0, 0, 0)),
                  _resident((tx, KV_W), lambda i: (0, 0)),
                  _resident(vt_extra.shape, lambda i: (0, 0, 0, 0))],
        out_specs=pl.BlockSpec((tq, BRANCH_W), lambda i: (i, 0)),
        out_shape=jax.ShapeDtypeStruct((t, BRANCH_W), BF16),
        scratch_shapes=scratch,
        compiler_params=_params(dimension_semantics=("arbitrary",)),
        name="gqa_attn",
    )(q, q, gate, k.reshape(n_chunk, ck, KV_W), vt, k_extra, vt_extra)


def _na_bias_tiles(rpb):
    cq = np.arange(GRID_W)
    col_start = np.clip(cq - NA_WIN_COLS // 2, 0, GRID_W - NA_WIN_COLS)
    col_ok = (cq[None, :] >= col_start[:, None]) & (cq[None, :] < col_start[:, None] + NA_WIN_COLS)
    col_off = np.clip(cq[None, :] - cq[:, None], -(NA_WIN_COLS - 1), NA_WIN_COLS - 1) + NA_WIN_COLS - 1
    onehot = jnp.asarray(col_off[:, :, None] == np.arange(2 * NA_WIN_COLS - 1), F32)
    tiles = jnp.einsum("lhrm,qkm->lhrqk", rpb.astype(F32), onehot, precision=lax.Precision.HIGHEST)
    tiles = jnp.where(col_ok, tiles * LOG2_E, NEG_BIG)
    masked = jnp.full(tiles.shape[:2] + (1, GRID_W, GRID_W), NEG_BIG, F32)
    tiles = jnp.concatenate([tiles, masked], axis=2)
    return jnp.concatenate([tiles, tiles], axis=-1)


def _na_body(q_ref, gate_ref, k_ref, v_ref, kc_ref, vc_ref, tiles_ref, o_ref, *, n_rows):
    rb, band = NA_ROW_BLOCK, NA_BAND_ROWS
    tq, nk = rb * GRID_W, band * GRID_W
    b = pl.program_id(0)
    u0 = jnp.clip(b * rb - NA_WIN_ROWS // 2, 0, n_rows - band)
    rows = pl.ds(pl.multiple_of(u0 * GRID_W, GRID_W), nk)
    low_q = lax.broadcasted_iota(jnp.int32, (tq, LANES), 1) < HEAD_DIM
    low_t = lax.broadcasted_iota(jnp.int32, (GRID_W, LANES), 1) < GRID_W

    def bias_rows(h, i):
        r = b * rb + i
        row_start = jnp.clip(r - NA_WIN_ROWS // 2, 0, n_rows - NA_WIN_ROWS)
        pieces = []
        for j in range(band):
            key_row = u0 + j
            in_window = jnp.logical_and(key_row >= row_start, key_row < row_start + NA_WIN_ROWS)
            ro = jnp.where(in_window, key_row - r + NA_WIN_ROWS - 1, 2 * NA_WIN_ROWS - 1)
            pieces.append(tiles_ref[0, h, ro])
        return jnp.concatenate([jnp.where(low_t, pieces[j], pieces[j + 1]) for j in range(0, band, 2)], axis=1)

    for p in range(N_HEADS // 2):
        bias = jnp.concatenate([bias_rows(2 * p + hd, i) for hd in range(2) for i in range(rb)], axis=0)
        cols = slice(LANES * p, LANES * (p + 1))
        qp = q_ref[:, cols].astype(F32) * SCORE_SCALE_LOG2
        qa, qb = _pair_queries(qp, low_q, None)
        lhs = jnp.concatenate([qa, qb], axis=0).astype(BF16)
        kb, vb = k_ref[rows, cols], v_ref[rows, cols]
        kc, vc = kc_ref[:, cols], vc_ref[:, cols]
        s_band = _nt_dot(lhs, kb) + bias
        s_ctx = _nt_dot(lhs, kc)
        m = jnp.maximum(jnp.max(s_band, axis=-1, keepdims=True), jnp.max(s_ctx, axis=-1, keepdims=True))
        pb = jnp.exp2(s_band - m).astype(BF16)
        pc = jnp.exp2(s_ctx - m).astype(BF16)
        heads = []
        for hd in range(2):
            r = slice(hd * tq, (hd + 1) * tq)
            acc = (jnp.dot(pb[r], _values_with_ones(vb, hd), preferred_element_type=F32)
                   + jnp.dot(pc[r], _values_with_ones(vc, hd), preferred_element_type=F32))
            ones_lane = HEAD_DIM * (1 - hd)
            heads.append(acc / acc[:, ones_lane:ones_lane + 1])
        o = jnp.where(low_q, heads[0], heads[1])
        o_ref[:, cols] = (o * _silu(gate_ref[:, cols].astype(F32))).astype(BF16)


def _na_attn(p, off, pc, bias, layer):
    s = p.shape[0]
    cn = pc.shape[0]
    n_rows = s // GRID_W
    tq = NA_ROW_BLOCK * GRID_W
    assert n_rows % NA_ROW_BLOCK == 0 and n_rows >= NA_BAND_ROWS >= NA_ROW_BLOCK + NA_WIN_ROWS - 1
    assert NA_BAND_ROWS % 2 == 0
    nb = n_rows // NA_ROW_BLOCK
    w = BRANCH_W
    return pl.pallas_call(
        functools.partial(_na_body, n_rows=n_rows),
        grid=(nb,),
        in_specs=[_slab(tq, w, off["a_q"]), _slab(tq, w, off["a_gate"]),
                  _resident((s, w), lambda i, _c=off["a_k"] // w: (0, _c)),
                  _resident((s, w), lambda i, _c=off["a_v"] // w: (0, _c)),
                  _resident((cn, w), lambda i, _c=off["a_k"] // w: (0, _c)),
                  _resident((cn, w), lambda i, _c=off["a_v"] // w: (0, _c)),
                  _resident((1,) + bias.shape[1:], lambda i: (layer, 0, 0, 0, 0))],
        out_specs=pl.BlockSpec((tq, w), lambda i: (i, 0)),
        out_shape=jax.ShapeDtypeStruct((s, w), BF16),
        compiler_params=_params(dimension_semantics=("arbitrary",)),
        name="na_attn",
    )(p, p, p, p, pc, pc, bias)


def _halo_specs(t, tm, width, off):
    assert off % LANES == 0 and tm % HALO == 0
    block = lambda rows: (pl.Element(rows), pl.Element(width))
    per, last = tm // HALO, t // HALO - 1
    return [pl.BlockSpec(block(tm), lambda i: (i * tm, off)),
            pl.BlockSpec(block(HALO), lambda i: (jnp.maximum(i * per - 1, 0) * HALO, off)),
            pl.BlockSpec(block(HALO), lambda i: (jnp.minimum((i + 1) * per, last) * HALO, off))]


def _fill_padded(pad_ref, cur, prev, nxt, tm):
    i, n = pl.program_id(0), pl.num_programs(0)
    pad_ref[0:HALO] = jnp.where(i > 0, prev, jnp.zeros_like(prev))
    pad_ref[HALO:HALO + tm] = cur
    pad_ref[HALO + tm:] = jnp.where(i < n - 1, nxt, jnp.zeros_like(nxt))


def _pool_body(u_ref, up_ref, un_ref, gate_ref, w_ref, sc_ref, o_ref, pad_ref, *, tm, seq):
    _fill_padded(pad_ref, u_ref[...].astype(F32), up_ref[...].astype(F32), un_ref[...].astype(F32), tm)
    t = pl.program_id(0) * tm + lax.broadcasted_iota(jnp.int32, (tm, POOL_GROUP), 0)
    for gi, ksz in enumerate(POOL_SIZES):
        cols = slice(POOL_GROUP * gi, POOL_GROUP * (gi + 1))
        back = ksz // 2
        tot = pad_ref[HALO - back:HALO - back + tm, cols]
        for d in range(1 - back, ksz - back):
            tot = tot + pad_ref[HALO + d:HALO + d + tm, cols]
        lo = jnp.maximum(t - back, 0)
        hi = jnp.minimum(t + (ksz - 1 - back), seq - 1)
        mean = tot / (hi - lo + 1).astype(F32)
        dlt = (mean - pad_ref[HALO:HALO + tm, cols]).astype(BF16)
        y = jnp.dot(dlt, w_ref[0, gi].astype(BF16), preferred_element_type=F32) * sc_ref[:, cols]
        o_ref[:, cols] = (y * _silu(gate_ref[:, cols].astype(F32))).astype(BF16)


def _pool(p, off, w_pool, layer, pool_scale):
    t = p.shape[0]
    tm = _tile(t, 512, HALO)
    w = BRANCH_W
    return pl.pallas_call(
        functools.partial(_pool_body, tm=tm, seq=t),
        grid=(t // tm,),
        in_specs=_halo_specs(t, tm, w, off["b_in"]) + [
            _slab(tm, w, off["b_gate"]),
            pl.BlockSpec((1,) + w_pool.shape[1:], lambda i: (layer, 0, 0, 0)),
            pl.BlockSpec((1, w), lambda i: (0, 0))],
        out_specs=pl.BlockSpec((tm, w), lambda i: (i, 0)),
        out_shape=jax.ShapeDtypeStruct((t, w), BF16),
        scratch_shapes=[pltpu.VMEM((tm + 2 * HALO, w), F32)],
        compiler_params=_params(dimension_semantics=("arbitrary",)),
        name="pool",
    )(p, p, p, p, w_pool, pool_scale)


def _glu(x):
    x = x.astype(F32)
    return x[:, :BRANCH_W] * jax.nn.sigmoid(x[:, BRANCH_W:])


def _conv_body(x_ref, xp_ref, xn_ref, gate_ref, cw_ref, cb_ref, lg_ref, lb_ref, pw_ref, o_ref, pad_ref, sh_ref,
               *, tm):
    _fill_padded(pad_ref, _glu(x_ref[...]), _glu(xp_ref[...]), _glu(xn_ref[...]), tm)
    span = sh_ref.shape[1]
    for b in range(1, SUBLANES):
        sh_ref[b - 1] = pad_ref[b:b + span, :]
    reach = CONV_WIDTH // 2
    y = jnp.zeros((tm, BRANCH_W), F32) + cb_ref[...]
    for j in range(CONV_WIDTH):
        a, b = divmod(HALO - reach + j, SUBLANES)
        rows = slice(SUBLANES * a, SUBLANES * a + tm)
        y = y + (pad_ref[rows, :] if b == 0 else sh_ref[b - 1, rows, :]) * cw_ref[j:j + 1, :]
    mu = jnp.mean(y, axis=-1, keepdims=True)
    yc = y - mu
    var = jnp.mean(yc * yc, axis=-1, keepdims=True)
    z = _silu(yc * lax.rsqrt(var + EPS) * lg_ref[...] + lb_ref[...]).astype(BF16)
    out = jnp.dot(z, pw_ref[0].astype(BF16), preferred_element_type=F32)
    o_ref[...] = (out * _silu(gate_ref[...].astype(F32))).astype(BF16)


def _conv(p, off, conv_w, conv_b, ln_g, ln_b, w_pw, layer):
    t = p.shape[0]
    tm = _tile(t, 512, HALO)
    w = BRANCH_W
    vec = pl.BlockSpec((1, w), lambda i: (0, 0))
    return pl.pallas_call(
        functools.partial(_conv_body, tm=tm),
        grid=(t // tm,),
        in_specs=_halo_specs(t, tm, 2 * w, off["d_glu"]) + [
            _slab(tm, w, off["d_gate"]),
            pl.BlockSpec(conv_w.shape, lambda i: (0, 0)), vec, vec, vec,
            pl.BlockSpec((1, w, w), lambda i: (layer, 0, 0))],
        out_specs=pl.BlockSpec((tm, w), lambda i: (i, 0)),
        out_shape=jax.ShapeDtypeStruct((t, w), BF16),
        scratch_shapes=[pltpu.VMEM((tm + 2 * HALO, w), F32),
                        pltpu.VMEM((SUBLANES - 1, tm + 2 * HALO - SUBLANES, w), F32)],
        compiler_params=_params(dimension_semantics=("arbitrary",)),
        name="conv",
    )(p, p, p, p, conv_w, conv_b, ln_g, ln_b, w_pw)


def _cast_once(w_ref, wb_ref):
    @pl.when(pl.program_id(0) == 0)
    def _():
        wb_ref[...] = w_ref[0].astype(BF16)


def _merge_body(m0, m1, m2, m3, o0, o1, o2, o3, w_ref, y_ref, wb_ref):
    _cast_once(w_ref, wb_ref)
    y = None
    for bi, (m_ref, o_ref) in enumerate(((m0, o0), (m1, o1), (m2, o2), (m3, o3))):
        proj = jnp.dot(o_ref[...], wb_ref[bi], preferred_element_type=F32)
        term = (1.0 + jnp.tanh(0.5 * m_ref[...].astype(F32))) * proj
        y = term if y is None else y + term
    y_ref[...] = (0.5 * y).astype(BF16)


def _merge(p, merge_off, outs, w_branch, layer):
    t = p.shape[0]
    d = w_branch.shape[-1]
    tm = _tile(t, 256, PACKED_ROWS)
    return pl.pallas_call(
        _merge_body,
        grid=(t // tm,),
        in_specs=[_slab(tm, d, merge_off + bi * d) for bi in range(N_BRANCH)]
        + [pl.BlockSpec((tm, BRANCH_W), lambda i: (i, 0))] * N_BRANCH
        + [_resident((1,) + w_branch.shape[1:], lambda i: (layer, 0, 0, 0))],
        out_specs=pl.BlockSpec((tm, d), lambda i: (i, 0)),
        out_shape=jax.ShapeDtypeStruct((t, d), BF16),
        scratch_shapes=[pltpu.VMEM(w_branch.shape[1:], BF16)],
        compiler_params=_params(dimension_semantics=("arbitrary",)),
        name="merge",
    )(p, p, p, p, *outs, w_branch)


def _out_body(y_ref, w_ref, x_ref, g_ref, gate_ref, *rest):
    *rest, wb_ref = rest
    _cast_once(w_ref, wb_ref)
    y = jnp.dot(y_ref[...], wb_ref[...], preferred_element_type=F32)
    yn = y * lax.rsqrt(jnp.mean(y * y, axis=-1, keepdims=True) + EPS) * g_ref[...]
    x_next = x_ref[...] + gate_ref[...] * yn
    if len(rest) == 1:
        rest[0][...] = x_next
    else:
        gn_ref, sh_ref, sc_ref, o_ref, h_ref = rest
        o_ref[...] = x_next
        h_ref[...] = _modulated(x_next, gn_ref[...], sh_ref[...], sc_ref[...])


def _out_proj(y, w_out, layer, x, g_post, gate, next_mod=None):
    t, d = x.shape
    tm = _tile(t, 256, PACKED_ROWS)
    vec = pl.BlockSpec((1, d), lambda i: (0, 0))
    row = pl.BlockSpec((tm, d), lambda i: (i, 0))
    n_next = 0 if next_mod is None else 1
    out = pl.pallas_call(
        _out_body,
        grid=(t // tm,),
        in_specs=[row, _resident((1, d, d), lambda i: (layer, 0, 0)), row, vec, vec] + [vec] * (3 * n_next),
        scratch_shapes=[pltpu.VMEM((d, d), BF16)],
        out_specs=[row] * (1 + n_next),
        out_shape=[jax.ShapeDtypeStruct((t, d), F32)] + [jax.ShapeDtypeStruct((t, d), BF16)] * n_next,
        compiler_params=_params(dimension_semantics=("arbitrary",)),
        name="out_proj",
    )(y, w_out, x, g_post, gate, *(next_mod or ()))
    return out if n_next else (out[0], None)


def kernel(x, c, ctx, c_ctx, w_ada, b_ada, g_pre, g_post, w_in, na_rpb, pool_w, pool_scale,
           q_norm, k_norm, conv_w, conv_b, conv_ln_g, conv_ln_b, conv_pw, w_branch, w_out):
    batch, seq, d = x.shape
    assert batch == 1 and seq % GRID_W == 0
    cn = ctx.shape[1]
    depth = w_ada.shape[0]
    off = _layout(d)

    xs, cs = x[0], ctx[0]
    cvec = jnp.zeros((8, d), F32).at[0].set(c[0]).at[1].set(c_ctx)
    ada = _ada(cvec, w_ada, b_ada)

    ones_bd = jnp.asarray(np.kron(np.eye(N_HEADS), np.ones((HEAD_DIM, HEAD_DIM))), BF16)
    rope_tabs = _rope_tables(seq)
    na_bias = _na_bias_tiles(na_rpb)
    row = lambda v: v.reshape(1, -1)
    qn_all = jnp.tile(q_norm, (1, N_HEADS))
    kn_all = jnp.tile(k_norm, (1, GQA_KV_HEADS))

    mod = lambda l, r: tuple(ada[l, r:r + 1, k * d:(k + 1) * d] for k in range(3))
    h = _modulate(xs, row(g_pre[0]), *mod(0, 0)[:2])
    hc = _modulate(cs, row(g_pre[0]), *mod(0, 1)[:2])
    for l in range(depth):
        last = l == depth - 1
        conv_args = (conv_w[l], row(conv_b[l]), row(conv_ln_g[l]), row(conv_ln_b[l]), conv_pw, l)
        gate, gate_c = mod(l, 0)[2], mod(l, 1)[2]
        next_mod = lambda r: None if last else (row(g_pre[l + 1]),) + mod(l + 1, r)[:2]
        qn, kn = row(qn_all[l]), row(kn_all[l])

        pc = _in_proj(hc, w_in, l, KV_COLS if last else None)
        qc_ctx, kq_ctx, vt_ctx = _qk_prep(pc, None if last else off["c_q"], off["c_k"], off["c_v"],
                                          qn, kn, ones_bd, None)

        p = _in_proj(h, w_in, l)
        qc, kc, vt = _qk_prep(p, off["c_q"], off["c_k"], off["c_v"], qn, kn, ones_bd, rope_tabs)
        o_c = _gqa_attn(qc, p, off["c_gate"], kc, vt, kq_ctx, vt_ctx)
        o_a = _na_attn(p, off, pc, na_bias, l)
        o_b = _pool(p, off, pool_w, l, row(pool_scale[l]))
        o_d = _conv(p, off, *conv_args)
        y = _merge(p, off["merge"], (o_a, o_b, o_c, o_d), w_branch, l)
        xs, h = _out_proj(y, w_out, l, xs, row(g_post[l]), gate, next_mod(0))

        if not last:
            o_a_c = _dense_attn(pc, off["a_q"], pc, off["a_gate"],
                                [(pc, off["a_k"], pc, off["a_v"])], False, SCORE_SCALE_LOG2)
            o_c_c = _dense_attn(qc_ctx, 0, pc, off["c_gate"], [(kq_ctx, 0, pc, off["c_v"])], True, 1.0)
            o_b_c = _pool(pc, off, pool_w, l, row(pool_scale[l]))
            o_d_c = _conv(pc, off, *conv_args)
            y_c = _merge(pc, off["merge"], (o_a_c, o_b_c, o_c_c, o_d_c), w_branch, l)
            cs, hc = _out_proj(y_c, w_out, l, cs, row(g_post[l]), gate_c, next_mod(1))
    return xs[None]
```

```python
import functools

import numpy as np
import jax
import jax.numpy as jnp
from jax import lax
from jax.experimental import pallas as pl
from jax.experimental.pallas import tpu as pltpu

F32 = jnp.float32
BF16 = jnp.bfloat16

GRID_W = 64
HEAD_DIM = 64
BRANCH_W = 512
N_BRANCH = 4
N_HEADS = BRANCH_W // HEAD_DIM
GQA_KV_HEADS = 2
KV_W = GQA_KV_HEADS * HEAD_DIM
NA_WIN_ROWS = 8
NA_WIN_COLS = 16
POOL_SIZES = (2, 4, 8, 16)
POOL_GROUP = BRANCH_W // len(POOL_SIZES)
ROPE_THETA = 10000.0
CONV_WIDTH = 31
EPS = 1e-6

LANES = 128
SUBLANES = 8
PACKED_ROWS = 16
HALO = PACKED_ROWS
ROPE_HALF = HEAD_DIM // 4
NA_ROW_BLOCK = 4
NA_BAND_ROWS = 12
NEG_BIG = -1e30
LOG2_E = 1.4426950408889634
SCORE_SCALE_LOG2 = HEAD_DIM ** -0.5 * LOG2_E
V7X_VMEM_BYTES = 64 * 1024 * 1024
V7X_VMEM_LIMIT = V7X_VMEM_BYTES * 13 // 16
IN_PROJ_VMEM_LIMIT = V7X_VMEM_BYTES * 29 // 32

PARTS = (("a_k", BRANCH_W), ("a_v", BRANCH_W), ("c_k", KV_W), ("c_v", KV_W), ("a_q", BRANCH_W), ("c_q", BRANCH_W),
         ("a_gate", BRANCH_W), ("b_in", BRANCH_W), ("b_gate", BRANCH_W), ("c_gate", BRANCH_W),
         ("d_glu", 2 * BRANCH_W), ("d_gate", BRANCH_W), ("merge", None))
KV_COLS = 2 * BRANCH_W + 2 * KV_W


def _layout(d_model):
    off, o = {}, 0
    for n, w in PARTS:
        off[n] = o
        o += N_BRANCH * d_model if w is None else w
    return off


def _params(vmem_limit=V7X_VMEM_LIMIT, **kw):
    return pltpu.CompilerParams(vmem_limit_bytes=vmem_limit, **kw)


def _tile(n, pref, mult):
    if n <= pref:
        return n
    t = (pref // mult) * mult
    while t >= mult:
        if n % t == 0:
            return t
        t -= mult
    raise ValueError(f"no tile for {n}")


def _resident(block_shape, index_map):
    return pl.BlockSpec(block_shape, index_map, pipeline_mode=pl.Buffered(1))


def _slab(tm, width, off):
    assert off % LANES == 0
    return pl.BlockSpec((pl.Element(tm), pl.Element(width)), lambda i: (i * tm, off))


def _silu(x):
    return x * jax.nn.sigmoid(x)


def _ada_body(cs_ref, w_ref, b_ref, o_ref):
    s = _silu(cs_ref[...]).astype(BF16)
    o_ref[0] = jnp.dot(s, w_ref[0].astype(BF16), preferred_element_type=F32) + b_ref[0]


def _ada(cs, w_ada, b_ada):
    depth, d, n = w_ada.shape
    tn = _tile(n, 1024, LANES)
    return pl.pallas_call(
        _ada_body,
        grid=(depth, n // tn),
        in_specs=[pl.BlockSpec((8, d), lambda l, j: (0, 0)),
                  pl.BlockSpec((1, d, tn), lambda l, j: (l, 0, j)),
                  pl.BlockSpec((1, 1, tn), lambda l, j: (l, 0, j))],
        out_specs=pl.BlockSpec((1, 8, tn), lambda l, j: (l, 0, j)),
        out_shape=jax.ShapeDtypeStruct((depth, 8, n), F32),
        compiler_params=_params(dimension_semantics=("arbitrary", "arbitrary")),
        name="ada",
    )(cs, w_ada, b_ada.reshape(depth, 1, n))


def _modulated(x, g, shift, scale):
    y = x * lax.rsqrt(jnp.mean(x * x, axis=-1, keepdims=True) + EPS) * g
    return (y * (1.0 + scale) + shift).astype(BF16)


def _modulate_body(x_ref, g_ref, sh_ref, sc_ref, o_ref):
    o_ref[...] = _modulated(x_ref[...], g_ref[...], sh_ref[...], sc_ref[...])


def _modulate(x, g, shift, scale):
    t, d = x.shape
    tm = _tile(t, 512, PACKED_ROWS)
    vec = pl.BlockSpec((1, d), lambda i: (0, 0))
    return pl.pallas_call(
        _modulate_body,
        grid=(t // tm,),
        in_specs=[pl.BlockSpec((tm, d), lambda i: (i, 0)), vec, vec, vec],
        out_specs=pl.BlockSpec((tm, d), lambda i: (i, 0)),
        out_shape=jax.ShapeDtypeStruct((t, d), BF16),
        compiler_params=_params(dimension_semantics=("arbitrary",)),
        name="modulate",
    )(x, g, shift, scale)


def _in_proj_body(h_ref, w_ref, o_ref, wb_ref):
    @pl.when(pl.program_id(1) == 0)
    def _():
        wb_ref[...] = w_ref[0].astype(BF16)

    half = h_ref.shape[0] // 2
    for r in (slice(0, half), slice(half, 2 * half)):
        o_ref[r, :] = jnp.dot(h_ref[r, :], wb_ref[...], preferred_element_type=F32).astype(o_ref.dtype)


def _col_tile(n):
    for mult in (2 * LANES, LANES):
        try:
            return _tile(n, 1536, mult)
        except ValueError:
            pass
    raise ValueError(n)


def _in_proj(h, w_in, layer, n_cols=None):
    t, d = h.shape
    n = w_in.shape[2]
    tm = _tile(t, 2048, 2 * PACKED_ROWS)
    tn = _col_tile(n)
    n_tiles = n // tn if n_cols is None else pl.cdiv(n_cols, tn)
    return pl.pallas_call(
        _in_proj_body,
        grid=(n_tiles, t // tm),
        in_specs=[pl.BlockSpec((tm, d), lambda j, i: (i, 0)),
                  pl.BlockSpec((1, d, tn), lambda j, i: (layer, 0, j))],
        out_specs=pl.BlockSpec((tm, tn), lambda j, i: (i, j)),
        out_shape=jax.ShapeDtypeStruct((t, n_tiles * tn), BF16),
        scratch_shapes=[pltpu.VMEM((d, tn), BF16)],
        compiler_params=_params(vmem_limit=IN_PROJ_VMEM_LIMIT, dimension_semantics=("arbitrary", "arbitrary")),
        name="in_proj",
    )(h, w_in)


def _head_meansq(x, ones_bd):
    ss = x * x
    hi = ss.astype(BF16)
    lo = (ss - hi.astype(F32)).astype(BF16)
    tot = (jnp.dot(hi, ones_bd, preferred_element_type=F32)
           + jnp.dot(lo, ones_bd, preferred_element_type=F32))
    return tot * (1.0 / HEAD_DIM)


def _rope(y, cos, sin_signed):
    w = y.shape[-1]
    lane = lax.broadcasted_iota(jnp.int32, y.shape, 1)
    nxt = pltpu.roll(y, w - ROPE_HALF, 1)
    prv = pltpu.roll(y, ROPE_HALF, 1)
    return y * cos + jnp.where((lane % (2 * ROPE_HALF)) < ROPE_HALF, nxt, prv) * sin_signed


def _norm_rope(x, w, bd, rope):
    x = x * lax.rsqrt(_head_meansq(x, bd) + EPS) * w
    if rope is not None:
        reps = x.shape[-1] // LANES
        x = _rope(x, jnp.concatenate([rope[0]] * reps, axis=1), jnp.concatenate([rope[1]] * reps, axis=1))
    return x


VT_ROWS = HEAD_DIM + PACKED_ROWS


def _qk_prep_body(*refs, use_rope, with_q):
    refs = list(refs)
    q_ref = refs.pop(0) if with_q else None
    k_ref, v_ref = refs.pop(0), refs.pop(0)
    qn_ref = refs.pop(0) if with_q else None
    kn_ref, bd_ref = refs.pop(0), refs.pop(0)
    rope = (refs.pop(0)[...], refs.pop(0)[...]) if use_rope else None
    bd = bd_ref[...]
    if with_q:
        q = _norm_rope(q_ref[...].astype(F32), qn_ref[...], bd, rope)
        refs.pop(0)[...] = (q * SCORE_SCALE_LOG2).astype(BF16)
    k = _norm_rope(k_ref[...].astype(F32), kn_ref[...], bd[:KV_W, :KV_W], rope)
    refs.pop(0)[...] = k.astype(BF16)
    vt_ref = refs.pop(0)
    vt = v_ref[...].astype(F32).T
    row = lax.broadcasted_iota(jnp.int32, (VT_ROWS - HEAD_DIM, vt.shape[1]), 0)
    tail = jnp.where(row == 0, 1.0, 0.0)
    for g in range(GQA_KV_HEADS):
        vt_ref[0, g] = jnp.concatenate([vt[HEAD_DIM * g:HEAD_DIM * (g + 1)], tail], axis=0).astype(BF16)


def _qk_prep(p, q_off, k_off, v_off, qn, kn, ones_bd, rope_tabs):
    t = p.shape[0]
    tm = _tile(t, 512, LANES)
    use_rope, with_q = rope_tabs is not None, q_off is not None
    const = lambda shape: pl.BlockSpec(shape, lambda i: (0, 0))
    in_specs, args, out_specs, out_shape = [], [], [], []
    if with_q:
        in_specs.append(_slab(tm, BRANCH_W, q_off))
        args.append(p)
    in_specs += [_slab(tm, KV_W, k_off), _slab(tm, KV_W, v_off)]
    args += [p, p]
    if with_q:
        in_specs.append(const((1, BRANCH_W)))
        args.append(qn)
        out_specs.append(pl.BlockSpec((tm, BRANCH_W), lambda i: (i, 0)))
        out_shape.append(jax.ShapeDtypeStruct((t, BRANCH_W), BF16))
    in_specs += [const((1, KV_W)), const((BRANCH_W, BRANCH_W))]
    args += [kn, ones_bd]
    if use_rope:
        in_specs += [pl.BlockSpec((tm, LANES), lambda i: (i, 0))] * 2
        args += list(rope_tabs)
    out_specs += [pl.BlockSpec((tm, KV_W), lambda i: (i, 0)),
                  pl.BlockSpec((1, GQA_KV_HEADS, VT_ROWS, tm), lambda i: (i, 0, 0, 0))]
    out_shape += [jax.ShapeDtypeStruct((t, KV_W), BF16),
                  jax.ShapeDtypeStruct((t // tm, GQA_KV_HEADS, VT_ROWS, tm), BF16)]
    res = pl.pallas_call(
        functools.partial(_qk_prep_body, use_rope=use_rope, with_q=with_q),
        grid=(t // tm,),
        in_specs=in_specs,
        out_specs=out_specs,
        out_shape=out_shape,
        compiler_params=_params(dimension_semantics=("arbitrary",)),
        name="qk_prep",
    )(*args)
    return res if with_q else (None, res[0], res[1])


def _rope_tables(seq):
    half = ROPE_HALF
    t = lax.broadcasted_iota(jnp.int32, (seq, LANES), 0)
    lane = lax.broadcasted_iota(jnp.int32, (seq, LANES), 1)
    pos = jnp.where((lane % HEAD_DIM) < 2 * half, t // GRID_W, t % GRID_W).astype(F32)
    freqs = ROPE_THETA ** (-(lane % half).astype(F32) / half)
    ang = pos * freqs
    sign = jnp.where((lane % (2 * half)) < half, -1.0, 1.0)
    return jnp.cos(ang), jnp.sin(ang) * sign


def _nt_dot(a, b):
    return lax.dot_general(a, b, (((1,), (1,)), ((), ())), preferred_element_type=F32)


def _pair_queries(qp, low, shared_kv_lanes):
    zero = jnp.zeros_like(qp)
    if shared_kv_lanes is None:
        return jnp.where(low, qp, zero), jnp.where(low, zero, qp)
    qr = pltpu.roll(qp, HEAD_DIM, 1)
    if shared_kv_lanes == 0:
        return jnp.where(low, qp, zero), jnp.where(low, qr, zero)
    return jnp.where(low, zero, qr), jnp.where(low, zero, qp)


def _values_with_ones(vc, half):
    lane = lax.broadcasted_iota(jnp.int32, vc.shape, 1)
    keep = (lane < HEAD_DIM) if half == 0 else (lane >= HEAD_DIM)
    ones_lane = HEAD_DIM * (1 - half)
    fill = jnp.where(lane == ones_lane, 1.0, 0.0)
    return jnp.where(keep, vc.astype(F32), fill).astype(vc.dtype)


def _dense_attn_body(*refs, n_src, chunks, kv_grouped, q_scale, tq):
    n_pair = N_HEADS // 2
    q_ref, gate_ref = refs[0], refs[1]
    srcs = [(refs[2 + 2 * i], refs[3 + 2 * i]) for i in range(n_src)]
    o_ref = refs[2 + 2 * n_src]
    scratch = refs[3 + 2 * n_src:]
    m_s, acc_s = scratch[:n_pair], scratch[n_pair:]
    low_q = lax.broadcasted_iota(jnp.int32, (tq, LANES), 1) < HEAD_DIM

    halves = [((2 * p) // (N_HEADS // GQA_KV_HEADS),) * 2 if kv_grouped else (0, 1) for p in range(n_pair)]
    lhs = []
    for p in range(n_pair):
        qp = q_ref[:, LANES * p:LANES * (p + 1)].astype(F32) * q_scale
        qa, qb = _pair_queries(qp, low_q, halves[p][0] if kv_grouped else None)
        lhs.append(jnp.concatenate([qa, qb], axis=0).astype(BF16))
        m_s[p][...] = jnp.full(m_s[p].shape, NEG_BIG, F32)
        acc_s[p][...] = jnp.zeros(acc_s[p].shape, F32)

    def step(p, kc, va, vb):
        s = _nt_dot(lhs[p], kc)
        m_old = m_s[p][...]
        m_new = jnp.maximum(m_old, jnp.max(s, axis=-1, keepdims=True))
        alpha = jnp.exp2(m_old - m_new)
        pr = jnp.exp2(s - m_new).astype(BF16)
        acc = acc_s[p]
        acc[:tq] = alpha[:tq] * acc[:tq] + jnp.dot(pr[:tq], va, preferred_element_type=F32)
        acc[tq:] = alpha[tq:] * acc[tq:] + jnp.dot(pr[tq:], vb, preferred_element_type=F32)
        m_s[p][...] = m_new

    def all_pairs(k_ref, v_ref, rows):
        if kv_grouped:
            kc, vc = k_ref[rows, :], v_ref[rows, :]
            vals = [_values_with_ones(vc, h) for h in range(GQA_KV_HEADS)]
        for p in range(n_pair):
            if kv_grouped:
                step(p, kc, vals[halves[p][0]], vals[halves[p][1]])
            else:
                cols = slice(LANES * p, LANES * (p + 1))
                vc = v_ref[rows, cols]
                step(p, k_ref[rows, cols], _values_with_ones(vc, 0), _values_with_ones(vc, 1))

    for (k_ref, v_ref), ck in zip(srcs, chunks):
        n_chunk = k_ref.shape[0] // ck
        if n_chunk == 1:
            all_pairs(k_ref, v_ref, slice(None))
        else:
            def loop(i, carry, k_ref=k_ref, v_ref=v_ref, ck=ck):
                all_pairs(k_ref, v_ref, pl.ds(pl.multiple_of(i * ck, ck), ck))
                return carry
            lax.fori_loop(0, n_chunk, loop, 0)

    for p in range(n_pair):
        cols = slice(LANES * p, LANES * (p + 1))
        heads = []
        for hd, half in enumerate(halves[p]):
            acc = acc_s[p][hd * tq:(hd + 1) * tq]
            ones_lane = HEAD_DIM * (1 - half)
            o = acc / acc[:, ones_lane:ones_lane + 1]
            heads.append(o if half == hd else pltpu.roll(o, HEAD_DIM, 1))
        o = jnp.where(low_q, heads[0], heads[1])
        o_ref[:, cols] = (o * _silu(gate_ref[:, cols].astype(F32))).astype(BF16)


def _key_chunk(tk):
    for mult in (2 * LANES, LANES, PACKED_ROWS):
        try:
            return _tile(tk, 1024, mult)
        except ValueError:
            pass
    raise ValueError(tk)


def _dense_attn(q, q_off, gate, gate_off, kv_srcs, kv_grouped, q_scale):
    t = q.shape[0]
    tq = _tile(t, 256, PACKED_ROWS)
    kw = KV_W if kv_grouped else BRANCH_W
    in_specs = [_slab(tq, BRANCH_W, q_off), _slab(tq, BRANCH_W, gate_off)]
    args = [q, gate]
    chunks = []
    for k_arr, k_off, v_arr, v_off in kv_srcs:
        tk = k_arr.shape[0]
        assert k_off % kw == 0 and v_off % kw == 0
        in_specs += [_resident((tk, kw), lambda i, _c=k_off // kw: (0, _c)),
                     _resident((tk, kw), lambda i, _c=v_off // kw: (0, _c))]
        args += [k_arr, v_arr]
        chunks.append(_key_chunk(tk))
    n_pair = N_HEADS // 2
    return pl.pallas_call(
        functools.partial(_dense_attn_body, n_src=len(kv_srcs), chunks=tuple(chunks),
                          kv_grouped=kv_grouped, q_scale=q_scale, tq=tq),
        grid=(t // tq,),
        in_specs=in_specs,
        out_specs=pl.BlockSpec((tq, BRANCH_W), lambda i: (i, 0)),
        out_shape=jax.ShapeDtypeStruct((t, BRANCH_W), BF16),
        scratch_shapes=[pltpu.VMEM((2 * tq, 1), F32)] * n_pair + [pltpu.VMEM((2 * tq, LANES), F32)] * n_pair,
        compiler_params=_params(dimension_semantics=("arbitrary",)),
        name="dense_attn",
    )(*args)


def _gqa_body(q_ref, qn_ref, gate_ref, k_ref, vt_ref, kx_ref, vtx_ref, o_ref, *scratch, tq, n_chunk):
    n_pair = N_HEADS // 2
    qt_s, qtn_s, m_s, acc_s, sx_s = (scratch[i * n_pair:(i + 1) * n_pair] for i in range(5))
    s_s = [scratch[5 * n_pair + 2 * p:5 * n_pair + 2 * p + 2] for p in range(n_pair)]
    mx_s = [scratch[7 * n_pair + 3 * p:7 * n_pair + 3 * p + 3] for p in range(n_pair)]
    s_s = [[(s_s[p][slot], mx_s[p][slot]) for slot in range(2)] for p in range(n_pair)]
    sx_s = [(sx_s[p], mx_s[p][2]) for p in range(n_pair)]
    low_q = lax.broadcasted_iota(jnp.int32, (tq, LANES), 1) < HEAD_DIM
    kv_of = [(2 * p) // (N_HEADS // GQA_KV_HEADS) for p in range(n_pair)]

    def scores(p, keys, buf, qt_ref=None):
        s = jnp.dot(keys, (qt_s[p] if qt_ref is None else qt_ref)[...], preferred_element_type=F32)
        buf[0][...] = s
        buf[1][...] = jnp.max(s, axis=0, keepdims=True)

    def consume(p, buf, vt):
        m_old = m_s[p][...]
        m_new = jnp.maximum(m_old, buf[1][...])
        alpha = jnp.exp2(m_old - m_new)
        pt = jnp.exp2(buf[0][...] - m_new).astype(BF16)
        for hd in range(2):
            cols = slice(hd * tq, (hd + 1) * tq)
            acc_s[p][hd] = alpha[:, cols] * acc_s[p][hd] + jnp.dot(vt, pt[:, cols], preferred_element_type=F32)
        m_s[p][...] = m_new

    def step(i, slot):
        for p in range(n_pair):
            scores(p, k_ref[i + 1], s_s[p][1 - slot])
            consume(p, s_s[p][slot], vt_ref[i, kv_of[p]])

    def transposed_queries(p, q_tile_ref):
        qp = q_tile_ref[:, LANES * p:LANES * (p + 1)].astype(F32)
        qa, qb = _pair_queries(qp, low_q, kv_of[p])
        return jnp.concatenate([qa, qb], axis=0).T.astype(BF16)

    is_first = pl.program_id(0) == 0

    @pl.when(is_first)
    def _():
        for p in range(n_pair):
            qt_s[p][...] = transposed_queries(p, q_ref)
            scores(p, k_ref[0], s_s[p][0])

    @pl.when(jnp.logical_not(is_first))
    def _():
        for p in range(n_pair):
            qt_s[p][...] = qtn_s[p][...]

    for p in range(n_pair):
        m_s[p][...] = jnp.full(m_s[p].shape, NEG_BIG, F32)
        acc_s[p][...] = jnp.zeros(acc_s[p].shape, F32)

    def two_steps(j, carry):
        step(2 * j, 0)
        step(2 * j + 1, 1)
        return carry

    n_step = n_chunk - 1
    lax.fori_loop(0, n_step // 2, two_steps, 0)
    last = n_chunk - 1
    if n_step % 2:
        step(last - 1, 0)
    def next_scores(p):
        scores(p, k_ref[0], s_s[p][0], qtn_s[p])

    for p in range(n_pair):
        qtn_s[p][...] = transposed_queries(p, qn_ref)
    for p in range(n_pair):
        scores(p, kx_ref[...], sx_s[p])
        if last % 2:
            next_scores(p)
        consume(p, s_s[p][last % 2], vt_ref[last, kv_of[p]])
    for p in range(n_pair):
        if not last % 2:
            next_scores(p)
        consume(p, sx_s[p], vtx_ref[0, kv_of[p]])
        acc = acc_s[p]
        ot = jnp.concatenate([acc[hd, :HEAD_DIM] / acc[hd, HEAD_DIM:HEAD_DIM + 1] for hd in range(2)], axis=0)
        cols = slice(LANES * p, LANES * (p + 1))
        o_ref[:, cols] = (ot.T * _silu(gate_ref[:, cols].astype(F32))).astype(BF16)


def _gqa_attn(q, gate, gate_off, k, vt, k_extra, vt_extra):
    t = q.shape[0]
    n_chunk, _, _, ck = vt.shape
    tx = k_extra.shape[0]
    assert vt_extra.shape[0] == 1 and k.shape[0] == n_chunk * ck
    tq = _tile(t, 256, LANES)
    n_pair = N_HEADS // 2
    scratch = ([pltpu.VMEM((LANES, 2 * tq), BF16)] * (2 * n_pair) + [pltpu.VMEM((1, 2 * tq), F32)] * n_pair
               + [pltpu.VMEM((2, VT_ROWS, tq), F32)] * n_pair + [pltpu.VMEM((tx, 2 * tq), F32)] * n_pair
               + [pltpu.VMEM((ck, 2 * tq), F32)] * (2 * n_pair) + [pltpu.VMEM((1, 2 * tq), F32)] * (3 * n_pair))
    return pl.pallas_call(
        functools.partial(_gqa_body, tq=tq, n_chunk=n_chunk),
        grid=(t // tq,),
        in_specs=[pl.BlockSpec((tq, BRANCH_W), lambda i: (i, 0)),
                  pl.BlockSpec((tq, BRANCH_W), lambda i: (jnp.minimum(i + 1, t // tq - 1), 0)),
                  _slab(tq, BRANCH_W, gate_off),
                  _resident((n_chunk, ck, KV_W), lambda i: (0, 0, 0)),
                  _resident(vt.shape, lambda i: (0, 0, 0, 0)),
                  _resident((tx, KV_W), lambda i: (0, 0)),
                  _resident(vt_extra.shape, lambda i: (0, 0, 0, 0))],
        out_specs=pl.BlockSpec((tq, BRANCH_W), lambda i: (i, 0)),
        out_shape=jax.ShapeDtypeStruct((t, BRANCH_W), BF16),
        scratch_shapes=scratch,
        compiler_params=_params(dimension_semantics=("arbitrary",)),
        name="gqa_attn",
    )(q, q, gate, k.reshape(n_chunk, ck, KV_W), vt, k_extra, vt_extra)


def _na_bias_tiles(rpb):
    cq = np.arange(GRID_W)
    col_start = np.clip(cq - NA_WIN_COLS // 2, 0, GRID_W - NA_WIN_COLS)
    col_ok = (cq[None, :] >= col_start[:, None]) & (cq[None, :] < col_start[:, None] + NA_WIN_COLS)
    col_off = np.clip(cq[None, :] - cq[:, None], -(NA_WIN_COLS - 1), NA_WIN_COLS - 1) + NA_WIN_COLS - 1
    onehot = jnp.asarray(col_off[:, :, None] == np.arange(2 * NA_WIN_COLS - 1), F32)
    tiles = jnp.einsum("lhrm,qkm->lhrqk", rpb.astype(F32), onehot, precision=lax.Precision.HIGHEST)
    tiles = jnp.where(col_ok, tiles * LOG2_E, NEG_BIG)
    masked = jnp.full(tiles.shape[:2] + (1, GRID_W, GRID_W), NEG_BIG, F32)
    tiles = jnp.concatenate([tiles, masked], axis=2)
    return jnp.concatenate([tiles, tiles], axis=-1)


def _na_body(q_ref, gate_ref, k_ref, v_ref, kc_ref, vc_ref, tiles_ref, o_ref, *, n_rows):
    rb, band = NA_ROW_BLOCK, NA_BAND_ROWS
    tq, nk = rb * GRID_W, band * GRID_W
    b = pl.program_id(0)
    u0 = jnp.clip(b * rb - NA_WIN_ROWS // 2, 0, n_rows - band)
    rows = pl.ds(pl.multiple_of(u0 * GRID_W, GRID_W), nk)
    low_q = lax.broadcasted_iota(jnp.int32, (tq, LANES), 1) < HEAD_DIM
    low_t = lax.broadcasted_iota(jnp.int32, (GRID_W, LANES), 1) < GRID_W

    def bias_rows(h, i):
        r = b * rb + i
        row_start = jnp.clip(r - NA_WIN_ROWS // 2, 0, n_rows - NA_WIN_ROWS)
        pieces = []
        for j in range(band):
            key_row = u0 + j
            in_window = jnp.logical_and(key_row >= row_start, key_row < row_start + NA_WIN_ROWS)
            ro = jnp.where(in_window, key_row - r + NA_WIN_ROWS - 1, 2 * NA_WIN_ROWS - 1)
            pieces.append(tiles_ref[0, h, ro])
        return jnp.concatenate([jnp.where(low_t, pieces[j], pieces[j + 1]) for j in range(0, band, 2)], axis=1)

    for p in range(N_HEADS // 2):
        bias = jnp.concatenate([bias_rows(2 * p + hd, i) for hd in range(2) for i in range(rb)], axis=0)
        cols = slice(LANES * p, LANES * (p + 1))
        qp = q_ref[:, cols].astype(F32) * SCORE_SCALE_LOG2
        qa, qb = _pair_queries(qp, low_q, None)
        lhs = jnp.concatenate([qa, qb], axis=0).astype(BF16)
        kb, vb = k_ref[rows, cols], v_ref[rows, cols]
        kc, vc = kc_ref[:, cols], vc_ref[:, cols]
        s_band = _nt_dot(lhs, kb) + bias
        s_ctx = _nt_dot(lhs, kc)
        m = jnp.maximum(jnp.max(s_band, axis=-1, keepdims=True), jnp.max(s_ctx, axis=-1, keepdims=True))
        pb = jnp.exp2(s_band - m).astype(BF16)
        pc = jnp.exp2(s_ctx - m).astype(BF16)
        heads = []
        for hd in range(2):
            r = slice(hd * tq, (hd + 1) * tq)
            acc = (jnp.dot(pb[r], _values_with_ones(vb, hd), preferred_element_type=F32)
                   + jnp.dot(pc[r], _values_with_ones(vc, hd), preferred_element_type=F32))
            ones_lane = HEAD_DIM * (1 - hd)
            heads.append(acc / acc[:, ones_lane:ones_lane + 1])
        o = jnp.where(low_q, heads[0], heads[1])
        o_ref[:, cols] = (o * _silu(gate_ref[:, cols].astype(F32))).astype(BF16)


def _na_attn(p, off, pc, bias, layer):
    s = p.shape[0]
    cn = pc.shape[0]
    n_rows = s // GRID_W
    tq = NA_ROW_BLOCK * GRID_W
    assert n_rows % NA_ROW_BLOCK == 0 and n_rows >= NA_BAND_ROWS >= NA_ROW_BLOCK + NA_WIN_ROWS - 1
    assert NA_BAND_ROWS % 2 == 0
    nb = n_rows // NA_ROW_BLOCK
    w = BRANCH_W
    return pl.pallas_call(
        functools.partial(_na_body, n_rows=n_rows),
        grid=(nb,),
        in_specs=[_slab(tq, w, off["a_q"]), _slab(tq, w, off["a_gate"]),
                  _resident((s, w), lambda i, _c=off["a_k"] // w: (0, _c)),
                  _resident((s, w), lambda i, _c=off["a_v"] // w: (0, _c)),
                  _resident((cn, w), lambda i, _c=off["a_k"] // w: (0, _c)),
                  _resident((cn, w), lambda i, _c=off["a_v"] // w: (0, _c)),
                  _resident((1,) + bias.shape[1:], lambda i: (layer, 0, 0, 0, 0))],
        out_specs=pl.BlockSpec((tq, w), lambda i: (i, 0)),
        out_shape=jax.ShapeDtypeStruct((s, w), BF16),
        compiler_params=_params(dimension_semantics=("arbitrary",)),
        name="na_attn",
    )(p, p, p, p, pc, pc, bias)


def _halo_specs(t, tm, width, off):
    assert off % LANES == 0 and tm % HALO == 0
    block = lambda rows: (pl.Element(rows), pl.Element(width))
    per, last = tm // HALO, t // HALO - 1
    return [pl.BlockSpec(block(tm), lambda i: (i * tm, off)),
            pl.BlockSpec(block(HALO), lambda i: (jnp.maximum(i * per - 1, 0) * HALO, off)),
            pl.BlockSpec(block(HALO), lambda i: (jnp.minimum((i + 1) * per, last) * HALO, off))]


def _fill_padded(pad_ref, cur, prev, nxt, tm):
    i, n = pl.program_id(0), pl.num_programs(0)
    pad_ref[0:HALO] = jnp.where(i > 0, prev, jnp.zeros_like(prev))
    pad_ref[HALO:HALO + tm] = cur
    pad_ref[HALO + tm:] = jnp.where(i < n - 1, nxt, jnp.zeros_like(nxt))


def _pool_body(u_ref, up_ref, un_ref, gate_ref, w_ref, sc_ref, o_ref, pad_ref, *, tm, seq):
    _fill_padded(pad_ref, u_ref[...].astype(F32), up_ref[...].astype(F32), un_ref[...].astype(F32), tm)
    t = pl.program_id(0) * tm + lax.broadcasted_iota(jnp.int32, (tm, POOL_GROUP), 0)
    for gi, ksz in enumerate(POOL_SIZES):
        cols = slice(POOL_GROUP * gi, POOL_GROUP * (gi + 1))
        back = ksz // 2
        tot = pad_ref[HALO - back:HALO - back + tm, cols]
        for d in range(1 - back, ksz - back):
            tot = tot + pad_ref[HALO + d:HALO + d + tm, cols]
        lo = jnp.maximum(t - back, 0)
        hi = jnp.minimum(t + (ksz - 1 - back), seq - 1)
        mean = tot / (hi - lo + 1).astype(F32)
        dlt = (mean - pad_ref[HALO:HALO + tm, cols]).astype(BF16)
        y = jnp.dot(dlt, w_ref[0, gi].astype(BF16), preferred_element_type=F32) * sc_ref[:, cols]
        o_ref[:, cols] = (y * _silu(gate_ref[:, cols].astype(F32))).astype(BF16)


def _pool(p, off, w_pool, layer, pool_scale):
    t = p.shape[0]
    tm = _tile(t, 512, HALO)
    w = BRANCH_W
    return pl.pallas_call(
        functools.partial(_pool_body, tm=tm, seq=t),
        grid=(t // tm,),
        in_specs=_halo_specs(t, tm, w, off["b_in"]) + [
            _slab(tm, w, off["b_gate"]),
            pl.BlockSpec((1,) + w_pool.shape[1:], lambda i: (layer, 0, 0, 0)),
            pl.BlockSpec((1, w), lambda i: (0, 0))],
        out_specs=pl.BlockSpec((tm, w), lambda i: (i, 0)),
        out_shape=jax.ShapeDtypeStruct((t, w), BF16),
        scratch_shapes=[pltpu.VMEM((tm + 2 * HALO, w), F32)],
        compiler_params=_params(dimension_semantics=("arbitrary",)),
        name="pool",
    )(p, p, p, p, w_pool, pool_scale)


def _glu(x):
    x = x.astype(F32)
    return x[:, :BRANCH_W] * jax.nn.sigmoid(x[:, BRANCH_W:])


def _conv_body(x_ref, xp_ref, xn_ref, gate_ref, cw_ref, cb_ref, lg_ref, lb_ref, pw_ref, o_ref, pad_ref, sh_ref,
               *, tm):
    _fill_padded(pad_ref, _glu(x_ref[...]), _glu(xp_ref[...]), _glu(xn_ref[...]), tm)
    span = sh_ref.shape[1]
    for b in range(1, SUBLANES):
        sh_ref[b - 1] = pad_ref[b:b + span, :]
    reach = CONV_WIDTH // 2
    y = jnp.zeros((tm, BRANCH_W), F32) + cb_ref[...]
    for j in range(CONV_WIDTH):
        a, b = divmod(HALO - reach + j, SUBLANES)
        rows = slice(SUBLANES * a, SUBLANES * a + tm)
        y = y + (pad_ref[rows, :] if b == 0 else sh_ref[b - 1, rows, :]) * cw_ref[j:j + 1, :]
    mu = jnp.mean(y, axis=-1, keepdims=True)
    yc = y - mu
    var = jnp.mean(yc * yc, axis=-1, keepdims=True)
    z = _silu(yc * lax.rsqrt(var + EPS) * lg_ref[...] + lb_ref[...]).astype(BF16)
    out = jnp.dot(z, pw_ref[0].astype(BF16), preferred_element_type=F32)
    o_ref[...] = (out * _silu(gate_ref[...].astype(F32))).astype(BF16)


def _conv(p, off, conv_w, conv_b, ln_g, ln_b, w_pw, layer):
    t = p.shape[0]
    tm = _tile(t, 512, HALO)
    w = BRANCH_W
    vec = pl.BlockSpec((1, w), lambda i: (0, 0))
    return pl.pallas_call(
        functools.partial(_conv_body, tm=tm),
        grid=(t // tm,),
        in_specs=_halo_specs(t, tm, 2 * w, off["d_glu"]) + [
            _slab(tm, w, off["d_gate"]),
            pl.BlockSpec(conv_w.shape, lambda i: (0, 0)), vec, vec, vec,
            pl.BlockSpec((1, w, w), lambda i: (layer, 0, 0))],
        out_specs=pl.BlockSpec((tm, w), lambda i: (i, 0)),
        out_shape=jax.ShapeDtypeStruct((t, w), BF16),
        scratch_shapes=[pltpu.VMEM((tm + 2 * HALO, w), F32),
                        pltpu.VMEM((SUBLANES - 1, tm + 2 * HALO - SUBLANES, w), F32)],
        compiler_params=_params(dimension_semantics=("arbitrary",)),
        name="conv",
    )(p, p, p, p, conv_w, conv_b, ln_g, ln_b, w_pw)


def _cast_once(w_ref, wb_ref):
    @pl.when(pl.program_id(0) == 0)
    def _():
        wb_ref[...] = w_ref[0].astype(BF16)


def _merge_body(m0, m1, m2, m3, o0, o1, o2, o3, w_ref, y_ref, wb_ref):
    _cast_once(w_ref, wb_ref)
    y = None
    for bi, (m_ref, o_ref) in enumerate(((m0, o0), (m1, o1), (m2, o2), (m3, o3))):
        proj = jnp.dot(o_ref[...], wb_ref[bi], preferred_element_type=F32)
        term = (1.0 + jnp.tanh(0.5 * m_ref[...].astype(F32))) * proj
        y = term if y is None else y + term
    y_ref[...] = (0.5 * y).astype(BF16)


def _merge(p, merge_off, outs, w_branch, layer):
    t = p.shape[0]
    d = w_branch.shape[-1]
    tm = _tile(t, 256, PACKED_ROWS)
    return pl.pallas_call(
        _merge_body,
        grid=(t // tm,),
        in_specs=[_slab(tm, d, merge_off + bi * d) for bi in range(N_BRANCH)]
        + [pl.BlockSpec((tm, BRANCH_W), lambda i: (i, 0))] * N_BRANCH
        + [_resident((1,) + w_branch.shape[1:], lambda i: (layer, 0, 0, 0))],
        out_specs=pl.BlockSpec((tm, d), lambda i: (i, 0)),
        out_shape=jax.ShapeDtypeStruct((t, d), BF16),
        scratch_shapes=[pltpu.VMEM(w_branch.shape[1:], BF16)],
        compiler_params=_params(dimension_semantics=("arbitrary",)),
        name="merge",
    )(p, p, p, p, *outs, w_branch)


def _out_body(y_ref, w_ref, x_ref, g_ref, gate_ref, *rest):
    *rest, wb_ref = rest
    _cast_once(w_ref, wb_ref)
    y = jnp.dot(y_ref[...], wb_ref[...], preferred_element_type=F32)
    yn = y * lax.rsqrt(jnp.mean(y * y, axis=-1, keepdims=True) + EPS) * g_ref[...]
    x_next = x_ref[...] + gate_ref[...] * yn
    if len(rest) == 1:
        rest[0][...] = x_next
    else:
        gn_ref, sh_ref, sc_ref, o_ref, h_ref = rest
        o_ref[...] = x_next
        h_ref[...] = _modulated(x_next, gn_ref[...], sh_ref[...], sc_ref[...])


def _out_proj(y, w_out, layer, x, g_post, gate, next_mod=None):
    t, d = x.shape
    tm = _tile(t, 256, PACKED_ROWS)
    vec = pl.BlockSpec((1, d), lambda i: (0, 0))
    row = pl.BlockSpec((tm, d), lambda i: (i, 0))
    n_next = 0 if next_mod is None else 1
    out = pl.pallas_call(
        _out_body,
        grid=(t // tm,),
        in_specs=[row, _resident((1, d, d), lambda i: (layer, 0, 0)), row, vec, vec] + [vec] * (3 * n_next),
        scratch_shapes=[pltpu.VMEM((d, d), BF16)],
        out_specs=[row] * (1 + n_next),
        out_shape=[jax.ShapeDtypeStruct((t, d), F32)] + [jax.ShapeDtypeStruct((t, d), BF16)] * n_next,
        compiler_params=_params(dimension_semantics=("arbitrary",)),
        name="out_proj",
    )(y, w_out, x, g_post, gate, *(next_mod or ()))
    return out if n_next else (out[0], None)


def kernel(x, c, ctx, c_ctx, w_ada, b_ada, g_pre, g_post, w_in, na_rpb, pool_w, pool_scale,
           q_norm, k_norm, conv_w, conv_b, conv_ln_g, conv_ln_b, conv_pw, w_branch, w_out):
    batch, seq, d = x.shape
    assert batch == 1 and seq % GRID_W == 0
    cn = ctx.shape[1]
    depth = w_ada.shape[0]
    off = _layout(d)

    xs, cs = x[0], ctx[0]
    cvec = jnp.zeros((8, d), F32).at[0].set(c[0]).at[1].set(c_ctx)
    ada = _ada(cvec, w_ada, b_ada)

    ones_bd = jnp.asarray(np.kron(np.eye(N_HEADS), np.ones((HEAD_DIM, HEAD_DIM))), BF16)
    rope_tabs = _rope_tables(seq)
    na_bias = _na_bias_tiles(na_rpb)
    row = lambda v: v.reshape(1, -1)
    qn_all = jnp.tile(q_norm, (1, N_HEADS))
    kn_all = jnp.tile(k_norm, (1, GQA_KV_HEADS))

    mod = lambda l, r: tuple(ada[l, r:r + 1, k * d:(k + 1) * d] for k in range(3))
    h = _modulate(xs, row(g_pre[0]), *mod(0, 0)[:2])
    hc = _modulate(cs, row(g_pre[0]), *mod(0, 1)[:2])
    for l in range(depth):
        last = l == depth - 1
        conv_args = (conv_w[l], row(conv_b[l]), row(conv_ln_g[l]), row(conv_ln_b[l]), conv_pw, l)
        gate, gate_c = mod(l, 0)[2], mod(l, 1)[2]
        next_mod = lambda r: None if last else (row(g_pre[l + 1]),) + mod(l + 1, r)[:2]
        qn, kn = row(qn_all[l]), row(kn_all[l])

        pc = _in_proj(hc, w_in, l, KV_COLS if last else None)
        qc_ctx, kq_ctx, vt_ctx = _qk_prep(pc, None if last else off["c_q"], off["c_k"], off["c_v"],
                                          qn, kn, ones_bd, None)

        p = _in_proj(h, w_in, l)
        qc, kc, vt = _qk_prep(p, off["c_q"], off["c_k"], off["c_v"], qn, kn, ones_bd, rope_tabs)
        o_c = _gqa_attn(qc, p, off["c_gate"], kc, vt, kq_ctx, vt_ctx)
        o_a = _na_attn(p, off, pc, na_bias, l)
        o_b = _pool(p, off, pool_w, l, row(pool_scale[l]))
        o_d = _conv(p, off, *conv_args)
        y = _merge(p, off["merge"], (o_a, o_b, o_c, o_d), w_branch, l)
        xs, h = _out_proj(y, w_out, l, xs, row(g_post[l]), gate, next_mod(0))

        if not last:
            o_a_c = _dense_attn(pc, off["a_q"], pc, off["a_gate"],
                                [(pc, off["a_k"], pc, off["a_v"])], False, SCORE_SCALE_LOG2)
            o_c_c = _dense_attn(qc_ctx, 0, pc, off["c_gate"], [(kq_ctx, 0, pc, off["c_v"])], True, 1.0)
            o_b_c = _pool(pc, off, pool_w, l, row(pool_scale[l]))
            o_d_c = _conv(pc, off, *conv_args)
            y_c = _merge(pc, off["merge"], (o_a_c, o_b_c, o_c_c, o_d_c), w_branch, l)
            cs, hc = _out_proj(y_c, w_out, l, cs, row(g_post[l]), gate_c, next_mod(1))
    return xs[None]
```

```python
import functools

import numpy as np
import jax
import jax.numpy as jnp
from jax import lax
from jax.experimental import pallas as pl
from jax.experimental.pallas import tpu as pltpu

F32 = jnp.float32
BF16 = jnp.bfloat16

GRID_W = 64
HEAD_DIM = 64
BRANCH_W = 512
N_BRANCH = 4
N_HEADS = BRANCH_W // HEAD_DIM
GQA_KV_HEADS = 2
KV_W = GQA_KV_HEADS * HEAD_DIM
NA_WIN_ROWS = 8
NA_WIN_COLS = 16
POOL_SIZES = (2, 4, 8, 16)
POOL_GROUP = BRANCH_W // len(POOL_SIZES)
ROPE_THETA = 10000.0
CONV_WIDTH = 31
EPS = 1e-6

LANES = 128
SUBLANES = 8
PACKED_ROWS = 16
HALO = PACKED_ROWS
ROPE_HALF = HEAD_DIM // 4
NA_ROW_BLOCK = 4
NA_BAND_ROWS = 12
NEG_BIG = -1e30
LOG2_E = 1.4426950408889634
SCORE_SCALE_LOG2 = HEAD_DIM ** -0.5 * LOG2_E
V7X_VMEM_BYTES = 64 * 1024 * 1024
V7X_VMEM_LIMIT = V7X_VMEM_BYTES * 13 // 16
IN_PROJ_VMEM_LIMIT = V7X_VMEM_BYTES * 29 // 32

PARTS = (("a_k", BRANCH_W), ("a_v", BRANCH_W), ("c_k", KV_W), ("c_v", KV_W), ("a_q", BRANCH_W), ("c_q", BRANCH_W),
         ("a_gate", BRANCH_W), ("b_in", BRANCH_W), ("b_gate", BRANCH_W), ("c_gate", BRANCH_W),
         ("d_glu", 2 * BRANCH_W), ("d_gate", BRANCH_W), ("merge", None))
KV_COLS = 2 * BRANCH_W + 2 * KV_W


def _layout(d_model):
    off, o = {}, 0
    for n, w in PARTS:
        off[n] = o
        o += N_BRANCH * d_model if w is None else w
    return off


def _params(vmem_limit=V7X_VMEM_LIMIT, **kw):
    return pltpu.CompilerParams(vmem_limit_bytes=vmem_limit, **kw)


def _tile(n, pref, mult):
    if n <= pref:
        return n
    t = (pref // mult) * mult
    while t >= mult:
        if n % t == 0:
            return t
        t -= mult
    raise ValueError(f"no tile for {n}")


def _resident(block_shape, index_map):
    return pl.BlockSpec(block_shape, index_map, pipeline_mode=pl.Buffered(1))


def _slab(tm, width, off):
    assert off % LANES == 0
    return pl.BlockSpec((pl.Element(tm), pl.Element(width)), lambda i: (i * tm, off))


def _silu(x):
    return x * jax.nn.sigmoid(x)


def _ada_body(cs_ref, w_ref, b_ref, o_ref):
    s = _silu(cs_ref[...]).astype(BF16)
    o_ref[0] = jnp.dot(s, w_ref[0].astype(BF16), preferred_element_type=F32) + b_ref[0]


def _ada(cs, w_ada, b_ada):
    depth, d, n = w_ada.shape
    tn = _tile(n, 1024, LANES)
    return pl.pallas_call(
        _ada_body,
        grid=(depth, n // tn),
        in_specs=[pl.BlockSpec((SUBLANES, d), lambda l, j: (0, 0)),
                  pl.BlockSpec((1, d, tn), lambda l, j: (l, 0, j)),
                  pl.BlockSpec((1, 1, tn), lambda l, j: (l, 0, j))],
        out_specs=pl.BlockSpec((1, SUBLANES, tn), lambda l, j: (l, 0, j)),
        out_shape=jax.ShapeDtypeStruct((depth, SUBLANES, n), F32),
        compiler_params=_params(dimension_semantics=("arbitrary", "arbitrary")),
        name="ada",
    )(cs, w_ada, b_ada.reshape(depth, 1, n))


def _modulated(x, g, shift, scale):
    y = x * lax.rsqrt(jnp.mean(x * x, axis=-1, keepdims=True) + EPS) * g
    return (y * (1.0 + scale) + shift).astype(BF16)


def _modulate_body(x_ref, g_ref, sh_ref, sc_ref, o_ref):
    o_ref[...] = _modulated(x_ref[...], g_ref[...], sh_ref[...], sc_ref[...])


def _modulate(x, g, shift, scale):
    t, d = x.shape
    tm = _tile(t, 512, PACKED_ROWS)
    vec = pl.BlockSpec((1, d), lambda i: (0, 0))
    return pl.pallas_call(
        _modulate_body,
        grid=(t // tm,),
        in_specs=[pl.BlockSpec((tm, d), lambda i: (i, 0)), vec, vec, vec],
        out_specs=pl.BlockSpec((tm, d), lambda i: (i, 0)),
        out_shape=jax.ShapeDtypeStruct((t, d), BF16),
        compiler_params=_params(dimension_semantics=("arbitrary",)),
        name="modulate",
    )(x, g, shift, scale)


def _in_proj_body(h_ref, w_ref, o_ref, wb_ref):
    @pl.when(pl.program_id(1) == 0)
    def _():
        wb_ref[...] = w_ref[0].astype(BF16)

    half = h_ref.shape[0] // 2
    for r in (slice(0, half), slice(half, 2 * half)):
        o_ref[r, :] = jnp.dot(h_ref[r, :], wb_ref[...], preferred_element_type=F32).astype(o_ref.dtype)


def _col_tile(n):
    for mult in (2 * LANES, LANES):
        try:
            return _tile(n, 1536, mult)
        except ValueError:
            pass
    raise ValueError(n)


def _in_proj(h, w_in, layer, n_cols=None):
    t, d = h.shape
    n = w_in.shape[2]
    tm = _tile(t, 2048, 2 * PACKED_ROWS)
    tn = _col_tile(n)
    n_tiles = n // tn if n_cols is None else pl.cdiv(n_cols, tn)
    return pl.pallas_call(
        _in_proj_body,
        grid=(n_tiles, t // tm),
        in_specs=[pl.BlockSpec((tm, d), lambda j, i: (i, 0)),
                  pl.BlockSpec((1, d, tn), lambda j, i: (layer, 0, j))],
        out_specs=pl.BlockSpec((tm, tn), lambda j, i: (i, j)),
        out_shape=jax.ShapeDtypeStruct((t, n_tiles * tn), BF16),
        scratch_shapes=[pltpu.VMEM((d, tn), BF16)],
        compiler_params=_params(vmem_limit=IN_PROJ_VMEM_LIMIT, dimension_semantics=("arbitrary", "arbitrary")),
        name="in_proj",
    )(h, w_in)


def _head_meansq(x, ones_bd):
    ss = x * x
    hi = ss.astype(BF16)
    lo = (ss - hi.astype(F32)).astype(BF16)
    tot = (jnp.dot(hi, ones_bd, preferred_element_type=F32)
           + jnp.dot(lo, ones_bd, preferred_element_type=F32))
    return tot * (1.0 / HEAD_DIM)


def _rope(y, cos, sin_signed):
    w = y.shape[-1]
    lane = lax.broadcasted_iota(jnp.int32, y.shape, 1)
    nxt = pltpu.roll(y, w - ROPE_HALF, 1)
    prv = pltpu.roll(y, ROPE_HALF, 1)
    return y * cos + jnp.where((lane % (2 * ROPE_HALF)) < ROPE_HALF, nxt, prv) * sin_signed


def _norm_rope(x, w, bd, rope):
    x = x * lax.rsqrt(_head_meansq(x, bd) + EPS) * w
    if rope is not None:
        reps = x.shape[-1] // LANES
        x = _rope(x, jnp.concatenate([rope[0]] * reps, axis=1), jnp.concatenate([rope[1]] * reps, axis=1))
    return x


VT_ROWS = HEAD_DIM + PACKED_ROWS


def _qk_prep_body(*refs, use_rope, with_q):
    refs = list(refs)
    q_ref = refs.pop(0) if with_q else None
    k_ref, v_ref = refs.pop(0), refs.pop(0)
    qn_ref = refs.pop(0) if with_q else None
    kn_ref, bd_ref = refs.pop(0), refs.pop(0)
    rope = (refs.pop(0)[...], refs.pop(0)[...]) if use_rope else None
    bd = bd_ref[...]
    if with_q:
        q = _norm_rope(q_ref[...].astype(F32), qn_ref[...], bd, rope)
        refs.pop(0)[...] = (q * SCORE_SCALE_LOG2).astype(BF16)
    k = _norm_rope(k_ref[...].astype(F32), kn_ref[...], bd[:KV_W, :KV_W], rope)
    refs.pop(0)[...] = k.astype(BF16)
    vt_ref = refs.pop(0)
    vt = v_ref[...].astype(F32).T
    row = lax.broadcasted_iota(jnp.int32, (VT_ROWS - HEAD_DIM, vt.shape[1]), 0)
    tail = jnp.where(row == 0, 1.0, 0.0)
    for g in range(GQA_KV_HEADS):
        vt_ref[0, g] = jnp.concatenate([vt[HEAD_DIM * g:HEAD_DIM * (g + 1)], tail], axis=0).astype(BF16)


def _qk_prep(p, q_off, k_off, v_off, qn, kn, ones_bd, rope_tabs):
    t = p.shape[0]
    tm = _tile(t, 512, LANES)
    use_rope, with_q = rope_tabs is not None, q_off is not None
    const = lambda shape: pl.BlockSpec(shape, lambda i: (0, 0))
    in_specs, args, out_specs, out_shape = [], [], [], []
    if with_q:
        in_specs.append(_slab(tm, BRANCH_W, q_off))
        args.append(p)
    in_specs += [_slab(tm, KV_W, k_off), _slab(tm, KV_W, v_off)]
    args += [p, p]
    if with_q:
        in_specs.append(const((1, BRANCH_W)))
        args.append(qn)
        out_specs.append(pl.BlockSpec((tm, BRANCH_W), lambda i: (i, 0)))
        out_shape.append(jax.ShapeDtypeStruct((t, BRANCH_W), BF16))
    in_specs += [const((1, KV_W)), const((BRANCH_W, BRANCH_W))]
    args += [kn, ones_bd]
    if use_rope:
        in_specs += [pl.BlockSpec((tm, LANES), lambda i: (i, 0))] * 2
        args += list(rope_tabs)
    out_specs += [pl.BlockSpec((tm, KV_W), lambda i: (i, 0)),
                  pl.BlockSpec((1, GQA_KV_HEADS, VT_ROWS, tm), lambda i: (i, 0, 0, 0))]
    out_shape += [jax.ShapeDtypeStruct((t, KV_W), BF16),
                  jax.ShapeDtypeStruct((t // tm, GQA_KV_HEADS, VT_ROWS, tm), BF16)]
    res = pl.pallas_call(
        functools.partial(_qk_prep_body, use_rope=use_rope, with_q=with_q),
        grid=(t // tm,),
        in_specs=in_specs,
        out_specs=out_specs,
        out_shape=out_shape,
        compiler_params=_params(dimension_semantics=("arbitrary",)),
        name="qk_prep",
    )(*args)
    return res if with_q else (None, res[0], res[1])


def _rope_tables(seq):
    half = ROPE_HALF
    t = lax.broadcasted_iota(jnp.int32, (seq, LANES), 0)
    lane = lax.broadcasted_iota(jnp.int32, (seq, LANES), 1)
    pos = jnp.where((lane % HEAD_DIM) < 2 * half, t // GRID_W, t % GRID_W).astype(F32)
    freqs = ROPE_THETA ** (-(lane % half).astype(F32) / half)
    ang = pos * freqs
    sign = jnp.where((lane % (2 * half)) < half, -1.0, 1.0)
    return jnp.cos(ang), jnp.sin(ang) * sign


def _nt_dot(a, b):
    return lax.dot_general(a, b, (((1,), (1,)), ((), ())), preferred_element_type=F32)


def _pair_queries(qp, low, shared_kv_lanes):
    zero = jnp.zeros_like(qp)
    if shared_kv_lanes is None:
        return jnp.where(low, qp, zero), jnp.where(low, zero, qp)
    qr = pltpu.roll(qp, HEAD_DIM, 1)
    if shared_kv_lanes == 0:
        return jnp.where(low, qp, zero), jnp.where(low, qr, zero)
    return jnp.where(low, zero, qr), jnp.where(low, zero, qp)


def _values_with_ones(vc, half):
    lane = lax.broadcasted_iota(jnp.int32, vc.shape, 1)
    keep = (lane < HEAD_DIM) if half == 0 else (lane >= HEAD_DIM)
    ones_lane = HEAD_DIM * (1 - half)
    fill = jnp.where(lane == ones_lane, 1.0, 0.0)
    return jnp.where(keep, vc.astype(F32), fill).astype(vc.dtype)


def _dense_attn_body(*refs, n_src, chunks, kv_grouped, q_scale, tq):
    n_pair = N_HEADS // 2
    q_ref, gate_ref = refs[0], refs[1]
    srcs = [(refs[2 + 2 * i], refs[3 + 2 * i]) for i in range(n_src)]
    o_ref = refs[2 + 2 * n_src]
    scratch = refs[3 + 2 * n_src:]
    m_s, acc_s = scratch[:n_pair], scratch[n_pair:]
    low_q = lax.broadcasted_iota(jnp.int32, (tq, LANES), 1) < HEAD_DIM

    halves = [((2 * p) // (N_HEADS // GQA_KV_HEADS),) * 2 if kv_grouped else (0, 1) for p in range(n_pair)]
    lhs = []
    for p in range(n_pair):
        qp = q_ref[:, LANES * p:LANES * (p + 1)].astype(F32) * q_scale
        qa, qb = _pair_queries(qp, low_q, halves[p][0] if kv_grouped else None)
        lhs.append(jnp.concatenate([qa, qb], axis=0).astype(BF16))
        m_s[p][...] = jnp.full(m_s[p].shape, NEG_BIG, F32)
        acc_s[p][...] = jnp.zeros(acc_s[p].shape, F32)

    def step(p, kc, va, vb):
        s = _nt_dot(lhs[p], kc)
        m_old = m_s[p][...]
        m_new = jnp.maximum(m_old, jnp.max(s, axis=-1, keepdims=True))
        alpha = jnp.exp2(m_old - m_new)
        pr = jnp.exp2(s - m_new).astype(BF16)
        acc = acc_s[p]
        acc[:tq] = alpha[:tq] * acc[:tq] + jnp.dot(pr[:tq], va, preferred_element_type=F32)
        acc[tq:] = alpha[tq:] * acc[tq:] + jnp.dot(pr[tq:], vb, preferred_element_type=F32)
        m_s[p][...] = m_new

    def all_pairs(k_ref, v_ref, rows):
        if kv_grouped:
            kc, vc = k_ref[rows, :], v_ref[rows, :]
            vals = [_values_with_ones(vc, h) for h in range(GQA_KV_HEADS)]
        for p in range(n_pair):
            if kv_grouped:
                step(p, kc, vals[halves[p][0]], vals[halves[p][1]])
            else:
                cols = slice(LANES * p, LANES * (p + 1))
                vc = v_ref[rows, cols]
                step(p, k_ref[rows, cols], _values_with_ones(vc, 0), _values_with_ones(vc, 1))

    for (k_ref, v_ref), ck in zip(srcs, chunks):
        n_chunk = k_ref.shape[0] // ck
        if n_chunk == 1:
            all_pairs(k_ref, v_ref, slice(None))
        else:
            def loop(i, carry, k_ref=k_ref, v_ref=v_ref, ck=ck):
                all_pairs(k_ref, v_ref, pl.ds(pl.multiple_of(i * ck, ck), ck))
                return carry
            lax.fori_loop(0, n_chunk, loop, 0)

    for p in range(n_pair):
        cols = slice(LANES * p, LANES * (p + 1))
        heads = []
        for hd, half in enumerate(halves[p]):
            acc = acc_s[p][hd * tq:(hd + 1) * tq]
            ones_lane = HEAD_DIM * (1 - half)
            o = acc / acc[:, ones_lane:ones_lane + 1]
            heads.append(o if half == hd else pltpu.roll(o, HEAD_DIM, 1))
        o = jnp.where(low_q, heads[0], heads[1])
        o_ref[:, cols] = (o * _silu(gate_ref[:, cols].astype(F32))).astype(BF16)


def _key_chunk(tk):
    for mult in (2 * LANES, LANES, PACKED_ROWS):
        try:
            return _tile(tk, 1024, mult)
        except ValueError:
            pass
    raise ValueError(tk)


def _dense_attn(q, q_off, gate, gate_off, kv_srcs, kv_grouped, q_scale):
    t = q.shape[0]
    tq = _tile(t, 256, PACKED_ROWS)
    kw = KV_W if kv_grouped else BRANCH_W
    in_specs = [_slab(tq, BRANCH_W, q_off), _slab(tq, BRANCH_W, gate_off)]
    args = [q, gate]
    chunks = []
    for k_arr, k_off, v_arr, v_off in kv_srcs:
        tk = k_arr.shape[0]
        assert k_off % kw == 0 and v_off % kw == 0
        in_specs += [_resident((tk, kw), lambda i, _c=k_off // kw: (0, _c)),
                     _resident((tk, kw), lambda i, _c=v_off // kw: (0, _c))]
        args += [k_arr, v_arr]
        chunks.append(_key_chunk(tk))
    n_pair = N_HEADS // 2
    return pl.pallas_call(
        functools.partial(_dense_attn_body, n_src=len(kv_srcs), chunks=tuple(chunks),
                          kv_grouped=kv_grouped, q_scale=q_scale, tq=tq),
        grid=(t // tq,),
        in_specs=in_specs,
        out_specs=pl.BlockSpec((tq, BRANCH_W), lambda i: (i, 0)),
        out_shape=jax.ShapeDtypeStruct((t, BRANCH_W), BF16),
        scratch_shapes=[pltpu.VMEM((2 * tq, 1), F32)] * n_pair + [pltpu.VMEM((2 * tq, LANES), F32)] * n_pair,
        compiler_params=_params(dimension_semantics=("arbitrary",)),
        name="dense_attn",
    )(*args)


def _gqa_body(q_ref, qn_ref, gate_ref, k_ref, vt_ref, kx_ref, vtx_ref, o_ref, *scratch, tq, n_chunk):
    n_pair = N_HEADS // 2
    qt_s, qtn_s, m_s, acc_s, sx_s = (scratch[i * n_pair:(i + 1) * n_pair] for i in range(5))
    s_s = [scratch[5 * n_pair + 2 * p:5 * n_pair + 2 * p + 2] for p in range(n_pair)]
    mx_s = [scratch[7 * n_pair + 3 * p:7 * n_pair + 3 * p + 3] for p in range(n_pair)]
    s_s = [[(s_s[p][slot], mx_s[p][slot]) for slot in range(2)] for p in range(n_pair)]
    sx_s = [(sx_s[p], mx_s[p][2]) for p in range(n_pair)]
    low_q = lax.broadcasted_iota(jnp.int32, (tq, LANES), 1) < HEAD_DIM
    kv_of = [(2 * p) // (N_HEADS // GQA_KV_HEADS) for p in range(n_pair)]

    def scores(p, keys, buf, qt_ref=None):
        s = jnp.dot(keys, (qt_s[p] if qt_ref is None else qt_ref)[...], preferred_element_type=F32)
        buf[0][...] = s
        buf[1][...] = jnp.max(s, axis=0, keepdims=True)

    def consume(p, buf, vt):
        m_old = m_s[p][...]
        m_new = jnp.maximum(m_old, buf[1][...])
        alpha = jnp.exp2(m_old - m_new)
        pt = jnp.exp2(buf[0][...] - m_new).astype(BF16)
        acc_s[p][...] = alpha * acc_s[p][...] + jnp.dot(vt, pt, preferred_element_type=F32)
        m_s[p][...] = m_new

    def step(i, slot):
        for p in range(n_pair):
            scores(p, k_ref[i + 1], s_s[p][1 - slot])
            consume(p, s_s[p][slot], vt_ref[i, kv_of[p]])

    def transposed_queries(p, q_tile_ref):
        qp = q_tile_ref[:, LANES * p:LANES * (p + 1)].astype(F32)
        qa, qb = _pair_queries(qp, low_q, kv_of[p])
        return jnp.concatenate([qa, qb], axis=0).T.astype(BF16)

    is_first = pl.program_id(0) == 0

    @pl.when(is_first)
    def _():
        for p in range(n_pair):
            qt_s[p][...] = transposed_queries(p, q_ref)
            scores(p, k_ref[0], s_s[p][0])

    @pl.when(jnp.logical_not(is_first))
    def _():
        for p in range(n_pair):
            qt_s[p][...] = qtn_s[p][...]

    for p in range(n_pair):
        m_s[p][...] = jnp.full(m_s[p].shape, NEG_BIG, F32)
        acc_s[p][...] = jnp.zeros(acc_s[p].shape, F32)

    def two_steps(j, carry):
        step(2 * j, 0)
        step(2 * j + 1, 1)
        return carry

    n_step = n_chunk - 1
    lax.fori_loop(0, n_step // 2, two_steps, 0)
    last = n_chunk - 1
    if n_step % 2:
        step(last - 1, 0)
    def next_scores(p):
        scores(p, k_ref[0], s_s[p][0], qtn_s[p])

    for p in range(n_pair):
        qtn_s[p][...] = transposed_queries(p, qn_ref)
    for p in range(n_pair):
        scores(p, kx_ref[...], sx_s[p])
        if last % 2:
            next_scores(p)
        consume(p, s_s[p][last % 2], vt_ref[last, kv_of[p]])
    for p in range(n_pair):
        if not last % 2:
            next_scores(p)
        consume(p, sx_s[p], vtx_ref[0, kv_of[p]])
        acc = acc_s[p]
        ot = acc[:HEAD_DIM] / acc[HEAD_DIM:HEAD_DIM + 1]
        ot = jnp.concatenate([ot[:, :tq], ot[:, tq:]], axis=0)
        cols = slice(LANES * p, LANES * (p + 1))
        o_ref[:, cols] = (ot.T * _silu(gate_ref[:, cols].astype(F32))).astype(BF16)


def _gqa_attn(q, gate, gate_off, k, vt, k_extra, vt_extra):
    t = q.shape[0]
    n_chunk, _, _, ck = vt.shape
    tx = k_extra.shape[0]
    assert vt_extra.shape[0] == 1 and k.shape[0] == n_chunk * ck
    tq = _tile(t, 256, LANES)
    n_pair = N_HEADS // 2
    scratch = ([pltpu.VMEM((LANES, 2 * tq), BF16)] * (2 * n_pair) + [pltpu.VMEM((1, 2 * tq), F32)] * n_pair
               + [pltpu.VMEM((VT_ROWS, 2 * tq), F32)] * n_pair + [pltpu.VMEM((tx, 2 * tq), F32)] * n_pair
               + [pltpu.VMEM((ck, 2 * tq), F32)] * (2 * n_pair) + [pltpu.VMEM((1, 2 * tq), F32)] * (3 * n_pair))
    return pl.pallas_call(
        functools.partial(_gqa_body, tq=tq, n_chunk=n_chunk),
        grid=(t // tq,),
        in_specs=[pl.BlockSpec((tq, BRANCH_W), lambda i: (i, 0)),
                  pl.BlockSpec((tq, BRANCH_W), lambda i: (jnp.minimum(i + 1, t // tq - 1), 0)),
                  _slab(tq, BRANCH_W, gate_off),
                  _resident((n_chunk, ck, KV_W), lambda i: (0, 0, 0)),
                  _resident(vt.shape, lambda i: (0, 0, 0, 0)),
                  _resident((tx, KV_W), lambda i: (0, 0)),
                  _resident(vt_extra.shape, lambda i: (0, 0, 0, 0))],
        out_specs=pl.BlockSpec((tq, BRANCH_W), lambda i: (i, 0)),
        out_shape=jax.ShapeDtypeStruct((t, BRANCH_W), BF16),
        scratch_shapes=scratch,
        compiler_params=_params(dimension_semantics=("arbitrary",)),
        name="gqa_attn",
    )(q, q, gate, k.reshape(n_chunk, ck, KV_W), vt, k_extra, vt_extra)


def _na_bias_tiles(rpb):
    cq = np.arange(GRID_W)
    col_start = np.clip(cq - NA_WIN_COLS // 2, 0, GRID_W - NA_WIN_COLS)
    col_ok = (cq[None, :] >= col_start[:, None]) & (cq[None, :] < col_start[:, None] + NA_WIN_COLS)
    col_off = np.clip(cq[None, :] - cq[:, None], -(NA_WIN_COLS - 1), NA_WIN_COLS - 1) + NA_WIN_COLS - 1
    onehot = jnp.asarray(col_off[:, :, None] == np.arange(2 * NA_WIN_COLS - 1), F32)
    tiles = jnp.einsum("lhrm,qkm->lhrqk", rpb.astype(F32), onehot, precision=lax.Precision.HIGHEST)
    tiles = jnp.where(col_ok, tiles * LOG2_E, NEG_BIG)
    masked = jnp.full(tiles.shape[:2] + (1, GRID_W, GRID_W), NEG_BIG, F32)
    tiles = jnp.concatenate([tiles, masked], axis=2)
    return jnp.concatenate([tiles, tiles], axis=-1)


def _na_body(q_ref, gate_ref, k_ref, v_ref, kc_ref, vc_ref, tiles_ref, o_ref, *, n_rows):
    rb, band = NA_ROW_BLOCK, NA_BAND_ROWS
    tq, nk = rb * GRID_W, band * GRID_W
    b = pl.program_id(0)
    u0 = jnp.clip(b * rb - NA_WIN_ROWS // 2, 0, n_rows - band)
    rows = pl.ds(pl.multiple_of(u0 * GRID_W, GRID_W), nk)
    low_q = lax.broadcasted_iota(jnp.int32, (tq, LANES), 1) < HEAD_DIM
    low_t = lax.broadcasted_iota(jnp.int32, (GRID_W, LANES), 1) < GRID_W

    def bias_rows(h, i):
        r = b * rb + i
        row_start = jnp.clip(r - NA_WIN_ROWS // 2, 0, n_rows - NA_WIN_ROWS)
        pieces = []
        for j in range(band):
            key_row = u0 + j
            in_window = jnp.logical_and(key_row >= row_start, key_row < row_start + NA_WIN_ROWS)
            ro = jnp.where(in_window, key_row - r + NA_WIN_ROWS - 1, 2 * NA_WIN_ROWS - 1)
            pieces.append(tiles_ref[0, h, ro])
        return jnp.concatenate([jnp.where(low_t, pieces[j], pieces[j + 1]) for j in range(0, band, 2)], axis=1)

    for p in range(N_HEADS // 2):
        bias = jnp.concatenate([bias_rows(2 * p + hd, i) for hd in range(2) for i in range(rb)], axis=0)
        cols = slice(LANES * p, LANES * (p + 1))
        qp = q_ref[:, cols].astype(F32) * SCORE_SCALE_LOG2
        qa, qb = _pair_queries(qp, low_q, None)
        lhs = jnp.concatenate([qa, qb], axis=0).astype(BF16)
        kb, vb = k_ref[rows, cols], v_ref[rows, cols]
        kc, vc = kc_ref[:, cols], vc_ref[:, cols]
        s_band = _nt_dot(lhs, kb) + bias
        s_ctx = _nt_dot(lhs, kc)
        m = jnp.maximum(jnp.max(s_band, axis=-1, keepdims=True), jnp.max(s_ctx, axis=-1, keepdims=True))
        pb = jnp.exp2(s_band - m).astype(BF16)
        pc = jnp.exp2(s_ctx - m).astype(BF16)
        heads = []
        for hd in range(2):
            r = slice(hd * tq, (hd + 1) * tq)
            acc = (jnp.dot(pb[r], _values_with_ones(vb, hd), preferred_element_type=F32)
                   + jnp.dot(pc[r], _values_with_ones(vc, hd), preferred_element_type=F32))
            ones_lane = HEAD_DIM * (1 - hd)
            heads.append(acc / acc[:, ones_lane:ones_lane + 1])
        o = jnp.where(low_q, heads[0], heads[1])
        o_ref[:, cols] = (o * _silu(gate_ref[:, cols].astype(F32))).astype(BF16)


def _na_attn(p, off, pc, bias, layer):
    s = p.shape[0]
    cn = pc.shape[0]
    n_rows = s // GRID_W
    tq = NA_ROW_BLOCK * GRID_W
    assert n_rows % NA_ROW_BLOCK == 0 and n_rows >= NA_BAND_ROWS >= NA_ROW_BLOCK + NA_WIN_ROWS - 1
    assert NA_BAND_ROWS % 2 == 0
    nb = n_rows // NA_ROW_BLOCK
    w = BRANCH_W
    return pl.pallas_call(
        functools.partial(_na_body, n_rows=n_rows),
        grid=(nb,),
        in_specs=[_slab(tq, w, off["a_q"]), _slab(tq, w, off["a_gate"]),
                  _resident((s, w), lambda i, _c=off["a_k"] // w: (0, _c)),
                  _resident((s, w), lambda i, _c=off["a_v"] // w: (0, _c)),
                  _resident((cn, w), lambda i, _c=off["a_k"] // w: (0, _c)),
                  _resident((cn, w), lambda i, _c=off["a_v"] // w: (0, _c)),
                  _resident((1,) + bias.shape[1:], lambda i: (layer, 0, 0, 0, 0))],
        out_specs=pl.BlockSpec((tq, w), lambda i: (i, 0)),
        out_shape=jax.ShapeDtypeStruct((s, w), BF16),
        compiler_params=_params(dimension_semantics=("arbitrary",)),
        name="na_attn",
    )(p, p, p, p, pc, pc, bias)


def _halo_specs(t, tm, width, off):
    assert off % LANES == 0 and tm % HALO == 0
    block = lambda rows: (pl.Element(rows), pl.Element(width))
    per, last = tm // HALO, t // HALO - 1
    return [pl.BlockSpec(block(tm), lambda i: (i * tm, off)),
            pl.BlockSpec(block(HALO), lambda i: (jnp.maximum(i * per - 1, 0) * HALO, off)),
            pl.BlockSpec(block(HALO), lambda i: (jnp.minimum((i + 1) * per, last) * HALO, off))]


def _fill_padded(pad_ref, cur, prev, nxt, tm):
    i, n = pl.program_id(0), pl.num_programs(0)
    pad_ref[0:HALO] = jnp.where(i > 0, prev, jnp.zeros_like(prev))
    pad_ref[HALO:HALO + tm] = cur
    pad_ref[HALO + tm:] = jnp.where(i < n - 1, nxt, jnp.zeros_like(nxt))


def _pool_body(u_ref, up_ref, un_ref, gate_ref, w_ref, sc_ref, o_ref, pad_ref, *, tm, seq):
    _fill_padded(pad_ref, u_ref[...].astype(F32), up_ref[...].astype(F32), un_ref[...].astype(F32), tm)
    t = pl.program_id(0) * tm + lax.broadcasted_iota(jnp.int32, (tm, POOL_GROUP), 0)
    for gi, ksz in enumerate(POOL_SIZES):
        cols = slice(POOL_GROUP * gi, POOL_GROUP * (gi + 1))
        back = ksz // 2
        tot = pad_ref[HALO - back:HALO - back + tm, cols]
        for d in range(1 - back, ksz - back):
            tot = tot + pad_ref[HALO + d:HALO + d + tm, cols]
        lo = jnp.maximum(t - back, 0)
        hi = jnp.minimum(t + (ksz - 1 - back), seq - 1)
        mean = tot / (hi - lo + 1).astype(F32)
        dlt = (mean - pad_ref[HALO:HALO + tm, cols]).astype(BF16)
        y = jnp.dot(dlt, w_ref[0, gi].astype(BF16), preferred_element_type=F32) * sc_ref[:, cols]
        o_ref[:, cols] = (y * _silu(gate_ref[:, cols].astype(F32))).astype(BF16)


def _pool(p, off, w_pool, layer, pool_scale):
    t = p.shape[0]
    tm = _tile(t, 512, HALO)
    w = BRANCH_W
    return pl.pallas_call(
        functools.partial(_pool_body, tm=tm, seq=t),
        grid=(t // tm,),
        in_specs=_halo_specs(t, tm, w, off["b_in"]) + [
            _slab(tm, w, off["b_gate"]),
            pl.BlockSpec((1,) + w_pool.shape[1:], lambda i: (layer, 0, 0, 0)),
            pl.BlockSpec((1, w), lambda i: (0, 0))],
        out_specs=pl.BlockSpec((tm, w), lambda i: (i, 0)),
        out_shape=jax.ShapeDtypeStruct((t, w), BF16),
        scratch_shapes=[pltpu.VMEM((tm + 2 * HALO, w), F32)],
        compiler_params=_params(dimension_semantics=("arbitrary",)),
        name="pool",
    )(p, p, p, p, w_pool, pool_scale)


def _glu(x):
    x = x.astype(F32)
    return x[:, :BRANCH_W] * jax.nn.sigmoid(x[:, BRANCH_W:])


def _conv_body(x_ref, xp_ref, xn_ref, gate_ref, cw_ref, cb_ref, lg_ref, lb_ref, pw_ref, o_ref, pad_ref, sh_ref,
               *, tm):
    _fill_padded(pad_ref, _glu(x_ref[...]), _glu(xp_ref[...]), _glu(xn_ref[...]), tm)
    span = sh_ref.shape[1]
    for b in range(1, SUBLANES):
        sh_ref[b - 1] = pad_ref[b:b + span, :]
    reach = CONV_WIDTH // 2
    y = jnp.zeros((tm, BRANCH_W), F32) + cb_ref[...]
    for j in range(CONV_WIDTH):
        a, b = divmod(HALO - reach + j, SUBLANES)
        rows = slice(SUBLANES * a, SUBLANES * a + tm)
        y = y + (pad_ref[rows, :] if b == 0 else sh_ref[b - 1, rows, :]) * cw_ref[j:j + 1, :]
    mu = jnp.mean(y, axis=-1, keepdims=True)
    yc = y - mu
    var = jnp.mean(yc * yc, axis=-1, keepdims=True)
    z = _silu(yc * lax.rsqrt(var + EPS) * lg_ref[...] + lb_ref[...]).astype(BF16)
    out = jnp.dot(z, pw_ref[0].astype(BF16), preferred_element_type=F32)
    o_ref[...] = (out * _silu(gate_ref[...].astype(F32))).astype(BF16)


def _conv(p, off, conv_w, conv_b, ln_g, ln_b, w_pw, layer):
    t = p.shape[0]
    tm = _tile(t, 512, HALO)
    w = BRANCH_W
    vec = pl.BlockSpec((1, w), lambda i: (0, 0))
    return pl.pallas_call(
        functools.partial(_conv_body, tm=tm),
        grid=(t // tm,),
        in_specs=_halo_specs(t, tm, 2 * w, off["d_glu"]) + [
            _slab(tm, w, off["d_gate"]),
            pl.BlockSpec(conv_w.shape, lambda i: (0, 0)), vec, vec, vec,
            pl.BlockSpec((1, w, w), lambda i: (layer, 0, 0))],
        out_specs=pl.BlockSpec((tm, w), lambda i: (i, 0)),
        out_shape=jax.ShapeDtypeStruct((t, w), BF16),
        scratch_shapes=[pltpu.VMEM((tm + 2 * HALO, w), F32),
                        pltpu.VMEM((SUBLANES - 1, tm + 2 * HALO - SUBLANES, w), F32)],
        compiler_params=_params(dimension_semantics=("arbitrary",)),
        name="conv",
    )(p, p, p, p, conv_w, conv_b, ln_g, ln_b, w_pw)


def _cast_once(w_ref, wb_ref):
    @pl.when(pl.program_id(0) == 0)
    def _():
        wb_ref[...] = w_ref[0].astype(BF16)


def _merge_body(m0, m1, m2, m3, o0, o1, o2, o3, w_ref, y_ref, wb_ref):
    _cast_once(w_ref, wb_ref)
    y = None
    for bi, (m_ref, o_ref) in enumerate(((m0, o0), (m1, o1), (m2, o2), (m3, o3))):
        proj = jnp.dot(o_ref[...], wb_ref[bi], preferred_element_type=F32)
        term = (1.0 + jnp.tanh(0.5 * m_ref[...].astype(F32))) * proj
        y = term if y is None else y + term
    y_ref[...] = (0.5 * y).astype(BF16)


def _merge(p, merge_off, outs, w_branch, layer):
    t = p.shape[0]
    d = w_branch.shape[-1]
    tm = _tile(t, 256, PACKED_ROWS)
    return pl.pallas_call(
        _merge_body,
        grid=(t // tm,),
        in_specs=[_slab(tm, d, merge_off + bi * d) for bi in range(N_BRANCH)]
        + [pl.BlockSpec((tm, BRANCH_W), lambda i: (i, 0))] * N_BRANCH
        + [_resident((1,) + w_branch.shape[1:], lambda i: (layer, 0, 0, 0))],
        out_specs=pl.BlockSpec((tm, d), lambda i: (i, 0)),
        out_shape=jax.ShapeDtypeStruct((t, d), BF16),
        scratch_shapes=[pltpu.VMEM(w_branch.shape[1:], BF16)],
        compiler_params=_params(dimension_semantics=("arbitrary",)),
        name="merge",
    )(p, p, p, p, *outs, w_branch)


def _out_body(y_ref, w_ref, x_ref, g_ref, gate_ref, *rest):
    *rest, wb_ref = rest
    _cast_once(w_ref, wb_ref)
    y = jnp.dot(y_ref[...], wb_ref[...], preferred_element_type=F32)
    yn = y * lax.rsqrt(jnp.mean(y * y, axis=-1, keepdims=True) + EPS) * g_ref[...]
    x_next = x_ref[...] + gate_ref[...] * yn
    if len(rest) == 1:
        rest[0][...] = x_next
    else:
        gn_ref, sh_ref, sc_ref, o_ref, h_ref = rest
        o_ref[...] = x_next
        h_ref[...] = _modulated(x_next, gn_ref[...], sh_ref[...], sc_ref[...])


def _out_proj(y, w_out, layer, x, g_post, gate, next_mod=None):
    t, d = x.shape
    tm = _tile(t, 256, PACKED_ROWS)
    vec = pl.BlockSpec((1, d), lambda i: (0, 0))
    row = pl.BlockSpec((tm, d), lambda i: (i, 0))
    n_next = 0 if next_mod is None else 1
    out = pl.pallas_call(
        _out_body,
        grid=(t // tm,),
        in_specs=[row, _resident((1, d, d), lambda i: (layer, 0, 0)), row, vec, vec] + [vec] * (3 * n_next),
        scratch_shapes=[pltpu.VMEM((d, d), BF16)],
        out_specs=[row] * (1 + n_next),
        out_shape=[jax.ShapeDtypeStruct((t, d), F32)] + [jax.ShapeDtypeStruct((t, d), BF16)] * n_next,
        compiler_params=_params(dimension_semantics=("arbitrary",)),
        name="out_proj",
    )(y, w_out, x, g_post, gate, *(next_mod or ()))
    return out if n_next else (out[0], None)


def kernel(x, c, ctx, c_ctx, w_ada, b_ada, g_pre, g_post, w_in, na_rpb, pool_w, pool_scale,
           q_norm, k_norm, conv_w, conv_b, conv_ln_g, conv_ln_b, conv_pw, w_branch, w_out):
    batch, seq, d = x.shape
    assert batch == 1 and seq % GRID_W == 0
    cn = ctx.shape[1]
    depth = w_ada.shape[0]
    off = _layout(d)

    xs, cs = x[0], ctx[0]
    cvec = jnp.zeros((SUBLANES, d), F32).at[0].set(c[0]).at[1].set(c_ctx)
    ada = _ada(cvec, w_ada, b_ada)

    ones_bd = jnp.asarray(np.kron(np.eye(N_HEADS), np.ones((HEAD_DIM, HEAD_DIM))), BF16)
    rope_tabs = _rope_tables(seq)
    na_bias = _na_bias_tiles(na_rpb)
    row = lambda v: v.reshape(1, -1)
    qn_all = jnp.tile(q_norm, (1, N_HEADS))
    kn_all = jnp.tile(k_norm, (1, GQA_KV_HEADS))

    mod = lambda l, r: tuple(ada[l, r:r + 1, k * d:(k + 1) * d] for k in range(3))
    h = _modulate(xs, row(g_pre[0]), *mod(0, 0)[:2])
    hc = _modulate(cs, row(g_pre[0]), *mod(0, 1)[:2])
    for l in range(depth):
        last = l == depth - 1
        conv_args = (conv_w[l], row(conv_b[l]), row(conv_ln_g[l]), row(conv_ln_b[l]), conv_pw, l)
        gate, gate_c = mod(l, 0)[2], mod(l, 1)[2]
        next_mod = lambda r: None if last else (row(g_pre[l + 1]),) + mod(l + 1, r)[:2]
        qn, kn = row(qn_all[l]), row(kn_all[l])

        pc = _in_proj(hc, w_in, l, KV_COLS if last else None)
        qc_ctx, kq_ctx, vt_ctx = _qk_prep(pc, None if last else off["c_q"], off["c_k"], off["c_v"],
                                          qn, kn, ones_bd, None)

        p = _in_proj(h, w_in, l)
        qc, kc, vt = _qk_prep(p, off["c_q"], off["c_k"], off["c_v"], qn, kn, ones_bd, rope_tabs)
        o_c = _gqa_attn(qc, p, off["c_gate"], kc, vt, kq_ctx, vt_ctx)
        o_a = _na_attn(p, off, pc, na_bias, l)
        o_b = _pool(p, off, pool_w, l, row(pool_scale[l]))
        o_d = _conv(p, off, *conv_args)
        y = _merge(p, off["merge"], (o_a, o_b, o_c, o_d), w_branch, l)
        xs, h = _out_proj(y, w_out, l, xs, row(g_post[l]), gate, next_mod(0))

        if not last:
            o_a_c = _dense_attn(pc, off["a_q"], pc, off["a_gate"],
                                [(pc, off["a_k"], pc, off["a_v"])], False, SCORE_SCALE_LOG2)
            o_c_c = _dense_attn(qc_ctx, 0, pc, off["c_gate"], [(kq_ctx, 0, pc, off["c_v"])], True, 1.0)
            o_b_c = _pool(pc, off, pool_w, l, row(pool_scale[l]))
            o_d_c = _conv(pc, off, *conv_args)
            y_c = _merge(pc, off["merge"], (o_a_c, o_b_c, o_c_c, o_d_c), w_branch, l)
            cs, hc = _out_proj(y_c, w_out, l, cs, row(g_post[l]), gate_c, next_mod(1))
    return xs[None]
```

```python
import functools

import numpy as np
import jax
import jax.numpy as jnp
from jax import lax
from jax.experimental import pallas as pl
from jax.experimental.pallas import tpu as pltpu

F32 = jnp.float32
BF16 = jnp.bfloat16

GRID_W = 64
HEAD_DIM = 64
BRANCH_W = 512
N_BRANCH = 4
N_HEADS = BRANCH_W // HEAD_DIM
GQA_KV_HEADS = 2
KV_W = GQA_KV_HEADS * HEAD_DIM
NA_WIN_ROWS = 8
NA_WIN_COLS = 16
POOL_SIZES = (2, 4, 8, 16)
POOL_GROUP = BRANCH_W // len(POOL_SIZES)
ROPE_THETA = 10000.0
CONV_WIDTH = 31
EPS = 1e-6

LANES = 128
SUBLANES = 8
PACKED_ROWS = 16
HALO = PACKED_ROWS
ROPE_HALF = HEAD_DIM // 4
NA_ROW_BLOCK = 4
NA_BAND_ROWS = 12
NEG_BIG = -1e30
LOG2_E = 1.4426950408889634
SCORE_SCALE_LOG2 = HEAD_DIM ** -0.5 * LOG2_E
V7X_VMEM_BYTES = 64 * 1024 * 1024
V7X_VMEM_LIMIT = V7X_VMEM_BYTES * 13 // 16
IN_PROJ_VMEM_LIMIT = V7X_VMEM_BYTES * 29 // 32

PARTS = (("a_k", BRANCH_W), ("a_v", BRANCH_W), ("c_k", KV_W), ("c_v", KV_W), ("a_q", BRANCH_W), ("c_q", BRANCH_W),
         ("a_gate", BRANCH_W), ("b_in", BRANCH_W), ("b_gate", BRANCH_W), ("c_gate", BRANCH_W),
         ("d_glu", 2 * BRANCH_W), ("d_gate", BRANCH_W), ("merge", None))
KV_COLS = 2 * BRANCH_W + 2 * KV_W


def _layout(d_model):
    off, o = {}, 0
    for n, w in PARTS:
        off[n] = o
        o += N_BRANCH * d_model if w is None else w
    return off


def _params(vmem_limit=V7X_VMEM_LIMIT, **kw):
    return pltpu.CompilerParams(vmem_limit_bytes=vmem_limit, **kw)


def _tile(n, pref, mult):
    if n <= pref:
        return n
    t = (pref // mult) * mult
    while t >= mult:
        if n % t == 0:
            return t
        t -= mult
    raise ValueError(f"no tile for {n}")


def _resident(block_shape, index_map):
    return pl.BlockSpec(block_shape, index_map, pipeline_mode=pl.Buffered(1))


def _slab(tm, width, off):
    assert off % LANES == 0
    return pl.BlockSpec((pl.Element(tm), pl.Element(width)), lambda i: (i * tm, off))


def _silu(x):
    return x * jax.nn.sigmoid(x)


def _ada_body(cs_ref, w_ref, b_ref, o_ref):
    s = _silu(cs_ref[...]).astype(BF16)
    o_ref[0] = jnp.dot(s, w_ref[0].astype(BF16), preferred_element_type=F32) + b_ref[0]


def _ada(cs, w_ada, b_ada):
    depth, d, n = w_ada.shape
    tn = _tile(n, 1024, LANES)
    return pl.pallas_call(
        _ada_body,
        grid=(depth, n // tn),
        in_specs=[pl.BlockSpec((SUBLANES, d), lambda l, j: (0, 0)),
                  pl.BlockSpec((1, d, tn), lambda l, j: (l, 0, j)),
                  pl.BlockSpec((1, 1, tn), lambda l, j: (l, 0, j))],
        out_specs=pl.BlockSpec((1, SUBLANES, tn), lambda l, j: (l, 0, j)),
        out_shape=jax.ShapeDtypeStruct((depth, SUBLANES, n), F32),
        compiler_params=_params(dimension_semantics=("arbitrary", "arbitrary")),
        name="ada",
    )(cs, w_ada, b_ada.reshape(depth, 1, n))


def _modulated(x, g, shift, scale):
    y = x * lax.rsqrt(jnp.mean(x * x, axis=-1, keepdims=True) + EPS) * g
    return (y * (1.0 + scale) + shift).astype(BF16)


def _modulate_body(x_ref, g_ref, sh_ref, sc_ref, o_ref):
    o_ref[...] = _modulated(x_ref[...], g_ref[...], sh_ref[...], sc_ref[...])


def _modulate(x, g, shift, scale):
    t, d = x.shape
    tm = _tile(t, 512, PACKED_ROWS)
    vec = pl.BlockSpec((1, d), lambda i: (0, 0))
    return pl.pallas_call(
        _modulate_body,
        grid=(t // tm,),
        in_specs=[pl.BlockSpec((tm, d), lambda i: (i, 0)), vec, vec, vec],
        out_specs=pl.BlockSpec((tm, d), lambda i: (i, 0)),
        out_shape=jax.ShapeDtypeStruct((t, d), BF16),
        compiler_params=_params(dimension_semantics=("arbitrary",)),
        name="modulate",
    )(x, g, shift, scale)


def _in_proj_body(h_ref, w_ref, o_ref, wb_ref):
    @pl.when(pl.program_id(1) == 0)
    def _():
        wb_ref[...] = w_ref[0].astype(BF16)

    half = h_ref.shape[0] // 2
    for r in (slice(0, half), slice(half, 2 * half)):
        o_ref[r, :] = jnp.dot(h_ref[r, :], wb_ref[...], preferred_element_type=F32).astype(o_ref.dtype)


def _col_tile(n):
    for mult in (2 * LANES, LANES):
        try:
            return _tile(n, 1536, mult)
        except ValueError:
            pass
    raise ValueError(n)


def _in_proj(h, w_in, layer, n_cols=None):
    t, d = h.shape
    n = w_in.shape[2]
    tm = _tile(t, 2048, 2 * PACKED_ROWS)
    tn = _col_tile(n)
    n_tiles = n // tn if n_cols is None else pl.cdiv(n_cols, tn)
    return pl.pallas_call(
        _in_proj_body,
        grid=(n_tiles, t // tm),
        in_specs=[pl.BlockSpec((tm, d), lambda j, i: (i, 0)),
                  pl.BlockSpec((1, d, tn), lambda j, i: (layer, 0, j))],
        out_specs=pl.BlockSpec((tm, tn), lambda j, i: (i, j)),
        out_shape=jax.ShapeDtypeStruct((t, n_tiles * tn), BF16),
        scratch_shapes=[pltpu.VMEM((d, tn), BF16)],
        compiler_params=_params(vmem_limit=IN_PROJ_VMEM_LIMIT, dimension_semantics=("arbitrary", "arbitrary")),
        name="in_proj",
    )(h, w_in)


def _head_meansq(x, ones_bd):
    ss = x * x
    hi = ss.astype(BF16)
    lo = (ss - hi.astype(F32)).astype(BF16)
    tot = (jnp.dot(hi, ones_bd, preferred_element_type=F32)
           + jnp.dot(lo, ones_bd, preferred_element_type=F32))
    return tot * (1.0 / HEAD_DIM)


def _rope(y, cos, sin_signed):
    w = y.shape[-1]
    lane = lax.broadcasted_iota(jnp.int32, y.shape, 1)
    nxt = pltpu.roll(y, w - ROPE_HALF, 1)
    prv = pltpu.roll(y, ROPE_HALF, 1)
    return y * cos + jnp.where((lane % (2 * ROPE_HALF)) < ROPE_HALF, nxt, prv) * sin_signed


def _norm_rope(x, w, bd, rope):
    x = x * lax.rsqrt(_head_meansq(x, bd) + EPS) * w
    if rope is not None:
        reps = x.shape[-1] // LANES
        x = _rope(x, jnp.concatenate([rope[0]] * reps, axis=1), jnp.concatenate([rope[1]] * reps, axis=1))
    return x


VT_ROWS = HEAD_DIM + PACKED_ROWS


def _qk_prep_body(*refs, use_rope, with_q):
    refs = list(refs)
    q_ref = refs.pop(0) if with_q else None
    k_ref, v_ref = refs.pop(0), refs.pop(0)
    qn_ref = refs.pop(0) if with_q else None
    kn_ref, bd_ref = refs.pop(0), refs.pop(0)
    rope = (refs.pop(0)[...], refs.pop(0)[...]) if use_rope else None
    bd = bd_ref[...]
    if with_q:
        q = _norm_rope(q_ref[...].astype(F32), qn_ref[...], bd, rope)
        refs.pop(0)[...] = (q * SCORE_SCALE_LOG2).astype(BF16)
    k = _norm_rope(k_ref[...].astype(F32), kn_ref[...], bd[:KV_W, :KV_W], rope)
    refs.pop(0)[...] = k.astype(BF16)
    vt_ref = refs.pop(0)
    vt = v_ref[...].astype(F32).T
    row = lax.broadcasted_iota(jnp.int32, (VT_ROWS - HEAD_DIM, vt.shape[1]), 0)
    tail = jnp.where(row == 0, 1.0, 0.0)
    for g in range(GQA_KV_HEADS):
        vt_ref[0, g] = jnp.concatenate([vt[HEAD_DIM * g:HEAD_DIM * (g + 1)], tail], axis=0).astype(BF16)


def _qk_prep(p, q_off, k_off, v_off, qn, kn, ones_bd, rope_tabs):
    t = p.shape[0]
    tm = _tile(t, 512, LANES)
    use_rope, with_q = rope_tabs is not None, q_off is not None
    const = lambda shape: pl.BlockSpec(shape, lambda i: (0, 0))
    in_specs, args, out_specs, out_shape = [], [], [], []
    if with_q:
        in_specs.append(_slab(tm, BRANCH_W, q_off))
        args.append(p)
    in_specs += [_slab(tm, KV_W, k_off), _slab(tm, KV_W, v_off)]
    args += [p, p]
    if with_q:
        in_specs.append(const((1, BRANCH_W)))
        args.append(qn)
        out_specs.append(pl.BlockSpec((tm, BRANCH_W), lambda i: (i, 0)))
        out_shape.append(jax.ShapeDtypeStruct((t, BRANCH_W), BF16))
    in_specs += [const((1, KV_W)), const((BRANCH_W, BRANCH_W))]
    args += [kn, ones_bd]
    if use_rope:
        in_specs += [pl.BlockSpec((tm, LANES), lambda i: (i, 0))] * 2
        args += list(rope_tabs)
    out_specs += [pl.BlockSpec((tm, KV_W), lambda i: (i, 0)),
                  pl.BlockSpec((1, GQA_KV_HEADS, VT_ROWS, tm), lambda i: (i, 0, 0, 0))]
    out_shape += [jax.ShapeDtypeStruct((t, KV_W), BF16),
                  jax.ShapeDtypeStruct((t // tm, GQA_KV_HEADS, VT_ROWS, tm), BF16)]
    res = pl.pallas_call(
        functools.partial(_qk_prep_body, use_rope=use_rope, with_q=with_q),
        grid=(t // tm,),
        in_specs=in_specs,
        out_specs=out_specs,
        out_shape=out_shape,
        compiler_params=_params(dimension_semantics=("arbitrary",)),
        name="qk_prep",
    )(*args)
    return res if with_q else (None, res[0], res[1])


def _rope_tables(seq):
    half = ROPE_HALF
    t = lax.broadcasted_iota(jnp.int32, (seq, LANES), 0)
    lane = lax.broadcasted_iota(jnp.int32, (seq, LANES), 1)
    pos = jnp.where((lane % HEAD_DIM) < 2 * half, t // GRID_W, t % GRID_W).astype(F32)
    freqs = ROPE_THETA ** (-(lane % half).astype(F32) / half)
    ang = pos * freqs
    sign = jnp.where((lane % (2 * half)) < half, -1.0, 1.0)
    return jnp.cos(ang), jnp.sin(ang) * sign


def _nt_dot(a, b):
    return lax.dot_general(a, b, (((1,), (1,)), ((), ())), preferred_element_type=F32)


def _pair_queries(qp, low, shared_kv_lanes):
    zero = jnp.zeros_like(qp)
    if shared_kv_lanes is None:
        return jnp.where(low, qp, zero), jnp.where(low, zero, qp)
    qr = pltpu.roll(qp, HEAD_DIM, 1)
    if shared_kv_lanes == 0:
        return jnp.where(low, qp, zero), jnp.where(low, qr, zero)
    return jnp.where(low, zero, qr), jnp.where(low, zero, qp)


def _values_with_ones(vc, half):
    lane = lax.broadcasted_iota(jnp.int32, vc.shape, 1)
    keep = (lane < HEAD_DIM) if half == 0 else (lane >= HEAD_DIM)
    ones_lane = HEAD_DIM * (1 - half)
    fill = jnp.where(lane == ones_lane, 1.0, 0.0)
    return jnp.where(keep, vc.astype(F32), fill).astype(vc.dtype)


def _dense_attn_body(*refs, n_src, chunks, kv_grouped, q_scale, tq):
    n_pair = N_HEADS // 2
    q_ref, gate_ref = refs[0], refs[1]
    srcs = [(refs[2 + 2 * i], refs[3 + 2 * i]) for i in range(n_src)]
    o_ref = refs[2 + 2 * n_src]
    scratch = refs[3 + 2 * n_src:]
    m_s, acc_s = scratch[:n_pair], scratch[n_pair:]
    low_q = lax.broadcasted_iota(jnp.int32, (tq, LANES), 1) < HEAD_DIM

    halves = [((2 * p) // (N_HEADS // GQA_KV_HEADS),) * 2 if kv_grouped else (0, 1) for p in range(n_pair)]
    lhs = []
    for p in range(n_pair):
        qp = q_ref[:, LANES * p:LANES * (p + 1)].astype(F32) * q_scale
        qa, qb = _pair_queries(qp, low_q, halves[p][0] if kv_grouped else None)
        lhs.append(jnp.concatenate([qa, qb], axis=0).astype(BF16))
        m_s[p][...] = jnp.full(m_s[p].shape, NEG_BIG, F32)
        acc_s[p][...] = jnp.zeros(acc_s[p].shape, F32)

    def step(p, kc, va, vb):
        s = _nt_dot(lhs[p], kc)
        m_old = m_s[p][...]
        m_new = jnp.maximum(m_old, jnp.max(s, axis=-1, keepdims=True))
        alpha = jnp.exp2(m_old - m_new)
        pr = jnp.exp2(s - m_new).astype(BF16)
        acc = acc_s[p]
        acc[:tq] = alpha[:tq] * acc[:tq] + jnp.dot(pr[:tq], va, preferred_element_type=F32)
        acc[tq:] = alpha[tq:] * acc[tq:] + jnp.dot(pr[tq:], vb, preferred_element_type=F32)
        m_s[p][...] = m_new

    def all_pairs(k_ref, v_ref, rows):
        if kv_grouped:
            kc, vc = k_ref[rows, :], v_ref[rows, :]
            vals = [_values_with_ones(vc, h) for h in range(GQA_KV_HEADS)]
        for p in range(n_pair):
            if kv_grouped:
                step(p, kc, vals[halves[p][0]], vals[halves[p][1]])
            else:
                cols = slice(LANES * p, LANES * (p + 1))
                vc = v_ref[rows, cols]
                step(p, k_ref[rows, cols], _values_with_ones(vc, 0), _values_with_ones(vc, 1))

    for (k_ref, v_ref), ck in zip(srcs, chunks):
        n_chunk = k_ref.shape[0] // ck
        if n_chunk == 1:
            all_pairs(k_ref, v_ref, slice(None))
        else:
            def loop(i, carry, k_ref=k_ref, v_ref=v_ref, ck=ck):
                all_pairs(k_ref, v_ref, pl.ds(pl.multiple_of(i * ck, ck), ck))
                return carry
            lax.fori_loop(0, n_chunk, loop, 0)

    for p in range(n_pair):
        cols = slice(LANES * p, LANES * (p + 1))
        heads = []
        for hd, half in enumerate(halves[p]):
            acc = acc_s[p][hd * tq:(hd + 1) * tq]
            ones_lane = HEAD_DIM * (1 - half)
            o = acc / acc[:, ones_lane:ones_lane + 1]
            heads.append(o if half == hd else pltpu.roll(o, HEAD_DIM, 1))
        o = jnp.where(low_q, heads[0], heads[1])
        o_ref[:, cols] = (o * _silu(gate_ref[:, cols].astype(F32))).astype(BF16)


def _key_chunk(tk):
    for mult in (2 * LANES, LANES, PACKED_ROWS):
        try:
            return _tile(tk, 1024, mult)
        except ValueError:
            pass
    raise ValueError(tk)


def _dense_attn(q, q_off, gate, gate_off, kv_srcs, kv_grouped, q_scale):
    t = q.shape[0]
    tq = _tile(t, 256, PACKED_ROWS)
    kw = KV_W if kv_grouped else BRANCH_W
    in_specs = [_slab(tq, BRANCH_W, q_off), _slab(tq, BRANCH_W, gate_off)]
    args = [q, gate]
    chunks = []
    for k_arr, k_off, v_arr, v_off in kv_srcs:
        tk = k_arr.shape[0]
        assert k_off % kw == 0 and v_off % kw == 0
        in_specs += [_resident((tk, kw), lambda i, _c=k_off // kw: (0, _c)),
                     _resident((tk, kw), lambda i, _c=v_off // kw: (0, _c))]
        args += [k_arr, v_arr]
        chunks.append(_key_chunk(tk))
    n_pair = N_HEADS // 2
    return pl.pallas_call(
        functools.partial(_dense_attn_body, n_src=len(kv_srcs), chunks=tuple(chunks),
                          kv_grouped=kv_grouped, q_scale=q_scale, tq=tq),
        grid=(t // tq,),
        in_specs=in_specs,
        out_specs=pl.BlockSpec((tq, BRANCH_W), lambda i: (i, 0)),
        out_shape=jax.ShapeDtypeStruct((t, BRANCH_W), BF16),
        scratch_shapes=[pltpu.VMEM((2 * tq, 1), F32)] * n_pair + [pltpu.VMEM((2 * tq, LANES), F32)] * n_pair,
        compiler_params=_params(dimension_semantics=("arbitrary",)),
        name="dense_attn",
    )(*args)


def _gqa_body(q_ref, qn_ref, gate_ref, k_ref, vt_ref, kx_ref, vtx_ref, o_ref, *scratch, tq, n_chunk):
    n_pair = N_HEADS // 2
    qt_s, qtn_s, m_s, acc_s, sx_s = (scratch[i * n_pair:(i + 1) * n_pair] for i in range(5))
    s_s = [scratch[5 * n_pair + 2 * p:5 * n_pair + 2 * p + 2] for p in range(n_pair)]
    mx_s = [scratch[7 * n_pair + 3 * p:7 * n_pair + 3 * p + 3] for p in range(n_pair)]
    s_s = [[(s_s[p][slot], mx_s[p][slot]) for slot in range(2)] for p in range(n_pair)]
    sx_s = [(sx_s[p], mx_s[p][2]) for p in range(n_pair)]
    low_q = lax.broadcasted_iota(jnp.int32, (tq, LANES), 1) < HEAD_DIM
    kv_of = [(2 * p) // (N_HEADS // GQA_KV_HEADS) for p in range(n_pair)]

    def scores(p, keys, buf, qt_ref=None):
        s = jnp.dot(keys, (qt_s[p] if qt_ref is None else qt_ref)[...], preferred_element_type=F32)
        buf[0][...] = s
        buf[1][...] = jnp.max(s, axis=0, keepdims=True)

    def consume(p, buf, vt):
        m_old = m_s[p][...]
        m_new = jnp.maximum(m_old, buf[1][...])
        alpha = jnp.exp2(m_old - m_new)
        pt = jnp.exp2(buf[0][...] - m_new).astype(BF16)
        acc_s[p][...] = alpha * acc_s[p][...] + jnp.dot(vt, pt, preferred_element_type=F32)
        m_s[p][...] = m_new

    def step(i, slot):
        for p in range(n_pair):
            scores(p, k_ref[i + 1], s_s[p][1 - slot])
            consume(p, s_s[p][slot], vt_ref[i, kv_of[p]])

    def transposed_queries(p, q_tile_ref):
        qp = q_tile_ref[:, LANES * p:LANES * (p + 1)].astype(F32)
        qa, qb = _pair_queries(qp, low_q, kv_of[p])
        return jnp.concatenate([qa, qb], axis=0).T.astype(BF16)

    is_first = pl.program_id(0) == 0

    @pl.when(is_first)
    def _():
        for p in range(n_pair):
            qt_s[p][...] = transposed_queries(p, q_ref)
            scores(p, k_ref[0], s_s[p][0])

    @pl.when(jnp.logical_not(is_first))
    def _():
        for p in range(n_pair):
            qt_s[p][...] = qtn_s[p][...]

    for p in range(n_pair):
        m_s[p][...] = jnp.full(m_s[p].shape, NEG_BIG, F32)
        acc_s[p][...] = jnp.zeros(acc_s[p].shape, F32)

    unroll = 4

    def steps(j, carry):
        for u in range(unroll):
            step(unroll * j + u, u % 2)
        return carry

    n_step = n_chunk - 1
    lax.fori_loop(0, n_step // unroll, steps, 0)
    last = n_chunk - 1
    for u in range(n_step % unroll):
        step(n_step - n_step % unroll + u, u % 2)
    def next_scores(p):
        scores(p, k_ref[0], s_s[p][0], qtn_s[p])

    for p in range(n_pair):
        qtn_s[p][...] = transposed_queries(p, qn_ref)
    for p in range(n_pair):
        scores(p, kx_ref[...], sx_s[p])
        if last % 2:
            next_scores(p)
        consume(p, s_s[p][last % 2], vt_ref[last, kv_of[p]])
    for p in range(n_pair):
        if not last % 2:
            next_scores(p)
        consume(p, sx_s[p], vtx_ref[0, kv_of[p]])
        acc = acc_s[p]
        ot = acc[:HEAD_DIM] / acc[HEAD_DIM:HEAD_DIM + 1]
        ot = jnp.concatenate([ot[:, :tq], ot[:, tq:]], axis=0)
        cols = slice(LANES * p, LANES * (p + 1))
        o_ref[:, cols] = (ot.T * _silu(gate_ref[:, cols].astype(F32))).astype(BF16)


def _gqa_attn(q, gate, gate_off, k, vt, k_extra, vt_extra):
    t = q.shape[0]
    n_chunk, _, _, ck = vt.shape
    tx = k_extra.shape[0]
    assert vt_extra.shape[0] == 1 and k.shape[0] == n_chunk * ck
    tq = _tile(t, 256, LANES)
    n_pair = N_HEADS // 2
    scratch = ([pltpu.VMEM((LANES, 2 * tq), BF16)] * (2 * n_pair) + [pltpu.VMEM((1, 2 * tq), F32)] * n_pair
               + [pltpu.VMEM((VT_ROWS, 2 * tq), F32)] * n_pair + [pltpu.VMEM((tx, 2 * tq), F32)] * n_pair
               + [pltpu.VMEM((ck, 2 * tq), F32)] * (2 * n_pair) + [pltpu.VMEM((1, 2 * tq), F32)] * (3 * n_pair))
    return pl.pallas_call(
        functools.partial(_gqa_body, tq=tq, n_chunk=n_chunk),
        grid=(t // tq,),
        in_specs=[pl.BlockSpec((tq, BRANCH_W), lambda i: (i, 0)),
                  pl.BlockSpec((tq, BRANCH_W), lambda i: (jnp.minimum(i + 1, t // tq - 1), 0)),
                  _slab(tq, BRANCH_W, gate_off),
                  _resident((n_chunk, ck, KV_W), lambda i: (0, 0, 0)),
                  _resident(vt.shape, lambda i: (0, 0, 0, 0)),
                  _resident((tx, KV_W), lambda i: (0, 0)),
                  _resident(vt_extra.shape, lambda i: (0, 0, 0, 0))],
        out_specs=pl.BlockSpec((tq, BRANCH_W), lambda i: (i, 0)),
        out_shape=jax.ShapeDtypeStruct((t, BRANCH_W), BF16),
        scratch_shapes=scratch,
        compiler_params=_params(dimension_semantics=("arbitrary",)),
        name="gqa_attn",
    )(q, q, gate, k.reshape(n_chunk, ck, KV_W), vt, k_extra, vt_extra)


def _na_bias_tiles(rpb):
    cq = np.arange(GRID_W)
    col_start = np.clip(cq - NA_WIN_COLS // 2, 0, GRID_W - NA_WIN_COLS)
    col_ok = (cq[None, :] >= col_start[:, None]) & (cq[None, :] < col_start[:, None] + NA_WIN_COLS)
    col_off = np.clip(cq[None, :] - cq[:, None], -(NA_WIN_COLS - 1), NA_WIN_COLS - 1) + NA_WIN_COLS - 1
    onehot = jnp.asarray(col_off[:, :, None] == np.arange(2 * NA_WIN_COLS - 1), F32)
    tiles = jnp.einsum("lhrm,qkm->lhrqk", rpb.astype(F32), onehot, precision=lax.Precision.HIGHEST)
    tiles = jnp.where(col_ok, tiles * LOG2_E, NEG_BIG)
    masked = jnp.full(tiles.shape[:2] + (1, GRID_W, GRID_W), NEG_BIG, F32)
    tiles = jnp.concatenate([tiles, masked], axis=2)
    return jnp.concatenate([tiles, tiles], axis=-1)


def _na_body(q_ref, gate_ref, k_ref, v_ref, kc_ref, vc_ref, tiles_ref, o_ref, *, n_rows):
    rb, band = NA_ROW_BLOCK, NA_BAND_ROWS
    tq, nk = rb * GRID_W, band * GRID_W
    b = pl.program_id(0)
    u0 = jnp.clip(b * rb - NA_WIN_ROWS // 2, 0, n_rows - band)
    rows = pl.ds(pl.multiple_of(u0 * GRID_W, GRID_W), nk)
    low_q = lax.broadcasted_iota(jnp.int32, (tq, LANES), 1) < HEAD_DIM
    low_t = lax.broadcasted_iota(jnp.int32, (GRID_W, LANES), 1) < GRID_W

    def bias_rows(h, i):
        r = b * rb + i
        row_start = jnp.clip(r - NA_WIN_ROWS // 2, 0, n_rows - NA_WIN_ROWS)
        pieces = []
        for j in range(band):
            key_row = u0 + j
            in_window = jnp.logical_and(key_row >= row_start, key_row < row_start + NA_WIN_ROWS)
            ro = jnp.where(in_window, key_row - r + NA_WIN_ROWS - 1, 2 * NA_WIN_ROWS - 1)
            pieces.append(tiles_ref[0, h, ro])
        return jnp.concatenate([jnp.where(low_t, pieces[j], pieces[j + 1]) for j in range(0, band, 2)], axis=1)

    for p in range(N_HEADS // 2):
        bias = jnp.concatenate([bias_rows(2 * p + hd, i) for hd in range(2) for i in range(rb)], axis=0)
        cols = slice(LANES * p, LANES * (p + 1))
        qp = q_ref[:, cols].astype(F32) * SCORE_SCALE_LOG2
        qa, qb = _pair_queries(qp, low_q, None)
        lhs = jnp.concatenate([qa, qb], axis=0).astype(BF16)
        kb, vb = k_ref[rows, cols], v_ref[rows, cols]
        kc, vc = kc_ref[:, cols], vc_ref[:, cols]
        s_band = _nt_dot(lhs, kb) + bias
        s_ctx = _nt_dot(lhs, kc)
        m = jnp.maximum(jnp.max(s_band, axis=-1, keepdims=True), jnp.max(s_ctx, axis=-1, keepdims=True))
        pb = jnp.exp2(s_band - m).astype(BF16)
        pc = jnp.exp2(s_ctx - m).astype(BF16)
        heads = []
        for hd in range(2):
            r = slice(hd * tq, (hd + 1) * tq)
            acc = (jnp.dot(pb[r], _values_with_ones(vb, hd), preferred_element_type=F32)
                   + jnp.dot(pc[r], _values_with_ones(vc, hd), preferred_element_type=F32))
            ones_lane = HEAD_DIM * (1 - hd)
            heads.append(acc / acc[:, ones_lane:ones_lane + 1])
        o = jnp.where(low_q, heads[0], heads[1])
        o_ref[:, cols] = (o * _silu(gate_ref[:, cols].astype(F32))).astype(BF16)


def _na_attn(p, off, pc, bias, layer):
    s = p.shape[0]
    cn = pc.shape[0]
    n_rows = s // GRID_W
    tq = NA_ROW_BLOCK * GRID_W
    assert n_rows % NA_ROW_BLOCK == 0 and n_rows >= NA_BAND_ROWS >= NA_ROW_BLOCK + NA_WIN_ROWS - 1
    assert NA_BAND_ROWS % 2 == 0
    nb = n_rows // NA_ROW_BLOCK
    w = BRANCH_W
    return pl.pallas_call(
        functools.partial(_na_body, n_rows=n_rows),
        grid=(nb,),
        in_specs=[_slab(tq, w, off["a_q"]), _slab(tq, w, off["a_gate"]),
                  _resident((s, w), lambda i, _c=off["a_k"] // w: (0, _c)),
                  _resident((s, w), lambda i, _c=off["a_v"] // w: (0, _c)),
                  _resident((cn, w), lambda i, _c=off["a_k"] // w: (0, _c)),
                  _resident((cn, w), lambda i, _c=off["a_v"] // w: (0, _c)),
                  _resident((1,) + bias.shape[1:], lambda i: (layer, 0, 0, 0, 0))],
        out_specs=pl.BlockSpec((tq, w), lambda i: (i, 0)),
        out_shape=jax.ShapeDtypeStruct((s, w), BF16),
        compiler_params=_params(dimension_semantics=("arbitrary",)),
        name="na_attn",
    )(p, p, p, p, pc, pc, bias)


def _halo_specs(t, tm, width, off):
    assert off % LANES == 0 and tm % HALO == 0
    block = lambda rows: (pl.Element(rows), pl.Element(width))
    per, last = tm // HALO, t // HALO - 1
    return [pl.BlockSpec(block(tm), lambda i: (i * tm, off)),
            pl.BlockSpec(block(HALO), lambda i: (jnp.maximum(i * per - 1, 0) * HALO, off)),
            pl.BlockSpec(block(HALO), lambda i: (jnp.minimum((i + 1) * per, last) * HALO, off))]


def _fill_padded(pad_ref, cur, prev, nxt, tm):
    i, n = pl.program_id(0), pl.num_programs(0)
    pad_ref[0:HALO] = jnp.where(i > 0, prev, jnp.zeros_like(prev))
    pad_ref[HALO:HALO + tm] = cur
    pad_ref[HALO + tm:] = jnp.where(i < n - 1, nxt, jnp.zeros_like(nxt))


def _pool_body(u_ref, up_ref, un_ref, gate_ref, w_ref, sc_ref, o_ref, pad_ref, *, tm, seq):
    _fill_padded(pad_ref, u_ref[...].astype(F32), up_ref[...].astype(F32), un_ref[...].astype(F32), tm)
    t = pl.program_id(0) * tm + lax.broadcasted_iota(jnp.int32, (tm, POOL_GROUP), 0)
    for gi, ksz in enumerate(POOL_SIZES):
        cols = slice(POOL_GROUP * gi, POOL_GROUP * (gi + 1))
        back = ksz // 2
        tot = pad_ref[HALO - back:HALO - back + tm, cols]
        for d in range(1 - back, ksz - back):
            tot = tot + pad_ref[HALO + d:HALO + d + tm, cols]
        lo = jnp.maximum(t - back, 0)
        hi = jnp.minimum(t + (ksz - 1 - back), seq - 1)
        mean = tot / (hi - lo + 1).astype(F32)
        dlt = (mean - pad_ref[HALO:HALO + tm, cols]).astype(BF16)
        y = jnp.dot(dlt, w_ref[0, gi].astype(BF16), preferred_element_type=F32) * sc_ref[:, cols]
        o_ref[:, cols] = (y * _silu(gate_ref[:, cols].astype(F32))).astype(BF16)


def _pool(p, off, w_pool, layer, pool_scale):
    t = p.shape[0]
    tm = _tile(t, 512, HALO)
    w = BRANCH_W
    return pl.pallas_call(
        functools.partial(_pool_body, tm=tm, seq=t),
        grid=(t // tm,),
        in_specs=_halo_specs(t, tm, w, off["b_in"]) + [
            _slab(tm, w, off["b_gate"]),
            pl.BlockSpec((1,) + w_pool.shape[1:], lambda i: (layer, 0, 0, 0)),
            pl.BlockSpec((1, w), lambda i: (0, 0))],
        out_specs=pl.BlockSpec((tm, w), lambda i: (i, 0)),
        out_shape=jax.ShapeDtypeStruct((t, w), BF16),
        scratch_shapes=[pltpu.VMEM((tm + 2 * HALO, w), F32)],
        compiler_params=_params(dimension_semantics=("arbitrary",)),
        name="pool",
    )(p, p, p, p, w_pool, pool_scale)


def _glu(x):
    x = x.astype(F32)
    return x[:, :BRANCH_W] * jax.nn.sigmoid(x[:, BRANCH_W:])


def _conv_body(x_ref, xp_ref, xn_ref, gate_ref, cw_ref, cb_ref, lg_ref, lb_ref, pw_ref, o_ref, pad_ref, sh_ref,
               *, tm):
    _fill_padded(pad_ref, _glu(x_ref[...]), _glu(xp_ref[...]), _glu(xn_ref[...]), tm)
    span = sh_ref.shape[1]
    for b in range(1, SUBLANES):
        sh_ref[b - 1] = pad_ref[b:b + span, :]
    reach = CONV_WIDTH // 2
    y = jnp.zeros((tm, BRANCH_W), F32) + cb_ref[...]
    for j in range(CONV_WIDTH):
        a, b = divmod(HALO - reach + j, SUBLANES)
        rows = slice(SUBLANES * a, SUBLANES * a + tm)
        y = y + (pad_ref[rows, :] if b == 0 else sh_ref[b - 1, rows, :]) * cw_ref[j:j + 1, :]
    mu = jnp.mean(y, axis=-1, keepdims=True)
    yc = y - mu
    var = jnp.mean(yc * yc, axis=-1, keepdims=True)
    z = _silu(yc * lax.rsqrt(var + EPS) * lg_ref[...] + lb_ref[...]).astype(BF16)
    out = jnp.dot(z, pw_ref[0].astype(BF16), preferred_element_type=F32)
    o_ref[...] = (out * _silu(gate_ref[...].astype(F32))).astype(BF16)


def _conv(p, off, conv_w, conv_b, ln_g, ln_b, w_pw, layer):
    t = p.shape[0]
    tm = _tile(t, 512, HALO)
    w = BRANCH_W
    vec = pl.BlockSpec((1, w), lambda i: (0, 0))
    return pl.pallas_call(
        functools.partial(_conv_body, tm=tm),
        grid=(t // tm,),
        in_specs=_halo_specs(t, tm, 2 * w, off["d_glu"]) + [
            _slab(tm, w, off["d_gate"]),
            pl.BlockSpec(conv_w.shape, lambda i: (0, 0)), vec, vec, vec,
            pl.BlockSpec((1, w, w), lambda i: (layer, 0, 0))],
        out_specs=pl.BlockSpec((tm, w), lambda i: (i, 0)),
        out_shape=jax.ShapeDtypeStruct((t, w), BF16),
        scratch_shapes=[pltpu.VMEM((tm + 2 * HALO, w), F32),
                        pltpu.VMEM((SUBLANES - 1, tm + 2 * HALO - SUBLANES, w), F32)],
        compiler_params=_params(dimension_semantics=("arbitrary",)),
        name="conv",
    )(p, p, p, p, conv_w, conv_b, ln_g, ln_b, w_pw)


def _cast_once(w_ref, wb_ref):
    @pl.when(pl.program_id(0) == 0)
    def _():
        wb_ref[...] = w_ref[0].astype(BF16)


def _merge_body(m0, m1, m2, m3, o0, o1, o2, o3, w_ref, y_ref, wb_ref):
    _cast_once(w_ref, wb_ref)
    y = None
    for bi, (m_ref, o_ref) in enumerate(((m0, o0), (m1, o1), (m2, o2), (m3, o3))):
        proj = jnp.dot(o_ref[...], wb_ref[bi], preferred_element_type=F32)
        term = (1.0 + jnp.tanh(0.5 * m_ref[...].astype(F32))) * proj
        y = term if y is None else y + term
    y_ref[...] = (0.5 * y).astype(BF16)


def _merge(p, merge_off, outs, w_branch, layer):
    t = p.shape[0]
    d = w_branch.shape[-1]
    tm = _tile(t, 256, PACKED_ROWS)
    return pl.pallas_call(
        _merge_body,
        grid=(t // tm,),
        in_specs=[_slab(tm, d, merge_off + bi * d) for bi in range(N_BRANCH)]
        + [pl.BlockSpec((tm, BRANCH_W), lambda i: (i, 0))] * N_BRANCH
        + [_resident((1,) + w_branch.shape[1:], lambda i: (layer, 0, 0, 0))],
        out_specs=pl.BlockSpec((tm, d), lambda i: (i, 0)),
        out_shape=jax.ShapeDtypeStruct((t, d), BF16),
        scratch_shapes=[pltpu.VMEM(w_branch.shape[1:], BF16)],
        compiler_params=_params(dimension_semantics=("arbitrary",)),
        name="merge",
    )(p, p, p, p, *outs, w_branch)


def _out_body(y_ref, w_ref, x_ref, g_ref, gate_ref, *rest):
    *rest, wb_ref = rest
    _cast_once(w_ref, wb_ref)
    y = jnp.dot(y_ref[...], wb_ref[...], preferred_element_type=F32)
    yn = y * lax.rsqrt(jnp.mean(y * y, axis=-1, keepdims=True) + EPS) * g_ref[...]
    x_next = x_ref[...] + gate_ref[...] * yn
    if len(rest) == 1:
        rest[0][...] = x_next
    else:
        gn_ref, sh_ref, sc_ref, o_ref, h_ref = rest
        o_ref[...] = x_next
        h_ref[...] = _modulated(x_next, gn_ref[...], sh_ref[...], sc_ref[...])


def _out_proj(y, w_out, layer, x, g_post, gate, next_mod=None):
    t, d = x.shape
    tm = _tile(t, 256, PACKED_ROWS)
    vec = pl.BlockSpec((1, d), lambda i: (0, 0))
    row = pl.BlockSpec((tm, d), lambda i: (i, 0))
    n_next = 0 if next_mod is None else 1
    out = pl.pallas_call(
        _out_body,
        grid=(t // tm,),
        in_specs=[row, _resident((1, d, d), lambda i: (layer, 0, 0)), row, vec, vec] + [vec] * (3 * n_next),
        scratch_shapes=[pltpu.VMEM((d, d), BF16)],
        out_specs=[row] * (1 + n_next),
        out_shape=[jax.ShapeDtypeStruct((t, d), F32)] + [jax.ShapeDtypeStruct((t, d), BF16)] * n_next,
        compiler_params=_params(dimension_semantics=("arbitrary",)),
        name="out_proj",
    )(y, w_out, x, g_post, gate, *(next_mod or ()))
    return out if n_next else (out[0], None)


def kernel(x, c, ctx, c_ctx, w_ada, b_ada, g_pre, g_post, w_in, na_rpb, pool_w, pool_scale,
           q_norm, k_norm, conv_w, conv_b, conv_ln_g, conv_ln_b, conv_pw, w_branch, w_out):
    batch, seq, d = x.shape
    assert batch == 1 and seq % GRID_W == 0
    cn = ctx.shape[1]
    depth = w_ada.shape[0]
    off = _layout(d)

    xs, cs = x[0], ctx[0]
    cvec = jnp.zeros((SUBLANES, d), F32).at[0].set(c[0]).at[1].set(c_ctx)
    ada = _ada(cvec, w_ada, b_ada)

    ones_bd = jnp.asarray(np.kron(np.eye(N_HEADS), np.ones((HEAD_DIM, HEAD_DIM))), BF16)
    rope_tabs = _rope_tables(seq)
    na_bias = _na_bias_tiles(na_rpb)
    row = lambda v: v.reshape(1, -1)
    qn_all = jnp.tile(q_norm, (1, N_HEADS))
    kn_all = jnp.tile(k_norm, (1, GQA_KV_HEADS))

    mod = lambda l, r: tuple(ada[l, r:r + 1, k * d:(k + 1) * d] for k in range(3))
    h = _modulate(xs, row(g_pre[0]), *mod(0, 0)[:2])
    hc = _modulate(cs, row(g_pre[0]), *mod(0, 1)[:2])
    for l in range(depth):
        last = l == depth - 1
        conv_args = (conv_w[l], row(conv_b[l]), row(conv_ln_g[l]), row(conv_ln_b[l]), conv_pw, l)
        gate, gate_c = mod(l, 0)[2], mod(l, 1)[2]
        next_mod = lambda r: None if last else (row(g_pre[l + 1]),) + mod(l + 1, r)[:2]
        qn, kn = row(qn_all[l]), row(kn_all[l])

        pc = _in_proj(hc, w_in, l, KV_COLS if last else None)
        qc_ctx, kq_ctx, vt_ctx = _qk_prep(pc, None if last else off["c_q"], off["c_k"], off["c_v"],
                                          qn, kn, ones_bd, None)

        p = _in_proj(h, w_in, l)
        qc, kc, vt = _qk_prep(p, off["c_q"], off["c_k"], off["c_v"], qn, kn, ones_bd, rope_tabs)
        o_c = _gqa_attn(qc, p, off["c_gate"], kc, vt, kq_ctx, vt_ctx)
        o_a = _na_attn(p, off, pc, na_bias, l)
        o_b = _pool(p, off, pool_w, l, row(pool_scale[l]))
        o_d = _conv(p, off, *conv_args)
        y = _merge(p, off["merge"], (o_a, o_b, o_c, o_d), w_branch, l)
        xs, h = _out_proj(y, w_out, l, xs, row(g_post[l]), gate, next_mod(0))

        if not last:
            o_a_c = _dense_attn(pc, off["a_q"], pc, off["a_gate"],
                                [(pc, off["a_k"], pc, off["a_v"])], False, SCORE_SCALE_LOG2)
            o_c_c = _dense_attn(qc_ctx, 0, pc, off["c_gate"], [(kq_ctx, 0, pc, off["c_v"])], True, 1.0)
            o_b_c = _pool(pc, off, pool_w, l, row(pool_scale[l]))
            o_d_c = _conv(pc, off, *conv_args)
            y_c = _merge(pc, off["merge"], (o_a_c, o_b_c, o_c_c, o_d_c), w_branch, l)
            cs, hc = _out_proj(y_c, w_out, l, cs, row(g_post[l]), gate_c, next_mod(1))
    return xs[None]
```
